```python
import jax, jax.numpy as jnp
from jax import lax
import numpy as np

D_MODEL = 1024
BATCH = 2
SEQ = 16384
DEPTH = 1
DEC_BATCH = 128
DEC_SEQ = 1
PAST_LEN = 8192
PAGE_SIZE = 128

DIL_CONFIGS = ((128, 1), (512, 4), (2048, 16))
N_GROUPS = 3
HEADS_PER_GROUP = 4
N_HEADS_A = N_GROUPS * HEADS_PER_GROUP
HEAD_DIM_A = 64
DIL_BLOCK = 128
N_HEADS_B = 4
HEAD_K_B = 128
HEAD_V_B = 256
GATE_RANK = 16
GATE_TEMP = 16.0
GLA_CHUNK = 64
PEER_HEADS = 8
PEER_KEYS = 128
N_EXPERTS = PEER_KEYS * PEER_KEYS
PEER_QDIM = 256
PEER_TOPK = 16
PEER_BLOCK = 128
W_A = N_HEADS_A * HEAD_DIM_A
W_AO = HEADS_PER_GROUP * HEAD_DIM_A
W_BK = N_HEADS_B * HEAD_K_B
W_BV = N_HEADS_B * HEAD_V_B
IN_SIZES = (W_A, W_A, W_A, W_BK, W_BK, W_BV, W_BV, GATE_RANK, D_MODEL, D_MODEL)
IN_COLS = sum(IN_SIZES)
IN_SPLITS = tuple(int(o) for o in np.cumsum(IN_SIZES)[:-1])
DN_ALPHA = (2 * DEPTH) ** 0.25
DN_BETA = (8 * DEPTH) ** -0.25
LN_EPS = 1e-5

kernel_name = "hybrid_dilated_gla_peer_step"


def layer_norm(x, w=None, b=None):
    xf = x.astype(jnp.float32)
    mu = jnp.mean(xf, axis=-1, keepdims=True)
    var = jnp.mean(jnp.square(xf - mu), axis=-1, keepdims=True)
    y = (xf - mu) * lax.rsqrt(var + LN_EPS)
    if w is not None:
        y = y * w.astype(jnp.float32) + b.astype(jnp.float32)
    return y.astype(x.dtype)


def adaln(c, w_ada, b_ada):
    mod = jax.nn.silu(c) @ w_ada + b_ada
    return [m[:, None, :] for m in jnp.split(mod, 6, axis=-1)]


def modulate(x, shift, scale):
    return layer_norm(x) * (1 + scale) + shift


def alibi_slopes():
    h = jnp.arange(1, N_HEADS_A + 1, dtype=jnp.float32)
    return jnp.exp2(-8.0 * h / N_HEADS_A).reshape(N_GROUPS, HEADS_PER_GROUP)


def softmax_stats(s, valid):
    s = jnp.where(valid, s, -jnp.inf)
    m = jnp.max(s, axis=-1, keepdims=True)
    p = jnp.exp(s - m)
    z = jnp.sum(p, axis=-1, keepdims=True)
    return p / z, (m + jnp.log(z))[..., 0]


def dilated_prompt(q, k, v, slopes, window, dil):
    B, T, H, hd = q.shape
    span = dil * DIL_BLOCK
    t_pad = -(-T // span) * span
    L = t_pad // dil
    nb = L // DIL_BLOCK

    def to_phase(a):
        a = jnp.pad(a, ((0, 0), (0, t_pad - T), (0, 0), (0, 0)))
        a = jnp.transpose(a.reshape(B, L, dil, H, hd), (0, 2, 1, 3, 4))
        return a.reshape(B, dil, nb, DIL_BLOCK, H, hd)

    def with_prev(a):
        prev = jnp.pad(a, ((0, 0), (0, 0), (1, 0), (0, 0), (0, 0), (0, 0)))[:, :, :-1]
        return jnp.concatenate([prev, a], axis=3)

    qb = to_phase(q)
    kc = with_prev(to_phase(k))
    vc = with_prev(to_phase(v))
    s = jnp.einsum('brnqhd,brnkhd->brnhqk', qb, kc).astype(jnp.float32) * (hd ** -0.5)
    qi = jnp.arange(DIL_BLOCK)[:, None]
    ki = jnp.arange(2 * DIL_BLOCK)[None, :]
    steps = qi + DIL_BLOCK - ki
    in_band = (steps >= 0) & (steps <= window // dil)
    has_prev = jnp.arange(nb)[:, None, None] > 0
    valid = in_band[None] & (has_prev | (ki >= DIL_BLOCK)[None])
    bias = -slopes[:, None, None] * (steps * dil).astype(jnp.float32)[None]
    p, lse = softmax_stats(s + bias[None, None, None], valid[None, None, :, None])
    o = jnp.einsum('brnhqk,brnkhd->brnqhd', p.astype(v.dtype), vc)
    o = jnp.transpose(o.reshape(B, dil, L, H, hd), (0, 2, 1, 3, 4)).reshape(B, t_pad, H, hd)[:, :T]
    lse = jnp.transpose(lse, (0, 2, 4, 1, 3)).reshape(B, t_pad, H)[:, :T]
    return o, lse


def dilated_sample(q, k, v, buf, slopes, window, dil):
    S, hd = q.shape[1], q.shape[-1]
    W = buf.shape[1]
    kv_all = jnp.concatenate([buf, jnp.stack([k, v], axis=2)], axis=1)
    steps = jnp.arange(window // dil + 1)
    idx = W + jnp.arange(S)[:, None] - steps[None, :] * dil
    g = jnp.take(kv_all, jnp.maximum(idx, 0), axis=1)
    s = jnp.einsum('bshd,bskhd->bshk', q, g[:, :, :, 0]).astype(jnp.float32) * (hd ** -0.5)
    bias = -slopes[:, None] * (steps * dil).astype(jnp.float32)[None, :]
    p, lse = softmax_stats(s + bias, (idx >= 0)[None, :, None, :])
    o = jnp.einsum('bshk,bskhd->bshd', p.astype(v.dtype), g[:, :, :, 1])
    return o, lse, kv_all[:, S:]


def gla_chunk(S, q, k, v, g):
    q, k, v, g = (a.astype(jnp.float32) for a in (q, k, v, g))
    C = q.shape[1]
    b = jnp.cumsum(g, axis=1)
    causal = jnp.tril(jnp.ones((C, C), dtype=bool))
    diff = b[:, :, None] - b[:, None, :]
    decay = jnp.exp(jnp.where(causal[None, :, :, None, None], diff, -jnp.inf))
    scores = jnp.einsum('bthk,btshk,bshk->bhts', q, decay, k)
    o = jnp.einsum('bthk,bhkv->bthv', q * jnp.exp(b), S) + jnp.einsum('bhts,bshv->bthv', scores, v)
    b_last = b[:, -1]
    S_new = jnp.exp(b_last)[..., None] * S + jnp.einsum('bshk,bshv->bhkv', k * jnp.exp(b_last[:, None] - b), v)
    return o, S_new


def gla_prompt(q, k, v, g):
    B, T, H = q.shape[:3]
    n_chunks = T // GLA_CHUNK

    def to_chunks(a):
        return jnp.moveaxis(a.reshape(B, n_chunks, GLA_CHUNK, *a.shape[2:]), 1, 0)

    S0 = jnp.zeros((B, H, HEAD_K_B, HEAD_V_B), jnp.float32)

    def step(S, xs):
        o, S = gla_chunk(S, *xs)
        return S, o

    S_fin, o = lax.scan(step, S0, (to_chunks(q), to_chunks(k), to_chunks(v), to_chunks(g)))
    return jnp.moveaxis(o, 0, 1).reshape(B, T, H, HEAD_V_B), S_fin


def mixer_project(h, w_in, w_gla_up, b_gla):
    B, T = h.shape[:2]
    qa, ka, va, qb, kb, vb, rb, glr, ga, gb = jnp.split(h @ w_in, IN_SPLITS, axis=-1)
    heads_a = lambda a: a.reshape(B, T, N_HEADS_A, HEAD_DIM_A)
    qb = qb.reshape(B, T, N_HEADS_B, HEAD_K_B) * (HEAD_K_B ** -0.5)
    kb = kb.reshape(B, T, N_HEADS_B, HEAD_K_B)
    vb = vb.reshape(B, T, N_HEADS_B, HEAD_V_B)
    gdec = jax.nn.log_sigmoid((glr @ w_gla_up + b_gla).astype(jnp.float32)) / GATE_TEMP
    gdec = gdec.reshape(B, T, N_HEADS_B, HEAD_K_B)
    return heads_a(qa), heads_a(ka), heads_a(va), qb, kb, vb, gdec, rb, ga, gb


def mixer_merge(o_groups, lse_groups, o_gla, rb, ga, gb, gla_norm_w, w_br_a, w_br_b, w_out):
    B, T = rb.shape[:2]
    dt = rb.dtype
    wts = jax.nn.softmax(jnp.stack(lse_groups), axis=0)
    oa = jnp.sum(wts[..., None] * jnp.stack(o_groups).astype(jnp.float32), axis=0).reshape(B, T, W_AO)
    ob = o_gla * lax.rsqrt(jnp.mean(jnp.square(o_gla), axis=-1, keepdims=True) + LN_EPS)
    ob = ob.reshape(B, T, W_BV).astype(dt) * gla_norm_w * jax.nn.silu(rb)
    merged = jax.nn.sigmoid(ga) * (oa.astype(dt) @ w_br_a) + jax.nn.sigmoid(gb) * (ob @ w_br_b)
    return merged @ w_out


def mixer_prompt(h, w_in, w_gla_up, b_gla, gla_norm_w, w_br_a, w_br_b, w_out):
    qa, ka, va, qb, kb, vb, gdec, rb, ga, gb = mixer_project(h, w_in, w_gla_up, b_gla)
    slopes = alibi_slopes()
    T = h.shape[1]
    o_groups, lse_groups, bufs = [], [], []
    for gi, (window, dil) in enumerate(DIL_CONFIGS):
        hs = slice(gi * HEADS_PER_GROUP, (gi + 1) * HEADS_PER_GROUP)
        o, lse = dilated_prompt(qa[:, :, hs], ka[:, :, hs], va[:, :, hs], slopes[gi], window, dil)
        o_groups.append(o)
        lse_groups.append(lse)
        keep = min(window, T)
        bufs.append(jnp.stack([ka[:, T - keep:, hs], va[:, T - keep:, hs]], axis=2))
    o_gla, S_fin = gla_prompt(qb, kb, vb, gdec)
    out = mixer_merge(o_groups, lse_groups, o_gla, rb, ga, gb, gla_norm_w, w_br_a, w_br_b, w_out)
    return out, bufs, S_fin


def mixer_sample(h, bufs_in, S0, w_in, w_gla_up, b_gla, gla_norm_w, w_br_a, w_br_b, w_out):
    qa, ka, va, qb, kb, vb, gdec, rb, ga, gb = mixer_project(h, w_in, w_gla_up, b_gla)
    slopes = alibi_slopes()
    o_groups, lse_groups, bufs = [], [], []
    for gi, (window, dil) in enumerate(DIL_CONFIGS):
        hs = slice(gi * HEADS_PER_GROUP, (gi + 1) * HEADS_PER_GROUP)
        o, lse, nbuf = dilated_sample(qa[:, :, hs], ka[:, :, hs], va[:, :, hs], bufs_in[gi], slopes[gi], window, dil)
        o_groups.append(o)
        lse_groups.append(lse)
        bufs.append(nbuf)
    o_gla, S_new = gla_chunk(S0.astype(jnp.float32), qb, kb, vb, gdec)
    out = mixer_merge(o_groups, lse_groups, o_gla, rb, ga, gb, gla_norm_w, w_br_a, w_br_b, w_out)
    return out, bufs, S_new


def peer(h, w_pq, peer_k1, peer_k2, peer_u, peer_v):
    shp = h.shape
    hf = h.reshape(-1, D_MODEL)
    n = hf.shape[0]
    n_pad = -(-n // PEER_BLOCK) * PEER_BLOCK
    hb = jnp.pad(hf, ((0, n_pad - n), (0, 0))).reshape(-1, PEER_BLOCK, D_MODEL)
    half = PEER_QDIM // 2

    def one_block(xb):
        qv = (xb @ w_pq).reshape(PEER_BLOCK, PEER_HEADS, PEER_QDIM)
        s1 = jnp.einsum('thd,hkd->thk', qv[..., :half], peer_k1).astype(jnp.float32)
        s2 = jnp.einsum('thd,hkd->thk', qv[..., half:], peer_k2).astype(jnp.float32)
        v1, i1 = lax.top_k(s1, PEER_TOPK)
        v2, i2 = lax.top_k(s2, PEER_TOPK)
        cand = (v1[..., :, None] + v2[..., None, :]).reshape(PEER_BLOCK, PEER_HEADS, PEER_TOPK * PEER_TOPK)
        sc, ci = lax.top_k(cand, PEER_TOPK)
        e = (jnp.take_along_axis(i1, ci // PEER_TOPK, axis=-1) * PEER_KEYS
             + jnp.take_along_axis(i2, ci % PEER_TOPK, axis=-1))
        gw = jax.nn.softmax(sc, axis=-1)
        act = jax.nn.gelu(jnp.einsum('td,thkd->thk', xb, peer_u[e]), approximate=False)
        return jnp.einsum('thk,thkd->td', (gw * act).astype(xb.dtype), peer_v[e])

    y = lax.map(one_block, hb)
    return y.reshape(n_pad, D_MODEL)[:n].reshape(shp)


def decoder_layer(x, c, mixer_fn, w_ada, b_ada, ln1_w, ln1_b, w_pq, peer_k1, peer_k2, peer_u, peer_v, ln2_w, ln2_b):
    sh1, sc1, g1, sh2, sc2, g2 = adaln(c, w_ada, b_ada)
    mix, bufs, S = mixer_fn(modulate(x, sh1, sc1))
    x = layer_norm(DN_ALPHA * x + g1 * mix, ln1_w, ln1_b)
    ff = peer(modulate(x, sh2, sc2), w_pq, peer_k1, peer_k2, peer_u, peer_v)
    x = layer_norm(DN_ALPHA * x + g2 * ff, ln2_w, ln2_b)
    return x, bufs, S


def setup_inputs(seed: int = 0) -> dict:
    key = jax.random.key(seed)
    ks = list(jax.random.split(key, 40))

    def nrm(shape, scale):
        return jax.random.normal(ks.pop(), shape, jnp.float32) * scale

    def kv_cache(window):
        return nrm((DEPTH, DEC_BATCH, min(window, PAST_LEN), 2, HEADS_PER_GROUP, HEAD_DIM_A), 1.0)

    D = D_MODEL
    return {
        "x_prompt": nrm((BATCH, SEQ, D), 1.0),
        "x_sample": nrm((DEC_BATCH, DEC_SEQ, D), 1.0),
        "c_prompt": nrm((BATCH, D), 1.0),
        "c_sample": nrm((DEC_BATCH, D), 1.0),
        "cache_kv_w128": kv_cache(DIL_CONFIGS[0][0]),
        "cache_kv_w512": kv_cache(DIL_CONFIGS[1][0]),
        "cache_kv_w2048": kv_cache(DIL_CONFIGS[2][0]),
        "state_gla": nrm((DEPTH, DEC_BATCH, N_HEADS_B, HEAD_K_B, HEAD_V_B), 0.3),
        "w_ada": nrm((DEPTH, D, 6 * D), 0.5 * D ** -0.5),
        "b_ada": nrm((DEPTH, 6 * D), 0.02),
        "w_in": nrm((DEPTH, D, IN_COLS), D ** -0.5),
        "w_gla_up": nrm((DEPTH, GATE_RANK, W_BK), GATE_RANK ** -0.5),
        "b_gla": nrm((DEPTH, W_BK), 0.1),
        "gla_norm_w": 1.0 + nrm((DEPTH, W_BV), 0.02),
        "w_br_a": nrm((DEPTH, W_AO, D), W_AO ** -0.5),
        "w_br_b": nrm((DEPTH, W_BV, D), W_BV ** -0.5),
        "w_out": nrm((DEPTH, D, D), DN_BETA * D ** -0.5),
        "ln1_w": 1.0 + nrm((DEPTH, D), 0.02),
        "ln1_b": nrm((DEPTH, D), 0.02),
        "w_pq": nrm((DEPTH, D, PEER_HEADS * PEER_QDIM), D ** -0.5),
        "peer_k1": nrm((DEPTH, PEER_HEADS, PEER_KEYS, PEER_QDIM // 2), (PEER_QDIM // 2) ** -0.5),
        "peer_k2": nrm((DEPTH, PEER_HEADS, PEER_KEYS, PEER_QDIM // 2), (PEER_QDIM // 2) ** -0.5),
        "peer_u": nrm((DEPTH, N_EXPERTS, D), D ** -0.5),
        "peer_v": nrm((DEPTH, N_EXPERTS, D), DN_BETA * (PEER_HEADS * PEER_TOPK) ** -0.5),
        "ln2_w": 1.0 + nrm((DEPTH, D), 0.02),
        "ln2_b": nrm((DEPTH, D), 0.02),
    }


def reference(x_prompt, x_sample, c_prompt, c_sample, cache_kv_w128, cache_kv_w512, cache_kv_w2048, state_gla,
              w_ada, b_ada, w_in, w_gla_up, b_gla, gla_norm_w, w_br_a, w_br_b, w_out, ln1_w, ln1_b,
              w_pq, peer_k1, peer_k2, peer_u, peer_v, ln2_w, ln2_b):
    yp, ys = x_prompt, x_sample
    p128, p512, p2048, pgla = [], [], [], []
    s128, s512, s2048, sgla = [], [], [], []
    for l in range(DEPTH):
        mix_w = (w_in[l], w_gla_up[l], b_gla[l], gla_norm_w[l], w_br_a[l], w_br_b[l], w_out[l])
        rest_w = (w_ada[l], b_ada[l], ln1_w[l], ln1_b[l], w_pq[l], peer_k1[l], peer_k2[l],
                  peer_u[l], peer_v[l], ln2_w[l], ln2_b[l])
        yp, bufs_p, S_p = decoder_layer(yp, c_prompt, lambda h: mixer_prompt(h, *mix_w), *rest_w)
        caches_l = (cache_kv_w128[l], cache_kv_w512[l], cache_kv_w2048[l])
        ys, bufs_s, S_s = decoder_layer(ys, c_sample, lambda h: mixer_sample(h, caches_l, state_gla[l], *mix_w), *rest_w)
        p128.append(bufs_p[0]); p512.append(bufs_p[1]); p2048.append(bufs_p[2]); pgla.append(S_p)
        s128.append(bufs_s[0]); s512.append(bufs_s[1]); s2048.append(bufs_s[2]); sgla.append(S_s)
    kv_w128_prompt = jnp.stack(p128)
    kv_w512_prompt = jnp.stack(p512)
    kv_w2048_prompt = jnp.stack(p2048)
    gla_prompt_state = jnp.stack(pgla)
    kv_w128_sample = jnp.stack(s128)
    kv_w512_sample = jnp.stack(s512)
    kv_w2048_sample = jnp.stack(s2048)
    gla_sample_state = jnp.stack(sgla)
    return (yp, ys, kv_w128_prompt, kv_w512_prompt, kv_w2048_prompt, gla_prompt_state,
            kv_w128_sample, kv_w512_sample, kv_w2048_sample, gla_sample_state)
```

```python
import functools

import numpy as np
import jax
import jax.numpy as jnp
from jax import lax
from jax.experimental import pallas as pl
from jax.experimental.pallas import tpu as pltpu

DIL_CONFIGS = ((128, 1), (512, 4), (2048, 16))
N_GROUPS = 3
HEADS_PER_GROUP = 4
HEAD_DIM_A = 64
W_G = HEADS_PER_GROUP * HEAD_DIM_A
W_A = N_GROUPS * W_G
DIL_BLOCK = 128
N_HEADS_B = 4
HEAD_K_B = 128
HEAD_V_B = 256
W_BK = N_HEADS_B * HEAD_K_B
W_BV = N_HEADS_B * HEAD_V_B
GATE_RANK = 16
GATE_TEMP = 16.0
GLR_PAD = 128
GLA_CHUNK = 64
GLA_SUB = 16
PEER_HEADS = 8
PEER_KEYS = 128
PEER_TOPK = 16
TOPK_SHIFT = PEER_TOPK.bit_length() - 1
PEER_PAIRS = PEER_HEADS * PEER_TOPK
LN_EPS = 1e-5

LANES = 128
SUBLANES = 8
VREG_ELEMS = LANES * SUBLANES
VMEM_LIMIT = 56 * 1024 * 1024

PEER_SECTIONS = 2
PEER_GROUP = SUBLANES
PEER_SLOTS = PEER_PAIRS + PEER_SECTIONS * PEER_GROUP
PEER_TB = 64

_HI = lax.Precision.HIGHEST
_NEG = float("-inf")


def _cparams(sem, vmem=VMEM_LIMIT):
    return pltpu.CompilerParams(dimension_semantics=sem, vmem_limit_bytes=vmem)


def _ln(x):
    mu = jnp.mean(x, axis=-1, keepdims=True)
    xc = x - mu
    var = jnp.mean(xc * xc, axis=-1, keepdims=True)
    return xc * lax.rsqrt(var + LN_EPS)


def _bdot(a, b):
    return jnp.dot(a.astype(jnp.bfloat16), b.astype(jnp.bfloat16), preferred_element_type=jnp.float32)


def _bdot_nt(a, b):
    return lax.dot_general(a.astype(jnp.bfloat16), b.astype(jnp.bfloat16), (((1,), (1,)), ((), ())),
                           preferred_element_type=jnp.float32)


def _alibi_slope(head):
    return float(np.exp2(np.float32(-8.0 * (head + 1) / (N_GROUPS * HEADS_PER_GROUP))))


def _ada_kernel(c_ref, w_ref, b_ref, o_ref):
    c = c_ref[...]
    o_ref[...] = _bdot(c * jax.nn.sigmoid(c), w_ref[...]) + b_ref[...]


def _adaln(c, w_ada, b_ada):
    bc, d = c.shape
    ncol = w_ada.shape[1] // d
    return pl.pallas_call(
        _ada_kernel,
        grid=(ncol,),
        in_specs=[pl.BlockSpec((bc, d), lambda j: (0, 0)),
                  pl.BlockSpec((d, d), lambda j: (0, j)),
                  pl.BlockSpec((1, d), lambda j: (0, j))],
        out_specs=pl.BlockSpec((bc, d), lambda j: (0, j)),
        out_shape=jax.ShapeDtypeStruct((bc, ncol * d), jnp.float32),
        compiler_params=_cparams(("arbitrary",)),
        name="adaln",
    )(c, w_ada, b_ada.reshape(1, -1))


def _inproj_kernel(x_ref, sh_ref, sc_ref, wa_ref, wqb_ref, wkb_ref, wvb_ref, wrb_ref, wglr_ref, wga_ref, wgb_ref,
                   wup_ref, bup_ref, a_ref, qb_ref, kb_ref, vb_ref, rb_ref, gd_ref, ga_ref, gb_ref):
    h = (_ln(x_ref[...]) * (1.0 + sc_ref[...]) + sh_ref[...]).astype(jnp.bfloat16)

    def proj(w_ref):
        return jnp.dot(h, w_ref[...], preferred_element_type=jnp.float32)

    a_ref[...] = proj(wa_ref)
    qb_ref[...] = proj(wqb_ref) * (HEAD_K_B ** -0.5)
    kb_ref[...] = proj(wkb_ref)
    vb_ref[...] = proj(wvb_ref)
    rb_ref[...] = proj(wrb_ref)
    ga_ref[...] = proj(wga_ref)
    gb_ref[...] = proj(wgb_ref)
    glr = proj(wglr_ref)
    gate = _bdot(glr, wup_ref[...]) + bup_ref[...]
    gd_ref[...] = jax.nn.log_sigmoid(gate) * (1.0 / GATE_TEMP)


def _inproj(x2d, shift, scale, mod_map, tm, wts):
    n, d = x2d.shape
    rm = shift.shape[1] if shift.shape[1] == 1 else tm
    mod_spec = pl.BlockSpec((None, rm, d), lambda i: mod_map(i) + (0,))
    row = lambda w: pl.BlockSpec((tm, w), lambda i: (i, 0))
    const = lambda a: pl.BlockSpec(a.shape, lambda i: (0,) * a.ndim, pipeline_mode=pl.Buffered(1))
    names = ("wa", "wqb", "wkb", "wvb", "wrb", "wglr", "wga", "wgb", "wup", "bup")
    widths = (3 * W_A, W_BK, W_BK, W_BV, W_BV, W_BK, d, d)
    return pl.pallas_call(
        _inproj_kernel,
        grid=(n // tm,),
        in_specs=[row(d), mod_spec, mod_spec] + [const(wts[k]) for k in names],
        out_specs=[row(w) for w in widths],
        out_shape=[jax.ShapeDtypeStruct((n, w), jnp.float32) for w in widths],
        compiler_params=_cparams(("arbitrary",)),
        name="inproj",
    )(x2d, shift, scale, *[wts[k] for k in names])


def _dil_prompt_kernel(q_ref, kc_ref, kp_ref, vc_ref, vp_ref, o_ref, l_ref, *, group, dil):
    has_prev = pl.program_id(2) > 0
    qi = lax.broadcasted_iota(jnp.int32, (DIL_BLOCK, DIL_BLOCK), 0)
    ki = lax.broadcasted_iota(jnp.int32, (DIL_BLOCK, DIL_BLOCK), 1)
    valid_p = jnp.logical_and(ki >= qi, has_prev)
    valid_c = ki <= qi
    dist_p = ((qi + DIL_BLOCK - ki) * dil).astype(jnp.float32)
    dist_c = ((qi - ki) * dil).astype(jnp.float32)
    outs, lses = [], []
    for hh in range(HEADS_PER_GROUP):
        slope = _alibi_slope(group * HEADS_PER_GROUP + hh)
        sl = slice(hh * HEAD_DIM_A, (hh + 1) * HEAD_DIM_A)
        q = q_ref[:, sl]
        sp = _bdot_nt(q, kp_ref[:, sl]) * (HEAD_DIM_A ** -0.5) - slope * dist_p
        sc = _bdot_nt(q, kc_ref[:, sl]) * (HEAD_DIM_A ** -0.5) - slope * dist_c
        sp = jnp.where(valid_p, sp, _NEG)
        sc = jnp.where(valid_c, sc, _NEG)
        m = jnp.maximum(jnp.max(sp, axis=-1, keepdims=True), jnp.max(sc, axis=-1, keepdims=True))
        pp = jnp.exp(sp - m)
        pc = jnp.exp(sc - m)
        z = jnp.sum(pp, axis=-1, keepdims=True) + jnp.sum(pc, axis=-1, keepdims=True)
        o = (_bdot(pp, vp_ref[:, sl]) + _bdot(pc, vc_ref[:, sl])) / z
        outs.append(o)
        lses.append(jnp.broadcast_to(m + jnp.log(z), (DIL_BLOCK, HEAD_DIM_A)))
    o_ref[...] = jnp.concatenate(outs, axis=-1)
    l_ref[...] = jnp.concatenate(lses, axis=-1)


def _dil_prompt(qkv, group, dil):
    b, t, wq = qkv.shape
    assert t % (dil * DIL_BLOCK) == 0
    l = t // dil
    nb = l // DIL_BLOCK
    view = qkv.reshape(b, l, dil * wq)
    cpb = wq // W_G
    qcol, kcol, vcol = group, N_GROUPS + group, 2 * N_GROUPS + group
    blk = (None, DIL_BLOCK, W_G)
    cur = lambda col: pl.BlockSpec(blk, lambda bi, r, n: (bi, n, r * cpb + col))
    prev = lambda col: pl.BlockSpec(blk, lambda bi, r, n: (bi, jnp.maximum(n - 1, 0), r * cpb + col))
    ospec = pl.BlockSpec(blk, lambda bi, r, n: (bi, n, r))
    o, lse = pl.pallas_call(
        functools.partial(_dil_prompt_kernel, group=group, dil=dil),
        grid=(b, dil, nb),
        in_specs=[cur(qcol), cur(kcol), prev(kcol), cur(vcol), prev(vcol)],
        out_specs=[ospec, ospec],
        out_shape=[jax.ShapeDtypeStruct((b, l, dil * W_G), jnp.float32)] * 2,
        compiler_params=_cparams(("arbitrary", "arbitrary", "arbitrary")),
        name=f"dil_prompt_g{group}",
    )(view, view, view, view, view)
    return o.reshape(b, t, W_G), lse.reshape(b, t, W_G)


def _gla_prompt_kernel(q_ref, k_ref, v_ref, g_ref, o_ref, sfin_ref, s_scr):
    c = pl.program_id(1)
    nchunk = pl.num_programs(1)
    C = GLA_CHUNK

    @pl.when(c == 0)
    def _():
        s_scr[...] = jnp.zeros_like(s_scr)

    ri = lax.broadcasted_iota(jnp.int32, (C, C), 0)
    ci = lax.broadcasted_iota(jnp.int32, (C, C), 1)
    tri = (ri >= ci).astype(jnp.float32)
    bcum = jnp.dot(tri, g_ref[...], precision=_HI, preferred_element_type=jnp.float32)
    row16 = lax.broadcasted_iota(jnp.int32, (GLA_SUB, HEAD_K_B), 0)
    lane16 = lax.broadcasted_iota(jnp.int32, (GLA_SUB, LANES), 1)
    rowc = lax.broadcasted_iota(jnp.int32, (C, HEAD_K_B), 0)
    nsub = C // GLA_SUB
    outs = []
    for h in range(N_HEADS_B):
        ks = slice(h * HEAD_K_B, (h + 1) * HEAD_K_B)
        vs = slice(h * HEAD_V_B, (h + 1) * HEAD_V_B)
        bh = bcum[:, ks]
        qh = q_ref[:, ks]
        kh = k_ref[:, ks]
        vh = v_ref[:, vs]
        sh = s_scr[h]
        o_inter = _bdot(qh * jnp.exp(bh), sh)
        arows = []
        for i in range(nsub):
            r0 = i * GLA_SUB
            bi = bh[r0:r0 + GLA_SUB]
            qi_ = qh[r0:r0 + GLA_SUB]
            ki_ = kh[r0:r0 + GLA_SUB]
            a = jnp.zeros((GLA_SUB, LANES), jnp.float32)
            for s in range(GLA_SUB):
                e = jnp.exp(jnp.where(row16 >= s, bi - bi[s:s + 1], _NEG))
                col = jnp.sum(qi_ * (ki_[s:s + 1] * e), axis=-1, keepdims=True)
                a = jnp.where(lane16 == r0 + s, col, a)
            a = a[:, :C]
            if i > 0:
                b0 = bi[0:1]
                qt = qi_ * jnp.exp(bi - b0)
                kt = kh * jnp.exp(jnp.where(rowc < r0, b0 - bh, _NEG))
                a = a + _bdot_nt(qt, kt)
            arows.append(a)
        amat = jnp.concatenate(arows, axis=0)
        outs.append(o_inter + _bdot(amat, vh))
        bl = bh[C - 1:C]
        kt = kh * jnp.exp(bl - bh)
        dcol = jnp.transpose(jnp.broadcast_to(jnp.exp(bl), (SUBLANES, HEAD_K_B)))[:, 0:1]
        upd = lax.dot_general(kt.astype(jnp.bfloat16), vh.astype(jnp.bfloat16), (((0,), (0,)), ((), ())),
                              preferred_element_type=jnp.float32)
        s_scr[h] = dcol * sh + upd
    o_ref[...] = jnp.concatenate(outs, axis=-1)

    @pl.when(c == nchunk - 1)
    def _():
        sfin_ref[...] = s_scr[...]


def _gla_prompt(qb, kb, vb, gd):
    b, t, _ = qb.shape
    assert t % GLA_CHUNK == 0
    spec = lambda w: pl.BlockSpec((None, GLA_CHUNK, w), lambda bi, c: (bi, c, 0))
    sshape = (N_HEADS_B, HEAD_K_B, HEAD_V_B)
    return pl.pallas_call(
        _gla_prompt_kernel,
        grid=(b, t // GLA_CHUNK),
        in_specs=[spec(W_BK), spec(W_BK), spec(W_BV), spec(W_BK)],
        out_specs=[spec(W_BV), pl.BlockSpec((None,) + sshape, lambda bi, c: (bi, 0, 0, 0))],
        out_shape=[jax.ShapeDtypeStruct((b, t, W_BV), jnp.float32),
                   jax.ShapeDtypeStruct((b,) + sshape, jnp.float32)],
        scratch_shapes=[pltpu.VMEM(sshape, jnp.float32)],
        compiler_params=_cparams(("arbitrary", "arbitrary")),
        name="gla_prompt",
    )(qb, kb, vb, gd)


SAMPLE_SEQS = 8


def _dil_sample_kernel(a_ref, c0_ref, c1_ref, c2_ref, o_ref, l_ref):
    ci = lax.broadcasted_iota(jnp.int32, (W_G, W_G), 0) // HEAD_DIM_A
    cj = lax.broadcasted_iota(jnp.int32, (W_G, W_G), 1) // HEAD_DIM_A
    seg_ones = (ci == cj).astype(jnp.float32)
    lane_head = lax.broadcasted_iota(jnp.int32, (1, W_G), 1) // HEAD_DIM_A
    rowpos = lax.broadcasted_iota(jnp.int32, (DIL_BLOCK, W_G), 0)
    scale = HEAD_DIM_A ** -0.5
    for j in range(SAMPLE_SEQS):
        row = a_ref[j:j + 1, :]
        for g, (cref, (_, dil)) in enumerate(zip((c0_ref, c1_ref, c2_ref), DIL_CONFIGS)):
            slope = jnp.zeros((1, W_G), jnp.float32)
            for hh in range(HEADS_PER_GROUP):
                slope = jnp.where(lane_head == hh, _alibi_slope(g * HEADS_PER_GROUP + hh), slope)
            q = row[:, g * W_G:(g + 1) * W_G]
            knew = row[:, W_A + g * W_G:W_A + (g + 1) * W_G]
            vnew = row[:, 2 * W_A + g * W_G:2 * W_A + (g + 1) * W_G]
            kk = cref[j, :, 0:W_G]
            vv = cref[j, :, W_G:2 * W_G]
            sc = jnp.dot(kk * q, seg_ones, precision=_HI, preferred_element_type=jnp.float32) * scale
            sc = sc - slope * ((DIL_BLOCK - rowpos) * dil).astype(jnp.float32)
            ss = jnp.dot(jnp.broadcast_to(knew * q, (SUBLANES, W_G)), seg_ones, precision=_HI,
                         preferred_element_type=jnp.float32)[0:1] * scale
            m = jnp.maximum(jnp.max(sc, axis=0, keepdims=True), ss)
            p = jnp.exp(sc - m)
            ps = jnp.exp(ss - m)
            z = jnp.sum(p, axis=0, keepdims=True) + ps
            o = (jnp.sum(p * vv, axis=0, keepdims=True) + ps * vnew) / z
            o_ref[j:j + 1, g * W_G:(g + 1) * W_G] = o
            l_ref[j:j + 1, g * W_G:(g + 1) * W_G] = m + jnp.log(z)


def _dil_sample(qkv, caches):
    db = qkv.shape[0]
    views = []
    for cache, (window, dil) in zip(caches, DIL_CONFIGS):
        assert cache.shape[1] == window, "window caches shorter than the window are not supported"
        views.append(cache.reshape(db, DIL_BLOCK, dil * 2 * W_G))
    cspec = pl.BlockSpec((SAMPLE_SEQS, DIL_BLOCK, 2 * W_G), lambda i: (i, 0, 0))
    ospec = pl.BlockSpec((SAMPLE_SEQS, W_A), lambda i: (i, 0))
    return pl.pallas_call(
        _dil_sample_kernel,
        grid=(db // SAMPLE_SEQS,),
        in_specs=[pl.BlockSpec((SAMPLE_SEQS, 3 * W_A), lambda i: (i, 0)), cspec, cspec, cspec],
        out_specs=[ospec, ospec],
        out_shape=[jax.ShapeDtypeStruct((db, W_A), jnp.float32)] * 2,
        compiler_params=_cparams(("arbitrary",)),
        name="dil_sample",
    )(qkv, *views)


def _gla_sample_kernel(q_ref, k_ref, v_ref, g_ref, s0_ref, o_ref, s_ref):
    for h in range(N_HEADS_B):
        ks = slice(h * HEAD_K_B, (h + 1) * HEAD_K_B)
        vs = slice(h * HEAD_V_B, (h + 1) * HEAD_V_B)
        qT = jnp.transpose(q_ref[:, ks])
        kT = jnp.transpose(k_ref[:, ks])
        aT = jnp.transpose(jnp.exp(g_ref[:, ks]))
        for j in range(SAMPLE_SEQS):
            s_new = aT[:, j:j + 1] * s0_ref[j, h] + kT[:, j:j + 1] * v_ref[j:j + 1, vs]
            s_ref[j, h] = s_new
            o_ref[j:j + 1, vs] = jnp.sum(qT[:, j:j + 1] * s_new, axis=0, keepdims=True)


def _gla_sample(qb, kb, vb, gd, s0):
    db = qb.shape[0]
    row = lambda w: pl.BlockSpec((SAMPLE_SEQS, w), lambda i: (i, 0))
    sspec = pl.BlockSpec((SAMPLE_SEQS, N_HEADS_B, HEAD_K_B, HEAD_V_B), lambda i: (i, 0, 0, 0))
    return pl.pallas_call(
        _gla_sample_kernel,
        grid=(db // SAMPLE_SEQS,),
        in_specs=[row(W_BK), row(W_BK), row(W_BV), row(W_BK), sspec],
        out_specs=[row(W_BV), sspec],
        out_shape=[jax.ShapeDtypeStruct((db, W_BV), jnp.float32), jax.ShapeDtypeStruct(s0.shape, jnp.float32)],
        compiler_params=_cparams(("arbitrary",)),
        name="gla_sample",
    )(qb, kb, vb, gd, s0)


def _merge_kernel(o0_ref, o1_ref, o2_ref, l0_ref, l1_ref, l2_ref, og_ref, rb_ref, ga_ref, gb_ref, x_ref,
                  g1_ref, sh2_ref, sc2_ref, gnw_ref, wbra_ref, wbrb_ref, wout_ref, ln1w_ref, ln1b_ref,
                  wpq_ref, k1_ref, k2_ref, x1_ref, h2_ref, s1_ref, s2_ref, *, alpha):
    l0, l1, l2 = l0_ref[...], l1_ref[...], l2_ref[...]
    m = jnp.maximum(jnp.maximum(l0, l1), l2)
    e0, e1, e2 = jnp.exp(l0 - m), jnp.exp(l1 - m), jnp.exp(l2 - m)
    oa = (e0 * o0_ref[...] + e1 * o1_ref[...] + e2 * o2_ref[...]) / (e0 + e1 + e2)
    og = og_ref[...]
    parts = []
    for h in range(N_HEADS_B):
        oh = og[:, h * HEAD_V_B:(h + 1) * HEAD_V_B]
        parts.append(oh * lax.rsqrt(jnp.mean(oh * oh, axis=-1, keepdims=True) + LN_EPS))
    rb = rb_ref[...]
    ob = jnp.concatenate(parts, axis=-1) * gnw_ref[...] * (rb * jax.nn.sigmoid(rb))
    merged = (jax.nn.sigmoid(ga_ref[...]) * _bdot(oa, wbra_ref[...])
              + jax.nn.sigmoid(gb_ref[...]) * _bdot(ob, wbrb_ref[...]))
    mix = _bdot(merged, wout_ref[...])
    x1 = _ln(alpha * x_ref[...] + g1_ref[...] * mix) * ln1w_ref[...] + ln1b_ref[...]
    x1_ref[...] = x1
    h2 = _ln(x1) * (1.0 + sc2_ref[...]) + sh2_ref[...]
    h2_ref[...] = h2
    qv = _bdot(h2, wpq_ref[...]).astype(jnp.bfloat16)
    half = PEER_KEYS
    for h in range(PEER_HEADS):
        base = h * 2 * half
        s1_ref[h] = lax.dot_general(k1_ref[h], qv[:, base:base + half], (((1,), (1,)), ((), ())),
                                    preferred_element_type=jnp.float32)
        s2_ref[h] = lax.dot_general(k2_ref[h], qv[:, base + half:base + 2 * half], (((1,), (1,)), ((), ())),
                                    preferred_element_type=jnp.float32)


def _merge(o_g, l_g, og, rb, ga, gb, x2d, g1, sh2, sc2, mod_map, tm, wts, alpha):
    n, d = x2d.shape
    rm = g1.shape[1] if g1.shape[1] == 1 else tm
    mod_spec = pl.BlockSpec((None, rm, d), lambda i: mod_map(i) + (0,))
    row = lambda w: pl.BlockSpec((tm, w), lambda i: (i, 0))
    const = lambda a: pl.BlockSpec(a.shape, lambda i: (0,) * a.ndim, pipeline_mode=pl.Buffered(1))
    names = ("gnw", "wbra", "wbrb", "wout", "ln1w", "ln1b", "wpq", "k1", "k2")
    sspec = pl.BlockSpec((PEER_HEADS, PEER_KEYS, tm), lambda i: (0, 0, i))
    return pl.pallas_call(
        functools.partial(_merge_kernel, alpha=alpha),
        grid=(n // tm,),
        in_specs=[row(W_G)] * 6 + [row(W_BV), row(W_BV), row(d), row(d), row(d), mod_spec, mod_spec, mod_spec]
                 + [const(wts[k]) for k in names],
        out_specs=[row(d), row(d), sspec, sspec],
        out_shape=[jax.ShapeDtypeStruct((n, d), jnp.float32)] * 2
                  + [jax.ShapeDtypeStruct((PEER_HEADS, PEER_KEYS, n), jnp.float32)] * 2,
        compiler_params=_cparams(("arbitrary",)),
        name="merge",
    )(*o_g, *l_g, og, rb, ga, gb, x2d, g1, sh2, sc2, *[wts[k] for k in names])


def _topk_kernel(s1_ref, s2_ref, e_ref, w_ref, sa, sb, va, vb, ia, ib, cand, sc_scr, ci_scr):
    sub = s1_ref.shape[2]
    shape = (sub, LANES)
    sa[...] = s1_ref[0]
    sb[...] = s2_ref[0]

    def extract(s_scr, n, r, v_out, i_out):
        m = s_scr[0]
        for k in range(1, n):
            m = jnp.maximum(m, s_scr[k])
        idx = jnp.full(shape, n, jnp.int32)
        for k in range(n - 1, -1, -1):
            idx = jnp.where(s_scr[k] == m, k, idx)
        for k in range(n):
            s_scr[k] = jnp.where(idx == k, _NEG, s_scr[k])
        v_out[r] = m
        i_out[r] = idx

    def stage1(r, c):
        extract(sa, PEER_KEYS, r, va, ia)
        extract(sb, PEER_KEYS, r, vb, ib)
        return c

    lax.fori_loop(0, PEER_TOPK, stage1, 0)
    for i in range(PEER_TOPK):
        for j in range(PEER_TOPK):
            cand[i * PEER_TOPK + j] = va[i] + vb[j]

    def stage2(r, c):
        extract(cand, PEER_TOPK * PEER_TOPK, r, sc_scr, ci_scr)
        return c

    lax.fori_loop(0, PEER_TOPK, stage2, 0)
    top = sc_scr[0]
    z = jnp.zeros(shape, jnp.float32)
    for r in range(PEER_TOPK):
        z = z + jnp.exp(sc_scr[r] - top)
    for r in range(PEER_TOPK):
        w_ref[0, 0, r] = jnp.exp(sc_scr[r] - top) / z
        ci = ci_scr[r]
        hi = lax.shift_right_logical(ci, TOPK_SHIFT)
        lo = ci & (PEER_TOPK - 1)
        e1 = jnp.zeros(shape, jnp.int32)
        e2 = jnp.zeros(shape, jnp.int32)
        for i in range(PEER_TOPK):
            e1 = jnp.where(hi == i, ia[i], e1)
            e2 = jnp.where(lo == i, ib[i], e2)
        e_ref[0, 0, r] = e1 * PEER_KEYS + e2


def _topk(s1t, s2t):
    nh, nk, n = s1t.shape
    sub = min(SUBLANES, n // LANES)
    nchunk = n // (sub * LANES)
    v1 = s1t.reshape(nh, nk, n // LANES, LANES)
    v2 = s2t.reshape(nh, nk, n // LANES, LANES)
    ispec = pl.BlockSpec((1, nk, sub, LANES), lambda c, h: (h, 0, c, 0))
    ospec = pl.BlockSpec((1, 1, PEER_TOPK, sub, LANES), lambda c, h: (c, h, 0, 0, 0))
    oshape = (nchunk, nh, PEER_TOPK, sub, LANES)
    key = lambda k, dt: pltpu.VMEM((k, sub, LANES), dt)
    e, w = pl.pallas_call(
        _topk_kernel,
        grid=(nchunk, nh),
        in_specs=[ispec, ispec],
        out_specs=[ospec, ospec],
        out_shape=[jax.ShapeDtypeStruct(oshape, jnp.int32), jax.ShapeDtypeStruct(oshape, jnp.float32)],
        scratch_shapes=[key(nk, jnp.float32), key(nk, jnp.float32),
                        key(PEER_TOPK, jnp.float32), key(PEER_TOPK, jnp.float32),
                        key(PEER_TOPK, jnp.int32), key(PEER_TOPK, jnp.int32),
                        key(PEER_TOPK * PEER_TOPK, jnp.float32), key(PEER_TOPK, jnp.float32),
                        key(PEER_TOPK, jnp.int32)],
        compiler_params=_cparams(("arbitrary", "arbitrary")),
        name="peer_topk",
    )(v1, v2)
    tok = lambda a: jnp.transpose(a, (0, 3, 4, 1, 2)).reshape(n, nh * PEER_TOPK)
    return tok(e), tok(w)


def _load_section(tab_hbm, tab_vmem, sem):
    rows = tab_vmem.shape[0]

    @pl.when(pl.program_id(1) == 0)
    def _():
        start = pl.multiple_of(pl.program_id(0) * rows, SUBLANES)
        cp = pltpu.make_async_copy(tab_hbm.at[pl.ds(start, rows)], tab_vmem, sem)
        cp.start()
        cp.wait()


def _peer_u_kernel(gb_ref, idx_ref, x_ref, tab_hbm, act_ref, tab_vmem, dbuf, sem):
    _load_section(tab_hbm, tab_vmem, sem)
    sec = pl.program_id(0)
    dbuf[...] = jnp.zeros_like(dbuf)
    sub = lax.broadcasted_iota(jnp.int32, (SUBLANES, LANES), 0)

    def fold(a, b, sh):
        keep = (sub & sh) == 0
        u = jnp.where(keep, a, b)
        v = jnp.where(keep, b, a)
        if 2 * sh == SUBLANES:
            w = pltpu.roll(v, sh, axis=0)
        else:
            w = jnp.where(keep, pltpu.roll(v, SUBLANES - sh, axis=0), pltpu.roll(v, sh, axis=0))
        return u + w

    def tok(t, c):
        x = x_ref[pl.ds(pl.multiple_of(t * SUBLANES, SUBLANES), SUBLANES), :]

        def grp(g, c2):
            base = t * PEER_SLOTS + g * PEER_GROUP
            ps = [tab_vmem[pl.ds(pl.multiple_of(idx_ref[base + j], SUBLANES), SUBLANES), :] * x
                  for j in range(PEER_GROUP)]
            while len(ps) > 1:
                sh = len(ps) // 2
                ps = [fold(ps[2 * i], ps[2 * i + 1], sh) for i in range(sh)]
            dbuf[pl.ds(pl.multiple_of(base, SUBLANES), SUBLANES), :] = ps[0]
            return c2

        return lax.fori_loop(gb_ref[t * 4 + sec], gb_ref[t * 4 + sec + 1], grp, c)

    lax.fori_loop(0, PEER_TB, tok, 0)
    ones = jnp.ones((SUBLANES, LANES), jnp.float32)
    act = lax.dot_general(ones, dbuf[...], (((1,), (1,)), ((), ())), precision=_HI,
                          preferred_element_type=jnp.float32)
    act_ref[0] = act[0:1]


def _fold_slot_order():
    pos = [[j] for j in range(PEER_GROUP)]
    sl = [0] * PEER_GROUP
    sh = PEER_GROUP // 2
    groups = pos
    while len(groups) > 1:
        nxt = []
        for i in range(len(groups) // 2):
            for j in groups[2 * i + 1]:
                sl[j] |= sh
            nxt.append(groups[2 * i] + groups[2 * i + 1])
        groups = nxt
        sh //= 2
    return sl


def _peer_v_kernel(gb_ref, idx_ref, w_ref, tab_hbm, out_ref, tab_vmem, sem):
    _load_section(tab_hbm, tab_vmem, sem)
    sec = pl.program_id(0)

    def tok(t, c):
        def grp(g, acc):
            base = t * PEER_SLOTS + g * PEER_GROUP
            for j in range(PEER_GROUP):
                row = tab_vmem[pl.ds(pl.multiple_of(idx_ref[base + j], SUBLANES), SUBLANES), :]
                acc = acc + w_ref[base + j] * row
            return acc

        acc = lax.fori_loop(gb_ref[t * 4 + sec], gb_ref[t * 4 + sec + 1], grp,
                            jnp.zeros((SUBLANES, LANES), jnp.float32))
        out_ref[0, pl.ds(pl.multiple_of(t * SUBLANES, SUBLANES), SUBLANES), :] = acc
        return c

    lax.fori_loop(0, PEER_TB, tok, 0)


def _gate_kernel(a0_ref, a1_ref, gw_ref, w_ref):
    act = a0_ref[...] + a1_ref[...]
    w_ref[...] = gw_ref[...] * (0.5 * act * (1.0 + lax.erf(act * (2.0 ** -0.5))))


def _peer_slots(e, gw, n_experts):
    n = e.shape[0]
    es = n_experts // PEER_SECTIONS
    sec = e // es
    order = jnp.argsort(sec, axis=1, stable=True)
    e_s = jnp.take_along_axis(e, order, axis=1)
    gw_s = jnp.take_along_axis(gw, order, axis=1)
    c0 = jnp.sum(sec == 0, axis=1, keepdims=True).astype(jnp.int32)
    g0 = (c0 + PEER_GROUP - 1) // PEER_GROUP
    g1 = (PEER_PAIRS - c0 + PEER_GROUP - 1) // PEER_GROUP
    slot = jnp.arange(PEER_SLOTS, dtype=jnp.int32)[None, :]
    in0 = slot < g0 * PEER_GROUP
    src = jnp.where(in0, slot, c0 + slot - g0 * PEER_GROUP)
    valid = jnp.where(in0, slot < c0, src < PEER_PAIRS)
    src = jnp.clip(src, 0, PEER_PAIRS - 1)
    perm = np.argsort(np.array(_fold_slot_order()))
    idx = jnp.where(valid, jnp.take_along_axis(e_s, src, axis=1) % es, 0) * SUBLANES
    wgt = jnp.where(valid, jnp.take_along_axis(gw_s, src, axis=1), 0.0)
    idx_k = idx.reshape(n, PEER_SLOTS // PEER_GROUP, PEER_GROUP)[:, :, perm].reshape(n, PEER_SLOTS)
    zero = jnp.zeros_like(g0)
    gb = jnp.concatenate([zero, g0, g0 + g1, zero], axis=1).reshape(-1)
    return gb, idx_k.reshape(-1), idx.reshape(-1), wgt


def _peer(h2, e, gw, u_rows, v_rows):
    n, d = h2.shape
    assert d == VREG_ELEMS and n % PEER_TB == 0
    n_experts = u_rows.shape[0] // SUBLANES
    sec_rows = u_rows.shape[0] // PEER_SECTIONS
    nblk = n // PEER_TB
    gb, idx_u, idx_v, wgt = _peer_slots(e, gw, n_experts)
    smem = lambda w: pl.BlockSpec((PEER_TB * w,), lambda s, i: (i,), memory_space=pltpu.SMEM)
    tab_scratch = [pltpu.VMEM((sec_rows, LANES), jnp.float32)]
    nslot = PEER_TB * PEER_SLOTS
    act = pl.pallas_call(
        _peer_u_kernel,
        grid=(PEER_SECTIONS, nblk),
        in_specs=[smem(4), smem(PEER_SLOTS),
                  pl.BlockSpec((PEER_TB * SUBLANES, LANES), lambda s, i: (i, 0)),
                  pl.BlockSpec(memory_space=pl.ANY)],
        out_specs=pl.BlockSpec((1, 1, nslot), lambda s, i: (s * nblk + i, 0, 0)),
        out_shape=jax.ShapeDtypeStruct((PEER_SECTIONS * nblk, 1, nslot), jnp.float32),
        scratch_shapes=tab_scratch + [pltpu.VMEM((nslot, LANES), jnp.float32), pltpu.SemaphoreType.DMA],
        compiler_params=_cparams(("arbitrary", "arbitrary")),
        name="peer_u",
    )(gb, idx_u, h2.reshape(n * SUBLANES, LANES), u_rows)
    act = act.reshape(PEER_SECTIONS, n, PEER_SLOTS)
    tg = min(n, 512)
    gspec = pl.BlockSpec((tg, PEER_SLOTS), lambda i: (i, 0))
    w = pl.pallas_call(
        _gate_kernel,
        grid=(n // tg,),
        in_specs=[gspec, gspec, gspec],
        out_specs=gspec,
        out_shape=jax.ShapeDtypeStruct((n, PEER_SLOTS), jnp.float32),
        compiler_params=_cparams(("arbitrary",)),
        name="peer_gate",
    )(act[0], act[1], wgt)
    parts = pl.pallas_call(
        _peer_v_kernel,
        grid=(PEER_SECTIONS, nblk),
        in_specs=[smem(4), smem(PEER_SLOTS), smem(PEER_SLOTS), pl.BlockSpec(memory_space=pl.ANY)],
        out_specs=pl.BlockSpec((1, PEER_TB * SUBLANES, LANES), lambda s, i: (s, i, 0)),
        out_shape=jax.ShapeDtypeStruct((PEER_SECTIONS, n * SUBLANES, LANES), jnp.float32),
        scratch_shapes=tab_scratch + [pltpu.SemaphoreType.DMA],
        compiler_params=_cparams(("arbitrary", "arbitrary")),
        name="peer_v",
    )(gb, idx_v, w.reshape(-1), v_rows)
    return parts.reshape(PEER_SECTIONS, n, d)


def _final_kernel(x1_ref, ff_ref, g2_ref, w_ref, b_ref, o_ref, *, alpha):
    ff = ff_ref[0] + ff_ref[1]
    o_ref[...] = _ln(alpha * x1_ref[...] + g2_ref[...] * ff) * w_ref[...] + b_ref[...]


def _final(x1, parts, g2, mod_map, tm, ln2w, ln2b, alpha):
    n, d = x1.shape
    rm = g2.shape[1] if g2.shape[1] == 1 else tm
    vec = pl.BlockSpec((1, d), lambda i: (0, 0))
    return pl.pallas_call(
        functools.partial(_final_kernel, alpha=alpha),
        grid=(n // tm,),
        in_specs=[pl.BlockSpec((tm, d), lambda i: (i, 0)),
                  pl.BlockSpec((PEER_SECTIONS, tm, d), lambda i: (0, i, 0)),
                  pl.BlockSpec((None, rm, d), lambda i: mod_map(i) + (0,)), vec, vec],
        out_specs=pl.BlockSpec((tm, d), lambda i: (i, 0)),
        out_shape=jax.ShapeDtypeStruct((n, d), jnp.float32),
        compiler_params=_cparams(("arbitrary",)),
        name="final_ln",
    )(x1, parts, g2, ln2w, ln2b)


def _layer_weights(l, w_in, w_gla_up, b_gla, gla_norm_w, w_br_a, w_br_b, w_out, ln1_w, ln1_b, w_pq,
                   peer_k1, peer_k2, ln2_w, ln2_b):
    d = w_in.shape[1]
    bf = lambda a: a.astype(jnp.bfloat16)
    sizes = (W_A, W_A, W_A, W_BK, W_BK, W_BV, W_BV, GATE_RANK, d, d)
    offs = np.concatenate([[0], np.cumsum(sizes)])
    col = lambda i, j=None: w_in[l][:, offs[i]:offs[(i if j is None else j) + 1]]
    pad_rank = GLR_PAD - GATE_RANK
    return {
        "wa": bf(col(0, 2)), "wqb": bf(col(3)), "wkb": bf(col(4)), "wvb": bf(col(5)), "wrb": bf(col(6)),
        "wglr": bf(jnp.pad(col(7), ((0, 0), (0, pad_rank)))), "wga": bf(col(8)), "wgb": bf(col(9)),
        "wup": bf(jnp.pad(w_gla_up[l], ((0, pad_rank), (0, 0)))), "bup": b_gla[l].reshape(1, -1),
        "gnw": gla_norm_w[l].reshape(1, -1), "wbra": bf(w_br_a[l]), "wbrb": bf(w_br_b[l]), "wout": bf(w_out[l]),
        "ln1w": ln1_w[l].reshape(1, -1), "ln1b": ln1_b[l].reshape(1, -1), "wpq": bf(w_pq[l]),
        "k1": bf(peer_k1[l]), "k2": bf(peer_k2[l]),
        "ln2w": ln2_w[l].reshape(1, -1), "ln2b": ln2_b[l].reshape(1, -1),
    }


def _ffn_and_norm(x1, h2, s1t, s2t, g2, mod_map, tm, wts, u_rows, v_rows, alpha):
    e, gw = _topk(s1t, s2t)
    parts = _peer(h2, e, gw, u_rows, v_rows)
    return _final(x1, parts, g2, mod_map, tm, wts["ln2w"], wts["ln2b"], alpha)


def kernel(x_prompt, x_sample, c_prompt, c_sample, cache_kv_w128, cache_kv_w512, cache_kv_w2048, state_gla, w_ada, b_ada, w_in, w_gla_up, b_gla, gla_norm_w, w_br_a, w_br_b, w_out, ln1_w, ln1_b, w_pq, peer_k1, peer_k2, peer_u, peer_v, ln2_w, ln2_b):
    depth = w_ada.shape[0]
    b, t, d = x_prompt.shape
    db, ds, _ = x_sample.shape
    assert ds == 1, "the single-token kernels take one new token per sequence"
    alpha = (2 * depth) ** 0.25
    tm_p = 256
    tm_s = db
    yp = x_prompt.reshape(b * t, d)
    ys = x_sample.reshape(db, d)
    caches = (cache_kv_w128, cache_kv_w512, cache_kv_w2048)
    kv_p = [[] for _ in DIL_CONFIGS]
    kv_s = [[] for _ in DIL_CONFIGS]
    gla_p, gla_s = [], []
    nc = b + db
    nc_pad = -(-nc // SUBLANES) * SUBLANES
    c_all = jnp.pad(jnp.concatenate([c_prompt, c_sample], axis=0), ((0, nc_pad - nc), (0, 0)))
    map_p = lambda i: (i // (t // tm_p), 0)
    map_s = lambda i: (0, i)
    for l in range(depth):
        wts = _layer_weights(l, w_in, w_gla_up, b_gla, gla_norm_w, w_br_a, w_br_b, w_out, ln1_w, ln1_b, w_pq,
                             peer_k1, peer_k2, ln2_w, ln2_b)
        u_rows = peer_u[l].reshape(-1, LANES)
        v_rows = peer_v[l].reshape(-1, LANES)
        mod = _adaln(c_all, w_ada[l], b_ada[l])
        mods_p = [m.reshape(b, 1, d) for m in jnp.split(mod[:b], 6, axis=-1)]
        mods_s = [m.reshape(1, db, d) for m in jnp.split(mod[b:b + db], 6, axis=-1)]

        a, qb, kb, vb, rb, gd, ga, gb = _inproj(yp, mods_p[0], mods_p[1], map_p, tm_p, wts)
        a3 = a.reshape(b, t, 3 * W_A)
        o_g, l_g = [], []
        for gi, (window, dil) in enumerate(DIL_CONFIGS):
            o, lse = _dil_prompt(a3, gi, dil)
            o_g.append(o.reshape(b * t, W_G))
            l_g.append(lse.reshape(b * t, W_G))
            keep = min(window, t)
            k_last = a3[:, t - keep:, W_A + gi * W_G:W_A + (gi + 1) * W_G]
            v_last = a3[:, t - keep:, 2 * W_A + gi * W_G:2 * W_A + (gi + 1) * W_G]
            kv_p[gi].append(jnp.stack([k_last, v_last], axis=2).reshape(b, keep, 2, HEADS_PER_GROUP, HEAD_DIM_A))
        og, s_fin = _gla_prompt(qb.reshape(b, t, W_BK), kb.reshape(b, t, W_BK), vb.reshape(b, t, W_BV),
                                gd.reshape(b, t, W_BK))
        gla_p.append(s_fin)
        x1, h2, s1t, s2t = _merge(o_g, l_g, og.reshape(b * t, W_BV), rb, ga, gb, yp, mods_p[2], mods_p[3],
                                  mods_p[4], map_p, tm_p, wts, alpha)
        yp = _ffn_and_norm(x1, h2, s1t, s2t, mods_p[5], map_p, tm_p, wts, u_rows, v_rows, alpha)

        a, qb, kb, vb, rb, gd, ga, gb = _inproj(ys, mods_s[0], mods_s[1], map_s, tm_s, wts)
        layer_caches = [c[l] for c in caches]
        o, lse = _dil_sample(a, layer_caches)
        for gi in range(N_GROUPS):
            new = jnp.stack([a[:, W_A + gi * W_G:W_A + (gi + 1) * W_G],
                             a[:, 2 * W_A + gi * W_G:2 * W_A + (gi + 1) * W_G]], axis=1)
            new = new.reshape(db, 1, 2, HEADS_PER_GROUP, HEAD_DIM_A)
            kv_s[gi].append(jnp.concatenate([layer_caches[gi][:, 1:], new], axis=1))
        og, s_new = _gla_sample(qb, kb, vb, gd, state_gla[l])
        gla_s.append(s_new)
        o_g = [o[:, gi * W_G:(gi + 1) * W_G] for gi in range(N_GROUPS)]
        l_g = [lse[:, gi * W_G:(gi + 1) * W_G] for gi in range(N_GROUPS)]
        x1, h2, s1t, s2t = _merge(o_g, l_g, og, rb, ga, gb, ys, mods_s[2], mods_s[3], mods_s[4], map_s, tm_s,
                                  wts, alpha)
        ys = _ffn_and_norm(x1, h2, s1t, s2t, mods_s[5], map_s, tm_s, wts, u_rows, v_rows, alpha)

    return (yp.reshape(b, t, d), ys.reshape(db, ds, d),
            jnp.stack(kv_p[0]), jnp.stack(kv_p[1]), jnp.stack(kv_p[2]), jnp.stack(gla_p),
            jnp.stack(kv_s[0]), jnp.stack(kv_s[1]), jnp.stack(kv_s[2]), jnp.stack(gla_s))
```

```python
import functools

import numpy as np
import jax
import jax.numpy as jnp
from jax import lax
from jax.experimental import pallas as pl
from jax.experimental.pallas import tpu as pltpu

DIL_CONFIGS = ((128, 1), (512, 4), (2048, 16))
N_GROUPS = 3
HEADS_PER_GROUP = 4
HEAD_DIM_A = 64
W_G = HEADS_PER_GROUP * HEAD_DIM_A
W_A = N_GROUPS * W_G
DIL_BLOCK = 128
N_HEADS_B = 4
HEAD_K_B = 128
HEAD_V_B = 256
W_BK = N_HEADS_B * HEAD_K_B
W_BV = N_HEADS_B * HEAD_V_B
GATE_RANK = 16
GATE_TEMP = 16.0
GLR_PAD = 128
GLA_CHUNK = 64
GLA_SUB = 16
PEER_HEADS = 8
PEER_KEYS = 128
PEER_TOPK = 16
TOPK_SHIFT = PEER_TOPK.bit_length() - 1
PEER_PAIRS = PEER_HEADS * PEER_TOPK
LN_EPS = 1e-5

LANES = 128
SUBLANES = 8
VREG_ELEMS = LANES * SUBLANES
VMEM_LIMIT = 56 * 1024 * 1024

PEER_SECTIONS = 2
PEER_GROUP = SUBLANES
PEER_SLOTS = PEER_PAIRS + PEER_SECTIONS * PEER_GROUP
PEER_GPT = PEER_SLOTS // PEER_GROUP
PEER_GSTRIDE = 32
PEER_TB = 64
PEER_QCAP = PEER_TB * (PEER_PAIRS // PEER_GROUP)
PEER_UNROLL = 8
PEER_CHUNK = 1024

_HI = lax.Precision.HIGHEST
_NEG = float("-inf")


def _cparams(sem, vmem=VMEM_LIMIT):
    return pltpu.CompilerParams(dimension_semantics=sem, vmem_limit_bytes=vmem)


def _ln(x):
    mu = jnp.mean(x, axis=-1, keepdims=True)
    xc = x - mu
    var = jnp.mean(xc * xc, axis=-1, keepdims=True)
    return xc * lax.rsqrt(var + LN_EPS)


def _bdot(a, b):
    return jnp.dot(a.astype(jnp.bfloat16), b.astype(jnp.bfloat16), preferred_element_type=jnp.float32)


def _bdot_nt(a, b):
    return lax.dot_general(a.astype(jnp.bfloat16), b.astype(jnp.bfloat16), (((1,), (1,)), ((), ())),
                           preferred_element_type=jnp.float32)


def _alibi_slope(head):
    return float(np.exp2(np.float32(-8.0 * (head + 1) / (N_GROUPS * HEADS_PER_GROUP))))


def _ada_kernel(c_ref, w_ref, b_ref, o_ref):
    c = c_ref[...]
    o_ref[...] = _bdot(c * jax.nn.sigmoid(c), w_ref[...]) + b_ref[...]


def _adaln(c, w_ada, b_ada):
    bc, d = c.shape
    ncol = w_ada.shape[1] // d
    return pl.pallas_call(
        _ada_kernel,
        grid=(ncol,),
        in_specs=[pl.BlockSpec((bc, d), lambda j: (0, 0)),
                  pl.BlockSpec((d, d), lambda j: (0, j)),
                  pl.BlockSpec((1, d), lambda j: (0, j))],
        out_specs=pl.BlockSpec((bc, d), lambda j: (0, j)),
        out_shape=jax.ShapeDtypeStruct((bc, ncol * d), jnp.float32),
        compiler_params=_cparams(("arbitrary",)),
        name="adaln",
    )(c, w_ada, b_ada.reshape(1, -1))


def _inproj_kernel(x_ref, sh_ref, sc_ref, wa_ref, wqb_ref, wkb_ref, wvb_ref, wrb_ref, wglr_ref, wga_ref, wgb_ref,
                   wup_ref, bup_ref, a_ref, qb_ref, kb_ref, vb_ref, rb_ref, gd_ref, ga_ref, gb_ref):
    h = (_ln(x_ref[...]) * (1.0 + sc_ref[...]) + sh_ref[...]).astype(jnp.bfloat16)

    def proj(w_ref):
        return jnp.dot(h, w_ref[...], preferred_element_type=jnp.float32)

    a_ref[...] = proj(wa_ref)
    qb_ref[...] = proj(wqb_ref) * (HEAD_K_B ** -0.5)
    kb_ref[...] = proj(wkb_ref)
    vb_ref[...] = proj(wvb_ref)
    rb_ref[...] = proj(wrb_ref)
    ga_ref[...] = proj(wga_ref)
    gb_ref[...] = proj(wgb_ref)
    glr = proj(wglr_ref)
    gate = _bdot(glr, wup_ref[...]) + bup_ref[...]
    gd_ref[...] = jax.nn.log_sigmoid(gate) * (1.0 / GATE_TEMP)


def _inproj(x2d, shift, scale, mod_map, tm, wts):
    n, d = x2d.shape
    rm = shift.shape[1] if shift.shape[1] == 1 else tm
    mod_spec = pl.BlockSpec((None, rm, d), lambda i: mod_map(i) + (0,))
    row = lambda w: pl.BlockSpec((tm, w), lambda i: (i, 0))
    const = lambda a: pl.BlockSpec(a.shape, lambda i: (0,) * a.ndim, pipeline_mode=pl.Buffered(1))
    names = ("wa", "wqb", "wkb", "wvb", "wrb", "wglr", "wga", "wgb", "wup", "bup")
    widths = (3 * W_A, W_BK, W_BK, W_BV, W_BV, W_BK, d, d)
    return pl.pallas_call(
        _inproj_kernel,
        grid=(n // tm,),
        in_specs=[row(d), mod_spec, mod_spec] + [const(wts[k]) for k in names],
        out_specs=[row(w) for w in widths],
        out_shape=[jax.ShapeDtypeStruct((n, w), jnp.float32) for w in widths],
        compiler_params=_cparams(("arbitrary",)),
        name="inproj",
    )(x2d, shift, scale, *[wts[k] for k in names])


def _dil_prompt_kernel(q_ref, kc_ref, kp_ref, vc_ref, vp_ref, o_ref, l_ref, *, group, dil):
    has_prev = pl.program_id(2) > 0
    qi = lax.broadcasted_iota(jnp.int32, (DIL_BLOCK, DIL_BLOCK), 0)
    ki = lax.broadcasted_iota(jnp.int32, (DIL_BLOCK, DIL_BLOCK), 1)
    valid_p = jnp.logical_and(ki >= qi, has_prev)
    valid_c = ki <= qi
    dist_p = ((qi + DIL_BLOCK - ki) * dil).astype(jnp.float32)
    dist_c = ((qi - ki) * dil).astype(jnp.float32)
    outs, lses = [], []
    for hh in range(HEADS_PER_GROUP):
        slope = _alibi_slope(group * HEADS_PER_GROUP + hh)
        sl = slice(hh * HEAD_DIM_A, (hh + 1) * HEAD_DIM_A)
        q = q_ref[:, sl]
        sp = _bdot_nt(q, kp_ref[:, sl]) * (HEAD_DIM_A ** -0.5) - slope * dist_p
        sc = _bdot_nt(q, kc_ref[:, sl]) * (HEAD_DIM_A ** -0.5) - slope * dist_c
        sp = jnp.where(valid_p, sp, _NEG)
        sc = jnp.where(valid_c, sc, _NEG)
        m = jnp.maximum(jnp.max(sp, axis=-1, keepdims=True), jnp.max(sc, axis=-1, keepdims=True))
        pp = jnp.exp(sp - m)
        pc = jnp.exp(sc - m)
        z = jnp.sum(pp, axis=-1, keepdims=True) + jnp.sum(pc, axis=-1, keepdims=True)
        o = (_bdot(pp, vp_ref[:, sl]) + _bdot(pc, vc_ref[:, sl])) / z
        outs.append(o)
        lses.append(jnp.broadcast_to(m + jnp.log(z), (DIL_BLOCK, HEAD_DIM_A)))
    o_ref[...] = jnp.concatenate(outs, axis=-1)
    l_ref[...] = jnp.concatenate(lses, axis=-1)


def _dil_prompt(qkv, group, dil):
    b, t, wq = qkv.shape
    assert t % (dil * DIL_BLOCK) == 0
    l = t // dil
    nb = l // DIL_BLOCK
    view = qkv.reshape(b, l, dil * wq)
    cpb = wq // W_G
    qcol, kcol, vcol = group, N_GROUPS + group, 2 * N_GROUPS + group
    blk = (None, DIL_BLOCK, W_G)
    cur = lambda col: pl.BlockSpec(blk, lambda bi, r, n: (bi, n, r * cpb + col))
    prev = lambda col: pl.BlockSpec(blk, lambda bi, r, n: (bi, jnp.maximum(n - 1, 0), r * cpb + col))
    ospec = pl.BlockSpec(blk, lambda bi, r, n: (bi, n, r))
    o, lse = pl.pallas_call(
        functools.partial(_dil_prompt_kernel, group=group, dil=dil),
        grid=(b, dil, nb),
        in_specs=[cur(qcol), cur(kcol), prev(kcol), cur(vcol), prev(vcol)],
        out_specs=[ospec, ospec],
        out_shape=[jax.ShapeDtypeStruct((b, l, dil * W_G), jnp.float32)] * 2,
        compiler_params=_cparams(("arbitrary", "arbitrary", "arbitrary")),
        name=f"dil_prompt_g{group}",
    )(view, view, view, view, view)
    return o.reshape(b, t, W_G), lse.reshape(b, t, W_G)


def _gla_prompt_kernel(q_ref, k_ref, v_ref, g_ref, o_ref, sfin_ref, s_scr):
    c = pl.program_id(1)
    nchunk = pl.num_programs(1)
    C = GLA_CHUNK

    @pl.when(c == 0)
    def _():
        s_scr[...] = jnp.zeros_like(s_scr)

    ri = lax.broadcasted_iota(jnp.int32, (C, C), 0)
    ci = lax.broadcasted_iota(jnp.int32, (C, C), 1)
    tri = (ri >= ci).astype(jnp.float32)
    bcum = jnp.dot(tri, g_ref[...], precision=_HI, preferred_element_type=jnp.float32)
    row16 = lax.broadcasted_iota(jnp.int32, (GLA_SUB, HEAD_K_B), 0)
    lane16 = lax.broadcasted_iota(jnp.int32, (GLA_SUB, LANES), 1)
    rowc = lax.broadcasted_iota(jnp.int32, (C, HEAD_K_B), 0)
    nsub = C // GLA_SUB
    outs = []
    for h in range(N_HEADS_B):
        ks = slice(h * HEAD_K_B, (h + 1) * HEAD_K_B)
        vs = slice(h * HEAD_V_B, (h + 1) * HEAD_V_B)
        bh = bcum[:, ks]
        qh = q_ref[:, ks]
        kh = k_ref[:, ks]
        vh = v_ref[:, vs]
        sh = s_scr[h]
        o_inter = _bdot(qh * jnp.exp(bh), sh)
        arows = []
        for i in range(nsub):
            r0 = i * GLA_SUB
            bi = bh[r0:r0 + GLA_SUB]
            qi_ = qh[r0:r0 + GLA_SUB]
            ki_ = kh[r0:r0 + GLA_SUB]
            a = jnp.zeros((GLA_SUB, LANES), jnp.float32)
            for s in range(GLA_SUB):
                e = jnp.exp(jnp.where(row16 >= s, bi - bi[s:s + 1], _NEG))
                col = jnp.sum(qi_ * (ki_[s:s + 1] * e), axis=-1, keepdims=True)
                a = jnp.where(lane16 == r0 + s, col, a)
            a = a[:, :C]
            if i > 0:
                b0 = bi[0:1]
                qt = qi_ * jnp.exp(bi - b0)
                kt = kh * jnp.exp(jnp.where(rowc < r0, b0 - bh, _NEG))
                a = a + _bdot_nt(qt, kt)
            arows.append(a)
        amat = jnp.concatenate(arows, axis=0)
        outs.append(o_inter + _bdot(amat, vh))
        bl = bh[C - 1:C]
        kt = kh * jnp.exp(bl - bh)
        dcol = jnp.transpose(jnp.broadcast_to(jnp.exp(bl), (SUBLANES, HEAD_K_B)))[:, 0:1]
        upd = lax.dot_general(kt.astype(jnp.bfloat16), vh.astype(jnp.bfloat16), (((0,), (0,)), ((), ())),
                              preferred_element_type=jnp.float32)
        s_scr[h] = dcol * sh + upd
    o_ref[...] = jnp.concatenate(outs, axis=-1)

    @pl.when(c == nchunk - 1)
    def _():
        sfin_ref[...] = s_scr[...]


def _gla_prompt(qb, kb, vb, gd):
    b, t, _ = qb.shape
    assert t % GLA_CHUNK == 0
    spec = lambda w: pl.BlockSpec((None, GLA_CHUNK, w), lambda bi, c: (bi, c, 0))
    sshape = (N_HEADS_B, HEAD_K_B, HEAD_V_B)
    return pl.pallas_call(
        _gla_prompt_kernel,
        grid=(b, t // GLA_CHUNK),
        in_specs=[spec(W_BK), spec(W_BK), spec(W_BV), spec(W_BK)],
        out_specs=[spec(W_BV), pl.BlockSpec((None,) + sshape, lambda bi, c: (bi, 0, 0, 0))],
        out_shape=[jax.ShapeDtypeStruct((b, t, W_BV), jnp.float32),
                   jax.ShapeDtypeStruct((b,) + sshape, jnp.float32)],
        scratch_shapes=[pltpu.VMEM(sshape, jnp.float32)],
        compiler_params=_cparams(("arbitrary", "arbitrary")),
        name="gla_prompt",
    )(qb, kb, vb, gd)


SAMPLE_SEQS = 8


def _dil_sample_kernel(a_ref, c0_ref, c1_ref, c2_ref, o_ref, l_ref):
    ci = lax.broadcasted_iota(jnp.int32, (W_G, W_G), 0) // HEAD_DIM_A
    cj = lax.broadcasted_iota(jnp.int32, (W_G, W_G), 1) // HEAD_DIM_A
    seg_ones = (ci == cj).astype(jnp.float32)
    lane_head = lax.broadcasted_iota(jnp.int32, (1, W_G), 1) // HEAD_DIM_A
    rowpos = lax.broadcasted_iota(jnp.int32, (DIL_BLOCK, W_G), 0)
    scale = HEAD_DIM_A ** -0.5
    for j in range(SAMPLE_SEQS):
        row = a_ref[j:j + 1, :]
        for g, (cref, (_, dil)) in enumerate(zip((c0_ref, c1_ref, c2_ref), DIL_CONFIGS)):
            slope = jnp.zeros((1, W_G), jnp.float32)
            for hh in range(HEADS_PER_GROUP):
                slope = jnp.where(lane_head == hh, _alibi_slope(g * HEADS_PER_GROUP + hh), slope)
            q = row[:, g * W_G:(g + 1) * W_G]
            knew = row[:, W_A + g * W_G:W_A + (g + 1) * W_G]
            vnew = row[:, 2 * W_A + g * W_G:2 * W_A + (g + 1) * W_G]
            kk = cref[j, :, 0:W_G]
            vv = cref[j, :, W_G:2 * W_G]
            sc = jnp.dot(kk * q, seg_ones, precision=_HI, preferred_element_type=jnp.float32) * scale
            sc = sc - slope * ((DIL_BLOCK - rowpos) * dil).astype(jnp.float32)
            ss = jnp.dot(jnp.broadcast_to(knew * q, (SUBLANES, W_G)), seg_ones, precision=_HI,
                         preferred_element_type=jnp.float32)[0:1] * scale
            m = jnp.maximum(jnp.max(sc, axis=0, keepdims=True), ss)
            p = jnp.exp(sc - m)
            ps = jnp.exp(ss - m)
            z = jnp.sum(p, axis=0, keepdims=True) + ps
            o = (jnp.sum(p * vv, axis=0, keepdims=True) + ps * vnew) / z
            o_ref[j:j + 1, g * W_G:(g + 1) * W_G] = o
            l_ref[j:j + 1, g * W_G:(g + 1) * W_G] = m + jnp.log(z)


def _dil_sample(qkv, caches):
    db = qkv.shape[0]
    views = []
    for cache, (window, dil) in zip(caches, DIL_CONFIGS):
        assert cache.shape[1] == window, "window caches shorter than the window are not supported"
        views.append(cache.reshape(db, DIL_BLOCK, dil * 2 * W_G))
    cspec = pl.BlockSpec((SAMPLE_SEQS, DIL_BLOCK, 2 * W_G), lambda i: (i, 0, 0))
    ospec = pl.BlockSpec((SAMPLE_SEQS, W_A), lambda i: (i, 0))
    return pl.pallas_call(
        _dil_sample_kernel,
        grid=(db // SAMPLE_SEQS,),
        in_specs=[pl.BlockSpec((SAMPLE_SEQS, 3 * W_A), lambda i: (i, 0)), cspec, cspec, cspec],
        out_specs=[ospec, ospec],
        out_shape=[jax.ShapeDtypeStruct((db, W_A), jnp.float32)] * 2,
        compiler_params=_cparams(("arbitrary",)),
        name="dil_sample",
    )(qkv, *views)


def _gla_sample_kernel(q_ref, k_ref, v_ref, g_ref, s0_ref, o_ref, s_ref):
    for h in range(N_HEADS_B):
        ks = slice(h * HEAD_K_B, (h + 1) * HEAD_K_B)
        vs = slice(h * HEAD_V_B, (h + 1) * HEAD_V_B)
        qT = jnp.transpose(q_ref[:, ks])
        kT = jnp.transpose(k_ref[:, ks])
        aT = jnp.transpose(jnp.exp(g_ref[:, ks]))
        for j in range(SAMPLE_SEQS):
            s_new = aT[:, j:j + 1] * s0_ref[j, h] + kT[:, j:j + 1] * v_ref[j:j + 1, vs]
            s_ref[j, h] = s_new
            o_ref[j:j + 1, vs] = jnp.sum(qT[:, j:j + 1] * s_new, axis=0, keepdims=True)


def _gla_sample(qb, kb, vb, gd, s0):
    db = qb.shape[0]
    row = lambda w: pl.BlockSpec((SAMPLE_SEQS, w), lambda i: (i, 0))
    sspec = pl.BlockSpec((SAMPLE_SEQS, N_HEADS_B, HEAD_K_B, HEAD_V_B), lambda i: (i, 0, 0, 0))
    return pl.pallas_call(
        _gla_sample_kernel,
        grid=(db // SAMPLE_SEQS,),
        in_specs=[row(W_BK), row(W_BK), row(W_BV), row(W_BK), sspec],
        out_specs=[row(W_BV), sspec],
        out_shape=[jax.ShapeDtypeStruct((db, W_BV), jnp.float32), jax.ShapeDtypeStruct(s0.shape, jnp.float32)],
        compiler_params=_cparams(("arbitrary",)),
        name="gla_sample",
    )(qb, kb, vb, gd, s0)


def _merge_kernel(o0_ref, o1_ref, o2_ref, l0_ref, l1_ref, l2_ref, og_ref, rb_ref, ga_ref, gb_ref, x_ref,
                  g1_ref, sh2_ref, sc2_ref, gnw_ref, wbra_ref, wbrb_ref, wout_ref, ln1w_ref, ln1b_ref,
                  wpq_ref, k1_ref, k2_ref, x1_ref, h2_ref, s1_ref, s2_ref, *, alpha):
    l0, l1, l2 = l0_ref[...], l1_ref[...], l2_ref[...]
    m = jnp.maximum(jnp.maximum(l0, l1), l2)
    e0, e1, e2 = jnp.exp(l0 - m), jnp.exp(l1 - m), jnp.exp(l2 - m)
    oa = (e0 * o0_ref[...] + e1 * o1_ref[...] + e2 * o2_ref[...]) / (e0 + e1 + e2)
    og = og_ref[...]
    parts = []
    for h in range(N_HEADS_B):
        oh = og[:, h * HEAD_V_B:(h + 1) * HEAD_V_B]
        parts.append(oh * lax.rsqrt(jnp.mean(oh * oh, axis=-1, keepdims=True) + LN_EPS))
    rb = rb_ref[...]
    ob = jnp.concatenate(parts, axis=-1) * gnw_ref[...] * (rb * jax.nn.sigmoid(rb))
    merged = (jax.nn.sigmoid(ga_ref[...]) * _bdot(oa, wbra_ref[...])
              + jax.nn.sigmoid(gb_ref[...]) * _bdot(ob, wbrb_ref[...]))
    mix = _bdot(merged, wout_ref[...])
    x1 = _ln(alpha * x_ref[...] + g1_ref[...] * mix) * ln1w_ref[...] + ln1b_ref[...]
    x1_ref[...] = x1
    h2 = _ln(x1) * (1.0 + sc2_ref[...]) + sh2_ref[...]
    h2_ref[...] = h2
    qv = _bdot(h2, wpq_ref[...]).astype(jnp.bfloat16)
    half = PEER_KEYS
    for h in range(PEER_HEADS):
        base = h * 2 * half
        s1_ref[h] = lax.dot_general(k1_ref[h], qv[:, base:base + half], (((1,), (1,)), ((), ())),
                                    preferred_element_type=jnp.float32)
        s2_ref[h] = lax.dot_general(k2_ref[h], qv[:, base + half:base + 2 * half], (((1,), (1,)), ((), ())),
                                    preferred_element_type=jnp.float32)


def _merge(o_g, l_g, og, rb, ga, gb, x2d, g1, sh2, sc2, mod_map, tm, wts, alpha):
    n, d = x2d.shape
    rm = g1.shape[1] if g1.shape[1] == 1 else tm
    mod_spec = pl.BlockSpec((None, rm, d), lambda i: mod_map(i) + (0,))
    row = lambda w: pl.BlockSpec((tm, w), lambda i: (i, 0))
    const = lambda a: pl.BlockSpec(a.shape, lambda i: (0,) * a.ndim, pipeline_mode=pl.Buffered(1))
    names = ("gnw", "wbra", "wbrb", "wout", "ln1w", "ln1b", "wpq", "k1", "k2")
    sspec = pl.BlockSpec((PEER_HEADS, PEER_KEYS, tm), lambda i: (0, 0, i))
    return pl.pallas_call(
        functools.partial(_merge_kernel, alpha=alpha),
        grid=(n // tm,),
        in_specs=[row(W_G)] * 6 + [row(W_BV), row(W_BV), row(d), row(d), row(d), mod_spec, mod_spec, mod_spec]
                 + [const(wts[k]) for k in names],
        out_specs=[row(d), row(d), sspec, sspec],
        out_shape=[jax.ShapeDtypeStruct((n, d), jnp.float32)] * 2
                  + [jax.ShapeDtypeStruct((PEER_HEADS, PEER_KEYS, n), jnp.float32)] * 2,
        compiler_params=_cparams(("arbitrary",)),
        name="merge",
    )(*o_g, *l_g, og, rb, ga, gb, x2d, g1, sh2, sc2, *[wts[k] for k in names])


_CAND_IDS = tuple(i * PEER_TOPK + j for i in range(PEER_TOPK) for j in range(PEER_TOPK)
                  if (i + 1) * (j + 1) <= PEER_TOPK)
_NO_ID = PEER_KEYS * PEER_KEYS


def _topk_kernel(s1_ref, s2_ref, e_ref, w_ref, sa, sb, va, vb, ia, ib, cand, sc_scr, ci_scr):
    sub = s1_ref.shape[2]
    shape = (sub, LANES)
    sa[...] = s1_ref[0]
    sb[...] = s2_ref[0]

    def tree(op, xs):
        xs = list(xs)
        while len(xs) > 1:
            xs = [op(xs[k], xs[k + 1]) for k in range(0, len(xs) - 1, 2)] + (xs[-1:] if len(xs) % 2 else [])
        return xs[0]

    def extract(s_scr, ids, r, v_out, i_out):
        n = len(ids)
        m = tree(jnp.maximum, [s_scr[k] for k in range(n)])
        idx = tree(jnp.minimum, [jnp.where(s_scr[k] == m, ids[k], _NO_ID) for k in range(n)])
        for k in range(n):
            s_scr[k] = jnp.where(idx == ids[k], _NEG, s_scr[k])
        v_out[r] = m
        i_out[r] = idx

    def stage1(r, c):
        extract(sa, range(PEER_KEYS), r, va, ia)
        extract(sb, range(PEER_KEYS), r, vb, ib)
        return c

    lax.fori_loop(0, PEER_TOPK, stage1, 0)
    for k, ci in enumerate(_CAND_IDS):
        cand[k] = va[ci // PEER_TOPK] + vb[ci % PEER_TOPK]

    def stage2(r, c):
        extract(cand, _CAND_IDS, r, sc_scr, ci_scr)
        return c

    lax.fori_loop(0, PEER_TOPK, stage2, 0)
    top = sc_scr[0]
    z = jnp.zeros(shape, jnp.float32)
    for r in range(PEER_TOPK):
        z = z + jnp.exp(sc_scr[r] - top)
    for r in range(PEER_TOPK):
        w_ref[0, 0, r] = jnp.exp(sc_scr[r] - top) / z
        ci = ci_scr[r]
        hi = lax.shift_right_logical(ci, TOPK_SHIFT)
        lo = ci & (PEER_TOPK - 1)
        e1 = jnp.zeros(shape, jnp.int32)
        e2 = jnp.zeros(shape, jnp.int32)
        for i in range(PEER_TOPK):
            e1 = jnp.where(hi == i, ia[i], e1)
            e2 = jnp.where(lo == i, ib[i], e2)
        e_ref[0, 0, r] = e1 * PEER_KEYS + e2


def _topk(s1t, s2t):
    nh, nk, n = s1t.shape
    sub = min(SUBLANES, n // LANES)
    nchunk = n // (sub * LANES)
    v1 = s1t.reshape(nh, nk, n // LANES, LANES)
    v2 = s2t.reshape(nh, nk, n // LANES, LANES)
    ispec = pl.BlockSpec((1, nk, sub, LANES), lambda c, h: (h, 0, c, 0))
    ospec = pl.BlockSpec((1, 1, PEER_TOPK, sub, LANES), lambda c, h: (c, h, 0, 0, 0))
    oshape = (nchunk, nh, PEER_TOPK, sub, LANES)
    key = lambda k, dt: pltpu.VMEM((k, sub, LANES), dt)
    e, w = pl.pallas_call(
        _topk_kernel,
        grid=(nchunk, nh),
        in_specs=[ispec, ispec],
        out_specs=[ospec, ospec],
        out_shape=[jax.ShapeDtypeStruct(oshape, jnp.int32), jax.ShapeDtypeStruct(oshape, jnp.float32)],
        scratch_shapes=[key(nk, jnp.float32), key(nk, jnp.float32),
                        key(PEER_TOPK, jnp.float32), key(PEER_TOPK, jnp.float32),
                        key(PEER_TOPK, jnp.int32), key(PEER_TOPK, jnp.int32),
                        key(len(_CAND_IDS), jnp.float32), key(PEER_TOPK, jnp.float32),
                        key(PEER_TOPK, jnp.int32)],
        compiler_params=_cparams(("arbitrary", "arbitrary")),
        name="peer_topk",
    )(v1, v2)
    tok = lambda a: jnp.transpose(a, (0, 3, 4, 1, 2)).reshape(n, nh * PEER_TOPK)
    return tok(e), tok(w)


def _load_section(tab_hbm, tab_vmem, sem):
    rows = tab_vmem.shape[0]

    @pl.when(pl.program_id(1) == 0)
    def _():
        start = pl.multiple_of(pl.program_id(0) * rows, SUBLANES)
        cp = pltpu.make_async_copy(tab_hbm.at[pl.ds(start, rows)], tab_vmem, sem)
        cp.start()
        cp.wait()


def _grid_step():
    return pl.program_id(0) * pl.num_programs(1) + pl.program_id(1)


def _peer_u_kernel(nq_ref, xrow_ref, sbase_ref, *refs):
    idx_refs = refs[:PEER_GROUP]
    x_ref, tab_hbm, act_ref, tab_vmem, dbuf, sem = refs[PEER_GROUP:]
    _load_section(tab_hbm, tab_vmem, sem)
    step = _grid_step()

    @pl.when(step == 0)
    def _():
        dbuf[...] = jnp.zeros_like(dbuf)

    nq = nq_ref[step]
    sub = lax.broadcasted_iota(jnp.int32, (SUBLANES, LANES), 0)
    order = _fold_slot_order()

    def fold(a, b, sh):
        keep = (sub & sh) == 0
        u = jnp.where(keep, a, b)
        v = jnp.where(keep, b, a)
        if 2 * sh == SUBLANES:
            w = pltpu.roll(v, sh, axis=0)
        else:
            w = jnp.where(keep, pltpu.roll(v, SUBLANES - sh, axis=0), pltpu.roll(v, sh, axis=0))
        return u + w

    def body(i, c):
        for u in range(PEER_UNROLL):
            q = i * PEER_UNROLL + u
            x = x_ref[pl.ds(pl.multiple_of(xrow_ref[q], SUBLANES), SUBLANES), :]
            sb = sbase_ref[q]
            ps = [tab_vmem[pl.ds(pl.multiple_of(idx_refs[order[j]][sb], SUBLANES), SUBLANES), :] * x
                  for j in range(PEER_GROUP)]
            while len(ps) > 1:
                sh = len(ps) // 2
                ps = [fold(ps[2 * k], ps[2 * k + 1], sh) for k in range(sh)]
            dbuf[pl.ds(pl.multiple_of(q * PEER_GROUP, SUBLANES), SUBLANES), :] = ps[0]
        return c

    lax.fori_loop(0, nq // PEER_UNROLL, body, 0)
    nchunk = (nq * PEER_GROUP + PEER_CHUNK - 1) // PEER_CHUNK

    def chunk(c, carry):
        rows = dbuf[pl.ds(pl.multiple_of(c * PEER_CHUNK, PEER_CHUNK), PEER_CHUNK), :]
        act_ref[0, pl.ds(c, 1), :] = jnp.sum(jnp.transpose(rows), axis=0, keepdims=True)
        return carry

    lax.fori_loop(0, nchunk, chunk, 0)

    def zfill(c, carry):
        act_ref[0, pl.ds(c, 1), :] = jnp.zeros((1, PEER_CHUNK), jnp.float32)
        return carry

    lax.fori_loop(nchunk, PEER_QCAP * PEER_GROUP // PEER_CHUNK, zfill, 0)


def _fold_slot_order():
    pos = [[j] for j in range(PEER_GROUP)]
    sl = [0] * PEER_GROUP
    sh = PEER_GROUP // 2
    groups = pos
    while len(groups) > 1:
        nxt = []
        for i in range(len(groups) // 2):
            for j in groups[2 * i + 1]:
                sl[j] |= sh
            nxt.append(groups[2 * i] + groups[2 * i + 1])
        groups = nxt
        sh //= 2
    return sl


def _peer_v_kernel(nq_ref, orow_ref, sbase_ref, *refs):
    idx_refs = refs[:PEER_GROUP]
    w_ref, tab_hbm, out_ref, tab_vmem, wb, acc_scr, sem = refs[PEER_GROUP:]
    _load_section(tab_hbm, tab_vmem, sem)
    nq = nq_ref[_grid_step()]
    acc_scr[...] = jnp.zeros_like(acc_scr)
    def spread(i, carry):
        for k in range(SUBLANES):
            r = i * SUBLANES + k
            rep = jnp.broadcast_to(w_ref[0, pl.ds(r, 1), :], (LANES, LANES))
            wb[pl.ds(pl.multiple_of(r * LANES, LANES), LANES), :] = jnp.transpose(rep)
        return carry

    lax.fori_loop(0, (nq * PEER_GROUP + PEER_CHUNK - 1) // PEER_CHUNK, spread, 0)

    def body(i, carry):
        cur, acc = carry
        for u in range(PEER_UNROLL):
            q = i * PEER_UNROLL + u
            sb = sbase_ref[q]
            row = orow_ref[q]
            terms = [tab_vmem[pl.ds(pl.multiple_of(idx_refs[j][sb], SUBLANES), SUBLANES), :]
                     * wb[pl.ds(q * PEER_GROUP + j, 1), :] for j in range(PEER_GROUP)]
            while len(terms) > 1:
                terms = [terms[2 * k] + terms[2 * k + 1] for k in range(len(terms) // 2)]
            acc_scr[pl.ds(pl.multiple_of(cur, SUBLANES), SUBLANES), :] = acc
            acc = jnp.where(row == cur, acc + terms[0], terms[0])
            cur = row
        return cur, acc

    cur, acc = lax.fori_loop(0, nq // PEER_UNROLL, body,
                             (orow_ref[0], jnp.zeros((SUBLANES, LANES), jnp.float32)))
    acc_scr[pl.ds(pl.multiple_of(cur, SUBLANES), SUBLANES), :] = acc
    out_ref[0] = acc_scr[0:PEER_TB * SUBLANES, :]


def _gate_kernel(a_ref, gw_ref, w_ref):
    act = a_ref[...]
    w_ref[...] = gw_ref[...] * (0.5 * act * (1.0 + lax.erf(act * (2.0 ** -0.5))))


def _peer_slots(e, gw, n_experts):
    n = e.shape[0]
    es = n_experts // PEER_SECTIONS
    sec = e // es
    order = jnp.argsort(sec, axis=1, stable=True)
    e_s = jnp.take_along_axis(e, order, axis=1)
    gw_s = jnp.take_along_axis(gw, order, axis=1)
    c0 = jnp.sum(sec == 0, axis=1, keepdims=True).astype(jnp.int32)
    g0 = (c0 + PEER_GROUP - 1) // PEER_GROUP
    g1 = (PEER_PAIRS - c0 + PEER_GROUP - 1) // PEER_GROUP
    slot = jnp.arange(PEER_SLOTS, dtype=jnp.int32)[None, :]
    in0 = slot < g0 * PEER_GROUP
    src = jnp.where(in0, slot, c0 + slot - g0 * PEER_GROUP)
    valid = jnp.where(in0, slot < c0, src < PEER_PAIRS)
    src = jnp.clip(src, 0, PEER_PAIRS - 1)
    idx = jnp.where(valid, jnp.take_along_axis(e_s, src, axis=1) % es, 0) * SUBLANES
    wgt = jnp.where(valid, jnp.take_along_axis(gw_s, src, axis=1), 0.0)
    return g0, g0 + g1, idx, wgt


def _peer_lists(g_lo, g_hi):
    n = g_lo.shape[0]
    nblk = n // PEER_TB
    g = jnp.arange(PEER_GPT, dtype=jnp.int32)[None, :]
    valid = jnp.logical_and(g >= g_lo, g < g_hi).reshape(nblk, PEER_TB * PEER_GPT)
    cand = jnp.argsort(jnp.logical_not(valid), axis=1, stable=True)[:, :PEER_QCAP].astype(jnp.int32)
    count = jnp.sum(valid, axis=1, keepdims=True).astype(jnp.int32)
    live = jnp.arange(PEER_QCAP, dtype=jnp.int32)[None, :] < count
    t = cand // PEER_GPT
    gg = cand % PEER_GPT
    blk = jnp.arange(nblk, dtype=jnp.int32)[:, None]
    xrow = jnp.where(live, t * SUBLANES, 0)
    sbase = jnp.where(live, t * PEER_GSTRIDE + gg, PEER_GPT - 1)
    orow = jnp.where(live, t * SUBLANES, PEER_TB * SUBLANES)
    gid = jnp.where(live, (blk * PEER_TB + t) * PEER_GPT + gg, n * PEER_GPT)
    nq = (count[:, 0] + PEER_UNROLL - 1) // PEER_UNROLL * PEER_UNROLL
    return nq, xrow, sbase, orow, gid


def _peer(h2, e, gw, u_rows, v_rows):
    n, d = h2.shape
    assert d == VREG_ELEMS and n % PEER_TB == 0
    n_experts = u_rows.shape[0] // SUBLANES
    sec_rows = u_rows.shape[0] // PEER_SECTIONS
    nblk = n // PEER_TB
    nstep = PEER_SECTIONS * nblk
    g0, g01, idx, wgt = _peer_slots(e, gw, n_experts)
    bounds = (jnp.zeros_like(g0), g0, g01)
    lists = [_peer_lists(bounds[s], bounds[s + 1]) for s in range(PEER_SECTIONS)]
    nq, xrow, sbase, orow, gid = [jnp.stack(a).reshape(-1) for a in zip(*lists)]
    gw_rows = jnp.concatenate([wgt.reshape(n * PEER_GPT, PEER_GROUP),
                               jnp.zeros((1, PEER_GROUP), jnp.float32)], axis=0)
    nchunk = PEER_QCAP * PEER_GROUP // PEER_CHUNK
    gw_flat = jnp.take(gw_rows, gid, axis=0).reshape(nstep, nchunk, PEER_CHUNK)

    lst = pl.BlockSpec((PEER_QCAP,), lambda s, i, nq_ref: (s * nblk + i,), memory_space=pltpu.SMEM)
    slots = [pl.BlockSpec((PEER_TB * PEER_GSTRIDE,), lambda s, i, nq_ref: (i,), memory_space=pltpu.SMEM)
             ] * PEER_GROUP
    idx3 = jnp.pad(idx.reshape(n, PEER_GPT, PEER_GROUP), ((0, 0), (0, PEER_GSTRIDE - PEER_GPT), (0, 0)))
    idx_by_slot = [idx3[:, :, j].reshape(-1) for j in range(PEER_GROUP)]
    table = pl.BlockSpec(memory_space=pl.ANY)
    tab_scratch = pltpu.VMEM((sec_rows, LANES), jnp.float32)
    flat_rows = PEER_QCAP * PEER_GROUP
    act = pl.pallas_call(
        _peer_u_kernel,
        grid_spec=pltpu.PrefetchScalarGridSpec(
            num_scalar_prefetch=1,
            grid=(PEER_SECTIONS, nblk),
            in_specs=[lst, lst] + slots + [
                      pl.BlockSpec((PEER_TB * SUBLANES, LANES), lambda s, i, nq_ref: (i, 0)), table],
            out_specs=pl.BlockSpec((1, nchunk, PEER_CHUNK), lambda s, i, nq_ref: (s * nblk + i, 0, 0)),
            scratch_shapes=[tab_scratch, pltpu.VMEM((flat_rows, LANES), jnp.float32), pltpu.SemaphoreType.DMA]),
        out_shape=jax.ShapeDtypeStruct((nstep, nchunk, PEER_CHUNK), jnp.float32),
        compiler_params=_cparams(("arbitrary", "arbitrary")),
        name="peer_u",
    )(nq, xrow, sbase, *idx_by_slot, h2.reshape(n * SUBLANES, LANES), u_rows)
    bg = 8 if nstep % 8 == 0 else nstep
    gspec = pl.BlockSpec((bg, nchunk, PEER_CHUNK), lambda i: (i, 0, 0))
    w = pl.pallas_call(
        _gate_kernel,
        grid=(nstep // bg,),
        in_specs=[gspec, gspec],
        out_specs=gspec,
        out_shape=jax.ShapeDtypeStruct((nstep, nchunk, PEER_CHUNK), jnp.float32),
        compiler_params=_cparams(("arbitrary",)),
        name="peer_gate",
    )(act, gw_flat)
    parts = pl.pallas_call(
        _peer_v_kernel,
        grid_spec=pltpu.PrefetchScalarGridSpec(
            num_scalar_prefetch=1,
            grid=(PEER_SECTIONS, nblk),
            in_specs=[lst, lst] + slots + [
                      pl.BlockSpec((1, flat_rows // LANES, LANES), lambda s, i, nq_ref: (s * nblk + i, 0, 0)),
                      table],
            out_specs=pl.BlockSpec((1, PEER_TB * SUBLANES, LANES), lambda s, i, nq_ref: (s, i, 0)),
            scratch_shapes=[tab_scratch, pltpu.VMEM((flat_rows, LANES), jnp.float32),
                            pltpu.VMEM((PEER_TB * SUBLANES + SUBLANES, LANES), jnp.float32),
                            pltpu.SemaphoreType.DMA]),
        out_shape=jax.ShapeDtypeStruct((PEER_SECTIONS, n * SUBLANES, LANES), jnp.float32),
        compiler_params=_cparams(("arbitrary", "arbitrary")),
        name="peer_v",
    )(nq, orow, sbase, *idx_by_slot, w.reshape(nstep, flat_rows // LANES, LANES), v_rows)
    return parts.reshape(PEER_SECTIONS, n, d)


def _final_kernel(x1_ref, ff_ref, g2_ref, w_ref, b_ref, o_ref, *, alpha):
    ff = ff_ref[0] + ff_ref[1]
    o_ref[...] = _ln(alpha * x1_ref[...] + g2_ref[...] * ff) * w_ref[...] + b_ref[...]


def _final(x1, parts, g2, mod_map, tm, ln2w, ln2b, alpha):
    n, d = x1.shape
    rm = g2.shape[1] if g2.shape[1] == 1 else tm
    vec = pl.BlockSpec((1, d), lambda i: (0, 0))
    return pl.pallas_call(
        functools.partial(_final_kernel, alpha=alpha),
        grid=(n // tm,),
        in_specs=[pl.BlockSpec((tm, d), lambda i: (i, 0)),
                  pl.BlockSpec((PEER_SECTIONS, tm, d), lambda i: (0, i, 0)),
                  pl.BlockSpec((None, rm, d), lambda i: mod_map(i) + (0,)), vec, vec],
        out_specs=pl.BlockSpec((tm, d), lambda i: (i, 0)),
        out_shape=jax.ShapeDtypeStruct((n, d), jnp.float32),
        compiler_params=_cparams(("arbitrary",)),
        name="final_ln",
    )(x1, parts, g2, ln2w, ln2b)


def _layer_weights(l, w_in, w_gla_up, b_gla, gla_norm_w, w_br_a, w_br_b, w_out, ln1_w, ln1_b, w_pq,
                   peer_k1, peer_k2, ln2_w, ln2_b):
    d = w_in.shape[1]
    bf = lambda a: a.astype(jnp.bfloat16)
    sizes = (W_A, W_A, W_A, W_BK, W_BK, W_BV, W_BV, GATE_RANK, d, d)
    offs = np.concatenate([[0], np.cumsum(sizes)])
    col = lambda i, j=None: w_in[l][:, offs[i]:offs[(i if j is None else j) + 1]]
    pad_rank = GLR_PAD - GATE_RANK
    return {
        "wa": bf(col(0, 2)), "wqb": bf(col(3)), "wkb": bf(col(4)), "wvb": bf(col(5)), "wrb": bf(col(6)),
        "wglr": bf(jnp.pad(col(7), ((0, 0), (0, pad_rank)))), "wga": bf(col(8)), "wgb": bf(col(9)),
        "wup": bf(jnp.pad(w_gla_up[l], ((0, pad_rank), (0, 0)))), "bup": b_gla[l].reshape(1, -1),
        "gnw": gla_norm_w[l].reshape(1, -1), "wbra": bf(w_br_a[l]), "wbrb": bf(w_br_b[l]), "wout": bf(w_out[l]),
        "ln1w": ln1_w[l].reshape(1, -1), "ln1b": ln1_b[l].reshape(1, -1), "wpq": bf(w_pq[l]),
        "k1": bf(peer_k1[l]), "k2": bf(peer_k2[l]),
        "ln2w": ln2_w[l].reshape(1, -1), "ln2b": ln2_b[l].reshape(1, -1),
    }


def _ffn_and_norm(x1, h2, s1t, s2t, g2, mod_map, tm, wts, u_rows, v_rows, alpha):
    e, gw = _topk(s1t, s2t)
    parts = _peer(h2, e, gw, u_rows, v_rows)
    return _final(x1, parts, g2, mod_map, tm, wts["ln2w"], wts["ln2b"], alpha)


def kernel(x_prompt, x_sample, c_prompt, c_sample, cache_kv_w128, cache_kv_w512, cache_kv_w2048, state_gla, w_ada, b_ada, w_in, w_gla_up, b_gla, gla_norm_w, w_br_a, w_br_b, w_out, ln1_w, ln1_b, w_pq, peer_k1, peer_k2, peer_u, peer_v, ln2_w, ln2_b):
    depth = w_ada.shape[0]
    b, t, d = x_prompt.shape
    db, ds, _ = x_sample.shape
    assert ds == 1, "the single-token kernels take one new token per sequence"
    alpha = (2 * depth) ** 0.25
    tm_p = 256
    tm_s = db
    yp = x_prompt.reshape(b * t, d)
    ys = x_sample.reshape(db, d)
    caches = (cache_kv_w128, cache_kv_w512, cache_kv_w2048)
    kv_p = [[] for _ in DIL_CONFIGS]
    kv_s = [[] for _ in DIL_CONFIGS]
    gla_p, gla_s = [], []
    nc = b + db
    nc_pad = -(-nc // SUBLANES) * SUBLANES
    c_all = jnp.pad(jnp.concatenate([c_prompt, c_sample], axis=0), ((0, nc_pad - nc), (0, 0)))
    map_p = lambda i: (i // (t // tm_p), 0)
    map_s = lambda i: (0, i)
    for l in range(depth):
        wts = _layer_weights(l, w_in, w_gla_up, b_gla, gla_norm_w, w_br_a, w_br_b, w_out, ln1_w, ln1_b, w_pq,
                             peer_k1, peer_k2, ln2_w, ln2_b)
        u_rows = peer_u[l].reshape(-1, LANES)
        v_rows = peer_v[l].reshape(-1, LANES)
        mod = _adaln(c_all, w_ada[l], b_ada[l])
        mods_p = [m.reshape(b, 1, d) for m in jnp.split(mod[:b], 6, axis=-1)]
        mods_s = [m.reshape(1, db, d) for m in jnp.split(mod[b:b + db], 6, axis=-1)]

        a, qb, kb, vb, rb, gd, ga, gb = _inproj(yp, mods_p[0], mods_p[1], map_p, tm_p, wts)
        a3 = a.reshape(b, t, 3 * W_A)
        o_g, l_g = [], []
        for gi, (window, dil) in enumerate(DIL_CONFIGS):
            o, lse = _dil_prompt(a3, gi, dil)
            o_g.append(o.reshape(b * t, W_G))
            l_g.append(lse.reshape(b * t, W_G))
            keep = min(window, t)
            k_last = a3[:, t - keep:, W_A + gi * W_G:W_A + (gi + 1) * W_G]
            v_last = a3[:, t - keep:, 2 * W_A + gi * W_G:2 * W_A + (gi + 1) * W_G]
            kv_p[gi].append(jnp.stack([k_last, v_last], axis=2).reshape(b, keep, 2, HEADS_PER_GROUP, HEAD_DIM_A))
        og, s_fin = _gla_prompt(qb.reshape(b, t, W_BK), kb.reshape(b, t, W_BK), vb.reshape(b, t, W_BV),
                                gd.reshape(b, t, W_BK))
        gla_p.append(s_fin)
        x1, h2, s1t, s2t = _merge(o_g, l_g, og.reshape(b * t, W_BV), rb, ga, gb, yp, mods_p[2], mods_p[3],
                                  mods_p[4], map_p, tm_p, wts, alpha)
        yp = _ffn_and_norm(x1, h2, s1t, s2t, mods_p[5], map_p, tm_p, wts, u_rows, v_rows, alpha)

        a, qb, kb, vb, rb, gd, ga, gb = _inproj(ys, mods_s[0], mods_s[1], map_s, tm_s, wts)
        layer_caches = [c[l] for c in caches]
        o, lse = _dil_sample(a, layer_caches)
        for gi in range(N_GROUPS):
            new = jnp.stack([a[:, W_A + gi * W_G:W_A + (gi + 1) * W_G],
                             a[:, 2 * W_A + gi * W_G:2 * W_A + (gi + 1) * W_G]], axis=1)
            new = new.reshape(db, 1, 2, HEADS_PER_GROUP, HEAD_DIM_A)
            kv_s[gi].append(jnp.concatenate([layer_caches[gi][:, 1:], new], axis=1))
        og, s_new = _gla_sample(qb, kb, vb, gd, state_gla[l])
        gla_s.append(s_new)
        o_g = [o[:, gi * W_G:(gi + 1) * W_G] for gi in range(N_GROUPS)]
        l_g = [lse[:, gi * W_G:(gi + 1) * W_G] for gi in range(N_GROUPS)]
        x1, h2, s1t, s2t = _merge(o_g, l_g, og, rb, ga, gb, ys, mods_s[2], mods_s[3], mods_s[4], map_s, tm_s,
                                  wts, alpha)
        ys = _ffn_and_norm(x1, h2, s1t, s2t, mods_s[5], map_s, tm_s, wts, u_rows, v_rows, alpha)

    return (yp.reshape(b, t, d), ys.reshape(db, ds, d),
            jnp.stack(kv_p[0]), jnp.stack(kv_p[1]), jnp.stack(kv_p[2]), jnp.stack(gla_p),
            jnp.stack(kv_s[0]), jnp.stack(kv_s[1]), jnp.stack(kv_s[2]), jnp.stack(gla_s))
```

```python
import functools

import numpy as np
import jax
import jax.numpy as jnp
from jax import lax
from jax.experimental import pallas as pl
from jax.experimental.pallas import tpu as pltpu

DIL_CONFIGS = ((128, 1), (512, 4), (2048, 16))
N_GROUPS = 3
HEADS_PER_GROUP = 4
HEAD_DIM_A = 64
W_G = HEADS_PER_GROUP * HEAD_DIM_A
W_A = N_GROUPS * W_G
DIL_BLOCK = 128
N_HEADS_B = 4
HEAD_K_B = 128
HEAD_V_B = 256
W_BK = N_HEADS_B * HEAD_K_B
W_BV = N_HEADS_B * HEAD_V_B
GATE_RANK = 16
GATE_TEMP = 16.0
GLR_PAD = 128
GLA_CHUNK = 64
GLA_SUB = 16
PEER_HEADS = 8
PEER_KEYS = 128
PEER_TOPK = 16
TOPK_SHIFT = PEER_TOPK.bit_length() - 1
PEER_PAIRS = PEER_HEADS * PEER_TOPK
LN_EPS = 1e-5

LANES = 128
SUBLANES = 8
VREG_ELEMS = LANES * SUBLANES
VMEM_LIMIT = 56 * 1024 * 1024

PEER_SECTIONS = 2
PEER_GROUP = SUBLANES
PEER_SLOTS = PEER_PAIRS + PEER_SECTIONS * PEER_GROUP
GROUP_SHIFT = PEER_GROUP.bit_length() - 1
PEER_GPT = PEER_SLOTS // PEER_GROUP
PEER_SPARE_G = PEER_GPT - 1
PEER_GSTRIDE = 32
PEER_GSHIFT = PEER_GSTRIDE.bit_length() - 1
PEER_TB = LANES
TB_SHIFT = PEER_TB.bit_length() - 1
PEER_QCAP = PEER_TB * (PEER_PAIRS // PEER_GROUP)
PEER_UNROLL = 8

_HI = lax.Precision.HIGHEST
_NEG = float("-inf")


def _cparams(sem, vmem=VMEM_LIMIT):
    return pltpu.CompilerParams(dimension_semantics=sem, vmem_limit_bytes=vmem)


def _ln(x):
    mu = jnp.mean(x, axis=-1, keepdims=True)
    xc = x - mu
    var = jnp.mean(xc * xc, axis=-1, keepdims=True)
    return xc * lax.rsqrt(var + LN_EPS)


def _bdot(a, b):
    return jnp.dot(a.astype(jnp.bfloat16), b.astype(jnp.bfloat16), preferred_element_type=jnp.float32)


def _bdot_nt(a, b):
    return lax.dot_general(a.astype(jnp.bfloat16), b.astype(jnp.bfloat16), (((1,), (1,)), ((), ())),
                           preferred_element_type=jnp.float32)


def _alibi_slope(head):
    return float(np.exp2(np.float32(-8.0 * (head + 1) / (N_GROUPS * HEADS_PER_GROUP))))


def _ada_kernel(c_ref, w_ref, b_ref, o_ref):
    c = c_ref[...]
    o_ref[...] = _bdot(c * jax.nn.sigmoid(c), w_ref[...]) + b_ref[...]


def _adaln(c, w_ada, b_ada):
    bc, d = c.shape
    ncol = w_ada.shape[1] // d
    return pl.pallas_call(
        _ada_kernel,
        grid=(ncol,),
        in_specs=[pl.BlockSpec((bc, d), lambda j: (0, 0)),
                  pl.BlockSpec((d, d), lambda j: (0, j)),
                  pl.BlockSpec((1, d), lambda j: (0, j))],
        out_specs=pl.BlockSpec((bc, d), lambda j: (0, j)),
        out_shape=jax.ShapeDtypeStruct((bc, ncol * d), jnp.float32),
        compiler_params=_cparams(("arbitrary",)),
        name="adaln",
    )(c, w_ada, b_ada.reshape(1, -1))


def _inproj_kernel(x_ref, sh_ref, sc_ref, wa_ref, wqb_ref, wkb_ref, wvb_ref, wrb_ref, wglr_ref, wga_ref, wgb_ref,
                   wup_ref, bup_ref, a_ref, qb_ref, kb_ref, vb_ref, rb_ref, gd_ref, ga_ref, gb_ref):
    h = (_ln(x_ref[...]) * (1.0 + sc_ref[...]) + sh_ref[...]).astype(jnp.bfloat16)

    def proj(w_ref):
        return jnp.dot(h, w_ref[...], preferred_element_type=jnp.float32)

    a_ref[...] = proj(wa_ref)
    qb_ref[...] = proj(wqb_ref) * (HEAD_K_B ** -0.5)
    kb_ref[...] = proj(wkb_ref)
    vb_ref[...] = proj(wvb_ref)
    rb_ref[...] = proj(wrb_ref)
    ga_ref[...] = proj(wga_ref)
    gb_ref[...] = proj(wgb_ref)
    glr = proj(wglr_ref)
    gate = _bdot(glr, wup_ref[...]) + bup_ref[...]
    gd_ref[...] = jax.nn.log_sigmoid(gate) * (1.0 / GATE_TEMP)


def _inproj(x2d, shift, scale, mod_map, tm, wts):
    n, d = x2d.shape
    rm = shift.shape[1] if shift.shape[1] == 1 else tm
    mod_spec = pl.BlockSpec((None, rm, d), lambda i: mod_map(i) + (0,))
    row = lambda w: pl.BlockSpec((tm, w), lambda i: (i, 0))
    const = lambda a: pl.BlockSpec(a.shape, lambda i: (0,) * a.ndim, pipeline_mode=pl.Buffered(1))
    names = ("wa", "wqb", "wkb", "wvb", "wrb", "wglr", "wga", "wgb", "wup", "bup")
    widths = (3 * W_A, W_BK, W_BK, W_BV, W_BV, W_BK, d, d)
    return pl.pallas_call(
        _inproj_kernel,
        grid=(n // tm,),
        in_specs=[row(d), mod_spec, mod_spec] + [const(wts[k]) for k in names],
        out_specs=[row(w) for w in widths],
        out_shape=[jax.ShapeDtypeStruct((n, w), jnp.float32) for w in widths],
        compiler_params=_cparams(("arbitrary",)),
        name="inproj",
    )(x2d, shift, scale, *[wts[k] for k in names])


def _dil_prompt_kernel(q_ref, kc_ref, kp_ref, vc_ref, vp_ref, o_ref, l_ref, *, group, dil):
    has_prev = pl.program_id(2) > 0
    qi = lax.broadcasted_iota(jnp.int32, (DIL_BLOCK, DIL_BLOCK), 0)
    ki = lax.broadcasted_iota(jnp.int32, (DIL_BLOCK, DIL_BLOCK), 1)
    valid_p = jnp.logical_and(ki >= qi, has_prev)
    valid_c = ki <= qi
    dist_p = ((qi + DIL_BLOCK - ki) * dil).astype(jnp.float32)
    dist_c = ((qi - ki) * dil).astype(jnp.float32)
    outs, lses = [], []
    for hh in range(HEADS_PER_GROUP):
        slope = _alibi_slope(group * HEADS_PER_GROUP + hh)
        sl = slice(hh * HEAD_DIM_A, (hh + 1) * HEAD_DIM_A)
        q = q_ref[:, sl]
        sp = _bdot_nt(q, kp_ref[:, sl]) * (HEAD_DIM_A ** -0.5) - slope * dist_p
        sc = _bdot_nt(q, kc_ref[:, sl]) * (HEAD_DIM_A ** -0.5) - slope * dist_c
        sp = jnp.where(valid_p, sp, _NEG)
        sc = jnp.where(valid_c, sc, _NEG)
        m = jnp.maximum(jnp.max(sp, axis=-1, keepdims=True), jnp.max(sc, axis=-1, keepdims=True))
        pp = jnp.exp(sp - m)
        pc = jnp.exp(sc - m)
        z = jnp.sum(pp, axis=-1, keepdims=True) + jnp.sum(pc, axis=-1, keepdims=True)
        o = (_bdot(pp, vp_ref[:, sl]) + _bdot(pc, vc_ref[:, sl])) / z
        outs.append(o)
        lses.append(jnp.broadcast_to(m + jnp.log(z), (DIL_BLOCK, HEAD_DIM_A)))
    o_ref[...] = jnp.concatenate(outs, axis=-1)
    l_ref[...] = jnp.concatenate(lses, axis=-1)


def _dil_prompt(qkv, group, dil):
    b, t, wq = qkv.shape
    assert t % (dil * DIL_BLOCK) == 0
    l = t // dil
    nb = l // DIL_BLOCK
    view = qkv.reshape(b, l, dil * wq)
    cpb = wq // W_G
    qcol, kcol, vcol = group, N_GROUPS + group, 2 * N_GROUPS + group
    blk = (None, DIL_BLOCK, W_G)
    cur = lambda col: pl.BlockSpec(blk, lambda bi, r, n: (bi, n, r * cpb + col))
    prev = lambda col: pl.BlockSpec(blk, lambda bi, r, n: (bi, jnp.maximum(n - 1, 0), r * cpb + col))
    ospec = pl.BlockSpec(blk, lambda bi, r, n: (bi, n, r))
    o, lse = pl.pallas_call(
        functools.partial(_dil_prompt_kernel, group=group, dil=dil),
        grid=(b, dil, nb),
        in_specs=[cur(qcol), cur(kcol), prev(kcol), cur(vcol), prev(vcol)],
        out_specs=[ospec, ospec],
        out_shape=[jax.ShapeDtypeStruct((b, l, dil * W_G), jnp.float32)] * 2,
        compiler_params=_cparams(("arbitrary", "arbitrary", "arbitrary")),
        name=f"dil_prompt_g{group}",
    )(view, view, view, view, view)
    return o.reshape(b, t, W_G), lse.reshape(b, t, W_G)


def _gla_prompt_kernel(q_ref, k_ref, v_ref, g_ref, o_ref, sfin_ref, s_scr):
    c = pl.program_id(1)
    nchunk = pl.num_programs(1)
    C = GLA_CHUNK

    @pl.when(c == 0)
    def _():
        s_scr[...] = jnp.zeros_like(s_scr)

    ri = lax.broadcasted_iota(jnp.int32, (C, C), 0)
    ci = lax.broadcasted_iota(jnp.int32, (C, C), 1)
    tri = (ri >= ci).astype(jnp.float32)
    bcum = jnp.dot(tri, g_ref[...], precision=_HI, preferred_element_type=jnp.float32)
    row16 = lax.broadcasted_iota(jnp.int32, (GLA_SUB, HEAD_K_B), 0)
    lane16 = lax.broadcasted_iota(jnp.int32, (GLA_SUB, LANES), 1)
    rowc = lax.broadcasted_iota(jnp.int32, (C, HEAD_K_B), 0)
    nsub = C // GLA_SUB
    outs = []
    for h in range(N_HEADS_B):
        ks = slice(h * HEAD_K_B, (h + 1) * HEAD_K_B)
        vs = slice(h * HEAD_V_B, (h + 1) * HEAD_V_B)
        bh = bcum[:, ks]
        qh = q_ref[:, ks]
        kh = k_ref[:, ks]
        vh = v_ref[:, vs]
        sh = s_scr[h]
        o_inter = _bdot(qh * jnp.exp(bh), sh)
        arows = []
        for i in range(nsub):
            r0 = i * GLA_SUB
            bi = bh[r0:r0 + GLA_SUB]
            qi_ = qh[r0:r0 + GLA_SUB]
            ki_ = kh[r0:r0 + GLA_SUB]
            a = jnp.zeros((GLA_SUB, LANES), jnp.float32)
            for s in range(GLA_SUB):
                e = jnp.exp(jnp.where(row16 >= s, bi - bi[s:s + 1], _NEG))
                col = jnp.sum(qi_ * (ki_[s:s + 1] * e), axis=-1, keepdims=True)
                a = jnp.where(lane16 == r0 + s, col, a)
            a = a[:, :C]
            if i > 0:
                b0 = bi[0:1]
                qt = qi_ * jnp.exp(bi - b0)
                kt = kh * jnp.exp(jnp.where(rowc < r0, b0 - bh, _NEG))
                a = a + _bdot_nt(qt, kt)
            arows.append(a)
        amat = jnp.concatenate(arows, axis=0)
        outs.append(o_inter + _bdot(amat, vh))
        bl = bh[C - 1:C]
        kt = kh * jnp.exp(bl - bh)
        dcol = jnp.transpose(jnp.broadcast_to(jnp.exp(bl), (SUBLANES, HEAD_K_B)))[:, 0:1]
        upd = lax.dot_general(kt.astype(jnp.bfloat16), vh.astype(jnp.bfloat16), (((0,), (0,)), ((), ())),
                              preferred_element_type=jnp.float32)
        s_scr[h] = dcol * sh + upd
    o_ref[...] = jnp.concatenate(outs, axis=-1)

    @pl.when(c == nchunk - 1)
    def _():
        sfin_ref[...] = s_scr[...]


def _gla_prompt(qb, kb, vb, gd):
    b, t, _ = qb.shape
    assert t % GLA_CHUNK == 0
    spec = lambda w: pl.BlockSpec((None, GLA_CHUNK, w), lambda bi, c: (bi, c, 0))
    sshape = (N_HEADS_B, HEAD_K_B, HEAD_V_B)
    return pl.pallas_call(
        _gla_prompt_kernel,
        grid=(b, t // GLA_CHUNK),
        in_specs=[spec(W_BK), spec(W_BK), spec(W_BV), spec(W_BK)],
        out_specs=[spec(W_BV), pl.BlockSpec((None,) + sshape, lambda bi, c: (bi, 0, 0, 0))],
        out_shape=[jax.ShapeDtypeStruct((b, t, W_BV), jnp.float32),
                   jax.ShapeDtypeStruct((b,) + sshape, jnp.float32)],
        scratch_shapes=[pltpu.VMEM(sshape, jnp.float32)],
        compiler_params=_cparams(("arbitrary", "arbitrary")),
        name="gla_prompt",
    )(qb, kb, vb, gd)


SAMPLE_SEQS = 8


def _dil_sample_kernel(a_ref, c0_ref, c1_ref, c2_ref, o_ref, l_ref):
    ci = lax.broadcasted_iota(jnp.int32, (W_G, W_G), 0) // HEAD_DIM_A
    cj = lax.broadcasted_iota(jnp.int32, (W_G, W_G), 1) // HEAD_DIM_A
    seg_ones = (ci == cj).astype(jnp.float32)
    lane_head = lax.broadcasted_iota(jnp.int32, (1, W_G), 1) // HEAD_DIM_A
    rowpos = lax.broadcasted_iota(jnp.int32, (DIL_BLOCK, W_G), 0)
    scale = HEAD_DIM_A ** -0.5
    for j in range(SAMPLE_SEQS):
        row = a_ref[j:j + 1, :]
        for g, (cref, (_, dil)) in enumerate(zip((c0_ref, c1_ref, c2_ref), DIL_CONFIGS)):
            slope = jnp.zeros((1, W_G), jnp.float32)
            for hh in range(HEADS_PER_GROUP):
                slope = jnp.where(lane_head == hh, _alibi_slope(g * HEADS_PER_GROUP + hh), slope)
            q = row[:, g * W_G:(g + 1) * W_G]
            knew = row[:, W_A + g * W_G:W_A + (g + 1) * W_G]
            vnew = row[:, 2 * W_A + g * W_G:2 * W_A + (g + 1) * W_G]
            kk = cref[j, :, 0:W_G]
            vv = cref[j, :, W_G:2 * W_G]
            sc = jnp.dot(kk * q, seg_ones, precision=_HI, preferred_element_type=jnp.float32) * scale
            sc = sc - slope * ((DIL_BLOCK - rowpos) * dil).astype(jnp.float32)
            ss = jnp.dot(jnp.broadcast_to(knew * q, (SUBLANES, W_G)), seg_ones, precision=_HI,
                         preferred_element_type=jnp.float32)[0:1] * scale
            m = jnp.maximum(jnp.max(sc, axis=0, keepdims=True), ss)
            p = jnp.exp(sc - m)
            ps = jnp.exp(ss - m)
            z = jnp.sum(p, axis=0, keepdims=True) + ps
            o = (jnp.sum(p * vv, axis=0, keepdims=True) + ps * vnew) / z
            o_ref[j:j + 1, g * W_G:(g + 1) * W_G] = o
            l_ref[j:j + 1, g * W_G:(g + 1) * W_G] = m + jnp.log(z)


def _dil_sample(qkv, caches):
    db = qkv.shape[0]
    views = []
    for cache, (window, dil) in zip(caches, DIL_CONFIGS):
        assert cache.shape[1] == window, "window caches shorter than the window are not supported"
        views.append(cache.reshape(db, DIL_BLOCK, dil * 2 * W_G))
    cspec = pl.BlockSpec((SAMPLE_SEQS, DIL_BLOCK, 2 * W_G), lambda i: (i, 0, 0))
    ospec = pl.BlockSpec((SAMPLE_SEQS, W_A), lambda i: (i, 0))
    return pl.pallas_call(
        _dil_sample_kernel,
        grid=(db // SAMPLE_SEQS,),
        in_specs=[pl.BlockSpec((SAMPLE_SEQS, 3 * W_A), lambda i: (i, 0)), cspec, cspec, cspec],
        out_specs=[ospec, ospec],
        out_shape=[jax.ShapeDtypeStruct((db, W_A), jnp.float32)] * 2,
        compiler_params=_cparams(("arbitrary",)),
        name="dil_sample",
    )(qkv, *views)


def _gla_sample_kernel(q_ref, k_ref, v_ref, g_ref, s0_ref, o_ref, s_ref):
    for h in range(N_HEADS_B):
        ks = slice(h * HEAD_K_B, (h + 1) * HEAD_K_B)
        vs = slice(h * HEAD_V_B, (h + 1) * HEAD_V_B)
        qT = jnp.transpose(q_ref[:, ks])
        kT = jnp.transpose(k_ref[:, ks])
        aT = jnp.transpose(jnp.exp(g_ref[:, ks]))
        for j in range(SAMPLE_SEQS):
            s_new = aT[:, j:j + 1] * s0_ref[j, h] + kT[:, j:j + 1] * v_ref[j:j + 1, vs]
            s_ref[j, h] = s_new
            o_ref[j:j + 1, vs] = jnp.sum(qT[:, j:j + 1] * s_new, axis=0, keepdims=True)


def _gla_sample(qb, kb, vb, gd, s0):
    db = qb.shape[0]
    row = lambda w: pl.BlockSpec((SAMPLE_SEQS, w), lambda i: (i, 0))
    sspec = pl.BlockSpec((SAMPLE_SEQS, N_HEADS_B, HEAD_K_B, HEAD_V_B), lambda i: (i, 0, 0, 0))
    return pl.pallas_call(
        _gla_sample_kernel,
        grid=(db // SAMPLE_SEQS,),
        in_specs=[row(W_BK), row(W_BK), row(W_BV), row(W_BK), sspec],
        out_specs=[row(W_BV), sspec],
        out_shape=[jax.ShapeDtypeStruct((db, W_BV), jnp.float32), jax.ShapeDtypeStruct(s0.shape, jnp.float32)],
        compiler_params=_cparams(("arbitrary",)),
        name="gla_sample",
    )(qb, kb, vb, gd, s0)


def _merge_kernel(o0_ref, o1_ref, o2_ref, l0_ref, l1_ref, l2_ref, og_ref, rb_ref, ga_ref, gb_ref, x_ref,
                  g1_ref, sh2_ref, sc2_ref, gnw_ref, wbra_ref, wbrb_ref, wout_ref, ln1w_ref, ln1b_ref,
                  wpq_ref, k1_ref, k2_ref, x1_ref, h2_ref, s1_ref, s2_ref, *, alpha):
    l0, l1, l2 = l0_ref[...], l1_ref[...], l2_ref[...]
    m = jnp.maximum(jnp.maximum(l0, l1), l2)
    e0, e1, e2 = jnp.exp(l0 - m), jnp.exp(l1 - m), jnp.exp(l2 - m)
    oa = (e0 * o0_ref[...] + e1 * o1_ref[...] + e2 * o2_ref[...]) / (e0 + e1 + e2)
    og = og_ref[...]
    parts = []
    for h in range(N_HEADS_B):
        oh = og[:, h * HEAD_V_B:(h + 1) * HEAD_V_B]
        parts.append(oh * lax.rsqrt(jnp.mean(oh * oh, axis=-1, keepdims=True) + LN_EPS))
    rb = rb_ref[...]
    ob = jnp.concatenate(parts, axis=-1) * gnw_ref[...] * (rb * jax.nn.sigmoid(rb))
    merged = (jax.nn.sigmoid(ga_ref[...]) * _bdot(oa, wbra_ref[...])
              + jax.nn.sigmoid(gb_ref[...]) * _bdot(ob, wbrb_ref[...]))
    mix = _bdot(merged, wout_ref[...])
    x1 = _ln(alpha * x_ref[...] + g1_ref[...] * mix) * ln1w_ref[...] + ln1b_ref[...]
    x1_ref[...] = x1
    h2 = _ln(x1) * (1.0 + sc2_ref[...]) + sh2_ref[...]
    h2_ref[...] = h2
    qv = _bdot(h2, wpq_ref[...]).astype(jnp.bfloat16)
    half = PEER_KEYS
    for h in range(PEER_HEADS):
        base = h * 2 * half
        s1_ref[h] = lax.dot_general(k1_ref[h], qv[:, base:base + half], (((1,), (1,)), ((), ())),
                                    preferred_element_type=jnp.float32)
        s2_ref[h] = lax.dot_general(k2_ref[h], qv[:, base + half:base + 2 * half], (((1,), (1,)), ((), ())),
                                    preferred_element_type=jnp.float32)


def _merge(o_g, l_g, og, rb, ga, gb, x2d, g1, sh2, sc2, mod_map, tm, wts, alpha):
    n, d = x2d.shape
    rm = g1.shape[1] if g1.shape[1] == 1 else tm
    mod_spec = pl.BlockSpec((None, rm, d), lambda i: mod_map(i) + (0,))
    row = lambda w: pl.BlockSpec((tm, w), lambda i: (i, 0))
    const = lambda a: pl.BlockSpec(a.shape, lambda i: (0,) * a.ndim, pipeline_mode=pl.Buffered(1))
    names = ("gnw", "wbra", "wbrb", "wout", "ln1w", "ln1b", "wpq", "k1", "k2")
    sspec = pl.BlockSpec((PEER_HEADS, PEER_KEYS, tm), lambda i: (0, 0, i))
    return pl.pallas_call(
        functools.partial(_merge_kernel, alpha=alpha),
        grid=(n // tm,),
        in_specs=[row(W_G)] * 6 + [row(W_BV), row(W_BV), row(d), row(d), row(d), mod_spec, mod_spec, mod_spec]
                 + [const(wts[k]) for k in names],
        out_specs=[row(d), row(d), sspec, sspec],
        out_shape=[jax.ShapeDtypeStruct((n, d), jnp.float32)] * 2
                  + [jax.ShapeDtypeStruct((PEER_HEADS, PEER_KEYS, n), jnp.float32)] * 2,
        compiler_params=_cparams(("arbitrary",)),
        name="merge",
    )(*o_g, *l_g, og, rb, ga, gb, x2d, g1, sh2, sc2, *[wts[k] for k in names])


_CAND_IDS = tuple(i * PEER_TOPK + j for i in range(PEER_TOPK) for j in range(PEER_TOPK)
                  if (i + 1) * (j + 1) <= PEER_TOPK)
_NO_ID = PEER_KEYS * PEER_KEYS


def _topk_kernel(s1_ref, s2_ref, e_ref, w_ref, sa, sb, va, vb, ia, ib, cand, sc_scr, ci_scr):
    sub = s1_ref.shape[2]
    shape = (sub, LANES)
    sa[...] = s1_ref[0]
    sb[...] = s2_ref[0]

    def tree(op, xs):
        xs = list(xs)
        while len(xs) > 1:
            xs = [op(xs[k], xs[k + 1]) for k in range(0, len(xs) - 1, 2)] + (xs[-1:] if len(xs) % 2 else [])
        return xs[0]

    def extract(s_scr, ids, r, v_out, i_out):
        n = len(ids)
        m = tree(jnp.maximum, [s_scr[k] for k in range(n)])
        idx = tree(jnp.minimum, [jnp.where(s_scr[k] == m, ids[k], _NO_ID) for k in range(n)])
        for k in range(n):
            s_scr[k] = jnp.where(idx == ids[k], _NEG, s_scr[k])
        v_out[r] = m
        i_out[r] = idx

    def stage1(r, c):
        extract(sa, range(PEER_KEYS), r, va, ia)
        extract(sb, range(PEER_KEYS), r, vb, ib)
        return c

    lax.fori_loop(0, PEER_TOPK, stage1, 0)
    for k, ci in enumerate(_CAND_IDS):
        cand[k] = va[ci // PEER_TOPK] + vb[ci % PEER_TOPK]

    def stage2(r, c):
        extract(cand, _CAND_IDS, r, sc_scr, ci_scr)
        return c

    lax.fori_loop(0, PEER_TOPK, stage2, 0)
    top = sc_scr[0]
    z = jnp.zeros(shape, jnp.float32)
    for r in range(PEER_TOPK):
        z = z + jnp.exp(sc_scr[r] - top)
    for r in range(PEER_TOPK):
        w_ref[0, 0, r] = jnp.exp(sc_scr[r] - top) / z
        ci = ci_scr[r]
        hi = lax.shift_right_logical(ci, TOPK_SHIFT)
        lo = ci & (PEER_TOPK - 1)
        e1 = jnp.zeros(shape, jnp.int32)
        e2 = jnp.zeros(shape, jnp.int32)
        for i in range(PEER_TOPK):
            e1 = jnp.where(hi == i, ia[i], e1)
            e2 = jnp.where(lo == i, ib[i], e2)
        e_ref[0, 0, r] = e1 * PEER_KEYS + e2


def _topk(s1t, s2t):
    nh, nk, n = s1t.shape
    sub = min(SUBLANES, n // LANES)
    nchunk = n // (sub * LANES)
    v1 = s1t.reshape(nh, nk, n // LANES, LANES)
    v2 = s2t.reshape(nh, nk, n // LANES, LANES)
    ispec = pl.BlockSpec((1, nk, sub, LANES), lambda c, h: (h, 0, c, 0))
    ospec = pl.BlockSpec((1, 1, PEER_TOPK, sub, LANES), lambda c, h: (c, h, 0, 0, 0))
    oshape = (nchunk, nh, PEER_TOPK, sub, LANES)
    key = lambda k, dt: pltpu.VMEM((k, sub, LANES), dt)
    e, w = pl.pallas_call(
        _topk_kernel,
        grid=(nchunk, nh),
        in_specs=[ispec, ispec],
        out_specs=[ospec, ospec],
        out_shape=[jax.ShapeDtypeStruct(oshape, jnp.int32), jax.ShapeDtypeStruct(oshape, jnp.float32)],
        scratch_shapes=[key(nk, jnp.float32), key(nk, jnp.float32),
                        key(PEER_TOPK, jnp.float32), key(PEER_TOPK, jnp.float32),
                        key(PEER_TOPK, jnp.int32), key(PEER_TOPK, jnp.int32),
                        key(len(_CAND_IDS), jnp.float32), key(PEER_TOPK, jnp.float32),
                        key(PEER_TOPK, jnp.int32)],
        compiler_params=_cparams(("arbitrary", "arbitrary")),
        name="peer_topk",
    )(v1, v2)
    return e, w


def _load_section(tab_hbm, tab_vmem, sem):
    rows = tab_vmem.shape[0]

    @pl.when(pl.program_id(1) == 0)
    def _():
        start = pl.multiple_of(pl.program_id(0) * rows, SUBLANES)
        cp = pltpu.make_async_copy(tab_hbm.at[pl.ds(start, rows)], tab_vmem, sem)
        cp.start()
        cp.wait()


def _grid_step():
    return pl.program_id(0) * pl.num_programs(1) + pl.program_id(1)


def _peer_u_kernel(nq_ref, glo_ref, ghi_ref, xrow_ref, gt_ref, *refs):
    idx_refs = refs[:PEER_GROUP]
    x_ref, tab_hbm, act_ref, tab_vmem, dbuf, sem = refs[PEER_GROUP:]
    _load_section(tab_hbm, tab_vmem, sem)
    step = _grid_step()

    @pl.when(step == 0)
    def _():
        dbuf[...] = jnp.zeros_like(dbuf)

    nq = nq_ref[step]
    sub = lax.broadcasted_iota(jnp.int32, (SUBLANES, LANES), 0)
    order = _fold_slot_order()

    def fold(a, b, sh):
        keep = (sub & sh) == 0
        u = jnp.where(keep, a, b)
        v = jnp.where(keep, b, a)
        if 2 * sh == SUBLANES:
            w = pltpu.roll(v, sh, axis=0)
        else:
            w = jnp.where(keep, pltpu.roll(v, SUBLANES - sh, axis=0), pltpu.roll(v, sh, axis=0))
        return u + w

    def body(i, c):
        for u in range(PEER_UNROLL):
            q = i * PEER_UNROLL + u
            gt = gt_ref[q]
            x = x_ref[pl.ds(pl.multiple_of(xrow_ref[q], SUBLANES), SUBLANES), :]
            ps = [tab_vmem[pl.ds(pl.multiple_of(idx_refs[order[j]][gt], SUBLANES), SUBLANES), :] * x
                  for j in range(PEER_GROUP)]
            while len(ps) > 1:
                sh = len(ps) // 2
                ps = [fold(ps[2 * k], ps[2 * k + 1], sh) for k in range(sh)]
            dbuf[pl.ds(pl.multiple_of(gt * PEER_GROUP, SUBLANES), SUBLANES), :] = ps[0]
        return c

    lax.fori_loop(0, nq // PEER_UNROLL, body, 0)
    rows_per_g = PEER_GROUP * PEER_TB
    glo, ghi = glo_ref[step], ghi_ref[step]

    def zfill(g, carry):
        act_ref[0, pl.ds(g, 1), :] = jnp.zeros((1, rows_per_g), jnp.float32)
        return carry

    def reduce(g, carry):
        sums = [jnp.sum(jnp.transpose(dbuf[pl.ds(g * rows_per_g + j, PEER_TB, stride=PEER_GROUP), :]),
                        axis=0, keepdims=True) for j in range(PEER_GROUP)]
        act_ref[0, pl.ds(g, 1), :] = jnp.concatenate(sums, axis=-1)
        return carry

    lax.fori_loop(0, glo, zfill, 0)
    lax.fori_loop(glo, ghi, reduce, 0)
    lax.fori_loop(ghi, PEER_GPT, zfill, 0)


def _fold_slot_order():
    pos = [[j] for j in range(PEER_GROUP)]
    sl = [0] * PEER_GROUP
    sh = PEER_GROUP // 2
    groups = pos
    while len(groups) > 1:
        nxt = []
        for i in range(len(groups) // 2):
            for j in groups[2 * i + 1]:
                sl[j] |= sh
            nxt.append(groups[2 * i] + groups[2 * i + 1])
        groups = nxt
        sh //= 2
    return sl


def _peer_v_kernel(nq_ref, glo_ref, ghi_ref, xrow_ref, gt_ref, *refs):
    idx_refs = refs[:PEER_GROUP]
    w_ref, tab_hbm, out_ref, tab_vmem, wb, acc_scr, sem = refs[PEER_GROUP:]
    _load_section(tab_hbm, tab_vmem, sem)
    step = _grid_step()
    nq = nq_ref[step]
    rows_per_g = PEER_GROUP * PEER_TB

    @pl.when(step == 0)
    def _():
        wb[pl.ds(PEER_SPARE_G * rows_per_g, rows_per_g), :] = jnp.zeros((rows_per_g, LANES), jnp.float32)

    acc_scr[...] = jnp.zeros_like(acc_scr)

    def spread(g, carry):
        for j in range(PEER_GROUP):
            r = g * PEER_GROUP + j
            rep = jnp.broadcast_to(w_ref[0, pl.ds(r, 1), :], (LANES, LANES))
            wb[pl.ds(pl.multiple_of(r * PEER_TB, PEER_TB), PEER_TB), :] = jnp.transpose(rep)
        return carry

    lax.fori_loop(glo_ref[step], ghi_ref[step], spread, 0)
    trash = PEER_TB * SUBLANES

    def body(i, carry):
        cur, acc = carry
        for u in range(PEER_UNROLL):
            q = i * PEER_UNROLL + u
            gt = gt_ref[q]
            row0 = gt + lax.shift_right_logical(gt, TB_SHIFT) * ((PEER_GROUP - 1) * PEER_TB)
            terms = [tab_vmem[pl.ds(pl.multiple_of(idx_refs[j][gt], SUBLANES), SUBLANES), :]
                     * wb[pl.ds(row0 + j * PEER_TB, 1), :] for j in range(PEER_GROUP)]
            while len(terms) > 1:
                terms = [terms[2 * k] + terms[2 * k + 1] for k in range(len(terms) // 2)]
            row = xrow_ref[q]
            acc_scr[pl.ds(pl.multiple_of(cur, SUBLANES), SUBLANES), :] = acc
            acc = jnp.where(row == cur, acc + terms[0], terms[0])
            cur = row
        return cur, acc

    cur, acc = lax.fori_loop(0, nq // PEER_UNROLL, body,
                             (jnp.int32(trash), jnp.zeros((SUBLANES, LANES), jnp.float32)))
    acc_scr[pl.ds(pl.multiple_of(cur, SUBLANES), SUBLANES), :] = acc
    out_ref[0] = acc_scr[0:trash, :]


def _gate_kernel(a_ref, gw_ref, w_ref):
    act = a_ref[...]
    w_ref[...] = gw_ref[...] * (0.5 * act * (1.0 + lax.erf(act * (2.0 ** -0.5))))


def _router_kernel(e_ref, w_ref, idx_ref, gw_ref, g0_ref, g01_ref, pos_scr, loc_scr, *, sec_experts):
    shape = e_ref.shape[3:]
    zero = jnp.zeros(shape, jnp.int32)
    r0, r1 = zero, zero
    for p in range(PEER_PAIRS):
        e = e_ref[0, p // PEER_TOPK, p % PEER_TOPK]
        upper = e >= sec_experts
        loc_scr[p] = (e & (sec_experts - 1)) * SUBLANES
        pos_scr[p] = jnp.where(upper, r1 + PEER_SLOTS, r0)
        r0 = r0 + jnp.where(upper, 0, 1)
        r1 = r1 + jnp.where(upper, 1, 0)
    g0 = lax.shift_right_logical(r0 + (PEER_GROUP - 1), GROUP_SHIFT)
    g1 = lax.shift_right_logical(r1 + (PEER_GROUP - 1), GROUP_SHIFT)
    g0_ref[0] = g0
    g01_ref[0] = g0 + g1
    rebase = g0 * PEER_GROUP - PEER_SLOTS
    for p in range(PEER_PAIRS):
        pos = pos_scr[p]
        pos_scr[p] = jnp.where(pos >= PEER_SLOTS, pos + rebase, pos)

    def place(g, carry):
        for j in range(PEER_GROUP):
            slot = g * PEER_GROUP + j
            iv = zero
            wv = jnp.zeros(shape, jnp.float32)
            for p in range(PEER_PAIRS):
                hit = pos_scr[p] == slot
                iv = jnp.where(hit, loc_scr[p], iv)
                wv = jnp.where(hit, w_ref[0, p // PEER_TOPK, p % PEER_TOPK], wv)
            idx_ref[j, 0, g] = iv
            gw_ref[0, slot] = wv
        return carry

    lax.fori_loop(0, PEER_GPT, place, 0)
    for j in range(PEER_GROUP):
        for g in range(PEER_GPT, PEER_GSTRIDE):
            idx_ref[j, 0, g] = zero


def _router(e, w, n_experts):
    nchunk, nh, k, sub, _ = e.shape
    sec_experts = n_experts // PEER_SECTIONS
    assert sec_experts & (sec_experts - 1) == 0 and PEER_SECTIONS == 2
    ispec = pl.BlockSpec((1, nh, k, sub, LANES), lambda c: (c, 0, 0, 0, 0))
    bspec = pl.BlockSpec((1, sub, LANES), lambda c: (c, 0, 0))
    scr = lambda dt: pltpu.VMEM((PEER_PAIRS, sub, LANES), dt)
    return pl.pallas_call(
        functools.partial(_router_kernel, sec_experts=sec_experts),
        grid=(nchunk,),
        in_specs=[ispec, ispec],
        out_specs=[pl.BlockSpec((PEER_GROUP, 1, PEER_GSTRIDE, sub, LANES), lambda c: (0, c, 0, 0, 0)),
                   pl.BlockSpec((1, PEER_SLOTS, sub, LANES), lambda c: (c, 0, 0, 0)), bspec, bspec],
        out_shape=[jax.ShapeDtypeStruct((PEER_GROUP, nchunk, PEER_GSTRIDE, sub, LANES), jnp.int32),
                   jax.ShapeDtypeStruct((nchunk, PEER_SLOTS, sub, LANES), jnp.float32),
                   jax.ShapeDtypeStruct((nchunk, sub, LANES), jnp.int32),
                   jax.ShapeDtypeStruct((nchunk, sub, LANES), jnp.int32)],
        scratch_shapes=[scr(jnp.int32), scr(jnp.int32)],
        compiler_params=_cparams(("arbitrary",)),
        name="peer_router",
    )(e, w)


def _peer_lists(g_lo, g_hi):
    nblk = g_lo.shape[0]
    g = jnp.arange(PEER_GPT, dtype=jnp.int32)[None, None, :]
    valid = jnp.logical_and(g >= g_lo[:, :, None], g < g_hi[:, :, None]).reshape(nblk, PEER_TB * PEER_GPT)
    cand = jnp.argsort(jnp.logical_not(valid), axis=1, stable=True)[:, :PEER_QCAP].astype(jnp.int32)
    count = jnp.sum(valid, axis=1, keepdims=True).astype(jnp.int32)
    live = jnp.arange(PEER_QCAP, dtype=jnp.int32)[None, :] < count
    t = cand // PEER_GPT
    xrow = jnp.where(live, t * SUBLANES, 0)
    xrow = jnp.where(live, xrow, jnp.max(xrow, axis=1, keepdims=True))
    gt = jnp.where(live, (cand % PEER_GPT) * PEER_TB + t, PEER_SPARE_G * PEER_TB)
    nq = (count[:, 0] + PEER_UNROLL - 1) // PEER_UNROLL * PEER_UNROLL
    return nq, jnp.min(g_lo, axis=1), jnp.max(g_hi, axis=1), xrow, gt


def _peer(h2, e, gw, u_rows, v_rows):
    n, d = h2.shape
    assert d == VREG_ELEMS and n % PEER_TB == 0 and PEER_TB == LANES
    n_experts = u_rows.shape[0] // SUBLANES
    sec_rows = u_rows.shape[0] // PEER_SECTIONS
    nblk = n // PEER_TB
    nstep = PEER_SECTIONS * nblk
    idx5, gw4, g0, g01 = _router(e, gw, n_experts)
    idx_by_slot = jnp.transpose(idx5, (0, 1, 3, 2, 4)).reshape(PEER_GROUP, nblk * PEER_GSTRIDE * PEER_TB)
    idx_by_slot = [idx_by_slot[j] for j in range(PEER_GROUP)]
    rows_per_g = PEER_GROUP * PEER_TB
    gw_blk = jnp.transpose(gw4, (0, 2, 1, 3)).reshape(nblk, PEER_GPT, rows_per_g)
    g0 = g0.reshape(nblk, PEER_TB)
    g01 = g01.reshape(nblk, PEER_TB)
    bounds = (jnp.zeros_like(g0), g0, g01)
    lists = [_peer_lists(bounds[s], bounds[s + 1]) for s in range(PEER_SECTIONS)]
    nq, glo, ghi, xrow, gt = [jnp.stack(a).reshape(-1) for a in zip(*lists)]

    npre = 3
    lst = pl.BlockSpec((PEER_QCAP,), lambda s, i, *_: (s * nblk + i,), memory_space=pltpu.SMEM)
    slots = [pl.BlockSpec((PEER_TB * PEER_GSTRIDE,), lambda s, i, *_: (i,), memory_space=pltpu.SMEM)
             ] * PEER_GROUP
    table = pl.BlockSpec(memory_space=pl.ANY)
    tab_scratch = pltpu.VMEM((sec_rows, LANES), jnp.float32)
    slot_rows = PEER_GPT * rows_per_g
    act = pl.pallas_call(
        _peer_u_kernel,
        grid_spec=pltpu.PrefetchScalarGridSpec(
            num_scalar_prefetch=npre,
            grid=(PEER_SECTIONS, nblk),
            in_specs=[lst, lst] + slots + [pl.BlockSpec((PEER_TB * SUBLANES, LANES), lambda s, i, *_: (i, 0)), table],
            out_specs=pl.BlockSpec((1, PEER_GPT, rows_per_g), lambda s, i, *_: (s * nblk + i, 0, 0)),
            scratch_shapes=[tab_scratch, pltpu.VMEM((slot_rows, LANES), jnp.float32), pltpu.SemaphoreType.DMA]),
        out_shape=jax.ShapeDtypeStruct((nstep, PEER_GPT, rows_per_g), jnp.float32),
        compiler_params=_cparams(("arbitrary", "arbitrary")),
        name="peer_u",
    )(nq, glo, ghi, xrow, gt, *idx_by_slot, h2.reshape(n * SUBLANES, LANES), u_rows)
    bg = 8 if nblk % 8 == 0 else nblk
    w = pl.pallas_call(
        _gate_kernel,
        grid=(PEER_SECTIONS, nblk // bg),
        in_specs=[pl.BlockSpec((bg, PEER_GPT, rows_per_g), lambda s, i: (s * (nblk // bg) + i, 0, 0)),
                  pl.BlockSpec((bg, PEER_GPT, rows_per_g), lambda s, i: (i, 0, 0))],
        out_specs=pl.BlockSpec((bg, PEER_GPT, rows_per_g), lambda s, i: (s * (nblk // bg) + i, 0, 0)),
        out_shape=jax.ShapeDtypeStruct((nstep, PEER_GPT, rows_per_g), jnp.float32),
        compiler_params=_cparams(("arbitrary", "arbitrary")),
        name="peer_gate",
    )(act, gw_blk)
    parts = pl.pallas_call(
        _peer_v_kernel,
        grid_spec=pltpu.PrefetchScalarGridSpec(
            num_scalar_prefetch=npre,
            grid=(PEER_SECTIONS, nblk),
            in_specs=[lst, lst] + slots + [
                pl.BlockSpec((1, PEER_SLOTS, LANES), lambda s, i, *_: (s * nblk + i, 0, 0)), table],
            out_specs=pl.BlockSpec((1, PEER_TB * SUBLANES, LANES), lambda s, i, *_: (s, i, 0)),
            scratch_shapes=[tab_scratch, pltpu.VMEM((slot_rows, LANES), jnp.float32),
                            pltpu.VMEM((PEER_TB * SUBLANES + SUBLANES, LANES), jnp.float32),
                            pltpu.SemaphoreType.DMA]),
        out_shape=jax.ShapeDtypeStruct((PEER_SECTIONS, n * SUBLANES, LANES), jnp.float32),
        compiler_params=_cparams(("arbitrary", "arbitrary")),
        name="peer_v",
    )(nq, glo, ghi, xrow, gt, *idx_by_slot, w.reshape(nstep, PEER_SLOTS, LANES), v_rows)
    return parts.reshape(PEER_SECTIONS, n, d)


def _final_kernel(x1_ref, ff_ref, g2_ref, w_ref, b_ref, o_ref, *, alpha):
    ff = ff_ref[0] + ff_ref[1]
    o_ref[...] = _ln(alpha * x1_ref[...] + g2_ref[...] * ff) * w_ref[...] + b_ref[...]


def _final(x1, parts, g2, mod_map, tm, ln2w, ln2b, alpha):
    n, d = x1.shape
    rm = g2.shape[1] if g2.shape[1] == 1 else tm
    vec = pl.BlockSpec((1, d), lambda i: (0, 0))
    return pl.pallas_call(
        functools.partial(_final_kernel, alpha=alpha),
        grid=(n // tm,),
        in_specs=[pl.BlockSpec((tm, d), lambda i: (i, 0)),
                  pl.BlockSpec((PEER_SECTIONS, tm, d), lambda i: (0, i, 0)),
                  pl.BlockSpec((None, rm, d), lambda i: mod_map(i) + (0,)), vec, vec],
        out_specs=pl.BlockSpec((tm, d), lambda i: (i, 0)),
        out_shape=jax.ShapeDtypeStruct((n, d), jnp.float32),
        compiler_params=_cparams(("arbitrary",)),
        name="final_ln",
    )(x1, parts, g2, ln2w, ln2b)


def _layer_weights(l, w_in, w_gla_up, b_gla, gla_norm_w, w_br_a, w_br_b, w_out, ln1_w, ln1_b, w_pq,
                   peer_k1, peer_k2, ln2_w, ln2_b):
    d = w_in.shape[1]
    bf = lambda a: a.astype(jnp.bfloat16)
    sizes = (W_A, W_A, W_A, W_BK, W_BK, W_BV, W_BV, GATE_RANK, d, d)
    offs = np.concatenate([[0], np.cumsum(sizes)])
    col = lambda i, j=None: w_in[l][:, offs[i]:offs[(i if j is None else j) + 1]]
    pad_rank = GLR_PAD - GATE_RANK
    return {
        "wa": bf(col(0, 2)), "wqb": bf(col(3)), "wkb": bf(col(4)), "wvb": bf(col(5)), "wrb": bf(col(6)),
        "wglr": bf(jnp.pad(col(7), ((0, 0), (0, pad_rank)))), "wga": bf(col(8)), "wgb": bf(col(9)),
        "wup": bf(jnp.pad(w_gla_up[l], ((0, pad_rank), (0, 0)))), "bup": b_gla[l].reshape(1, -1),
        "gnw": gla_norm_w[l].reshape(1, -1), "wbra": bf(w_br_a[l]), "wbrb": bf(w_br_b[l]), "wout": bf(w_out[l]),
        "ln1w": ln1_w[l].reshape(1, -1), "ln1b": ln1_b[l].reshape(1, -1), "wpq": bf(w_pq[l]),
        "k1": bf(peer_k1[l]), "k2": bf(peer_k2[l]),
        "ln2w": ln2_w[l].reshape(1, -1), "ln2b": ln2_b[l].reshape(1, -1),
    }


def _ffn_and_norm(x1, h2, s1t, s2t, g2, mod_map, tm, wts, u_rows, v_rows, alpha):
    e, gw = _topk(s1t, s2t)
    parts = _peer(h2, e, gw, u_rows, v_rows)
    return _final(x1, parts, g2, mod_map, tm, wts["ln2w"], wts["ln2b"], alpha)


def kernel(x_prompt, x_sample, c_prompt, c_sample, cache_kv_w128, cache_kv_w512, cache_kv_w2048, state_gla, w_ada, b_ada, w_in, w_gla_up, b_gla, gla_norm_w, w_br_a, w_br_b, w_out, ln1_w, ln1_b, w_pq, peer_k1, peer_k2, peer_u, peer_v, ln2_w, ln2_b):
    depth = w_ada.shape[0]
    b, t, d = x_prompt.shape
    db, ds, _ = x_sample.shape
    assert ds == 1, "the single-token kernels take one new token per sequence"
    alpha = (2 * depth) ** 0.25
    tm_p = 256
    tm_s = db
    yp = x_prompt.reshape(b * t, d)
    ys = x_sample.reshape(db, d)
    caches = (cache_kv_w128, cache_kv_w512, cache_kv_w2048)
    kv_p = [[] for _ in DIL_CONFIGS]
    kv_s = [[] for _ in DIL_CONFIGS]
    gla_p, gla_s = [], []
    nc = b + db
    nc_pad = -(-nc // SUBLANES) * SUBLANES
    c_all = jnp.pad(jnp.concatenate([c_prompt, c_sample], axis=0), ((0, nc_pad - nc), (0, 0)))
    map_p = lambda i: (i // (t // tm_p), 0)
    map_s = lambda i: (0, i)
    for l in range(depth):
        wts = _layer_weights(l, w_in, w_gla_up, b_gla, gla_norm_w, w_br_a, w_br_b, w_out, ln1_w, ln1_b, w_pq,
                             peer_k1, peer_k2, ln2_w, ln2_b)
        u_rows = peer_u[l].reshape(-1, LANES)
        v_rows = peer_v[l].reshape(-1, LANES)
        mod = _adaln(c_all, w_ada[l], b_ada[l])
        mods_p = [m.reshape(b, 1, d) for m in jnp.split(mod[:b], 6, axis=-1)]
        mods_s = [m.reshape(1, db, d) for m in jnp.split(mod[b:b + db], 6, axis=-1)]

        a, qb, kb, vb, rb, gd, ga, gb = _inproj(yp, mods_p[0], mods_p[1], map_p, tm_p, wts)
        a3 = a.reshape(b, t, 3 * W_A)
        o_g, l_g = [], []
        for gi, (window, dil) in enumerate(DIL_CONFIGS):
            o, lse = _dil_prompt(a3, gi, dil)
            o_g.append(o.reshape(b * t, W_G))
            l_g.append(lse.reshape(b * t, W_G))
            keep = min(window, t)
            k_last = a3[:, t - keep:, W_A + gi * W_G:W_A + (gi + 1) * W_G]
            v_last = a3[:, t - keep:, 2 * W_A + gi * W_G:2 * W_A + (gi + 1) * W_G]
            kv_p[gi].append(jnp.stack([k_last, v_last], axis=2).reshape(b, keep, 2, HEADS_PER_GROUP, HEAD_DIM_A))
        og, s_fin = _gla_prompt(qb.reshape(b, t, W_BK), kb.reshape(b, t, W_BK), vb.reshape(b, t, W_BV),
                                gd.reshape(b, t, W_BK))
        gla_p.append(s_fin)
        x1, h2, s1t, s2t = _merge(o_g, l_g, og.reshape(b * t, W_BV), rb, ga, gb, yp, mods_p[2], mods_p[3],
                                  mods_p[4], map_p, tm_p, wts, alpha)
        yp = _ffn_and_norm(x1, h2, s1t, s2t, mods_p[5], map_p, tm_p, wts, u_rows, v_rows, alpha)

        a, qb, kb, vb, rb, gd, ga, gb = _inproj(ys, mods_s[0], mods_s[1], map_s, tm_s, wts)
        layer_caches = [c[l] for c in caches]
        o, lse = _dil_sample(a, layer_caches)
        for gi in range(N_GROUPS):
            new = jnp.stack([a[:, W_A + gi * W_G:W_A + (gi + 1) * W_G],
                             a[:, 2 * W_A + gi * W_G:2 * W_A + (gi + 1) * W_G]], axis=1)
            new = new.reshape(db, 1, 2, HEADS_PER_GROUP, HEAD_DIM_A)
            kv_s[gi].append(jnp.concatenate([layer_caches[gi][:, 1:], new], axis=1))
        og, s_new = _gla_sample(qb, kb, vb, gd, state_gla[l])
        gla_s.append(s_new)
        o_g = [o[:, gi * W_G:(gi + 1) * W_G] for gi in range(N_GROUPS)]
        l_g = [lse[:, gi * W_G:(gi + 1) * W_G] for gi in range(N_GROUPS)]
        x1, h2, s1t, s2t = _merge(o_g, l_g, og, rb, ga, gb, ys, mods_s[2], mods_s[3], mods_s[4], map_s, tm_s,
                                  wts, alpha)
        ys = _ffn_and_norm(x1, h2, s1t, s2t, mods_s[5], map_s, tm_s, wts, u_rows, v_rows, alpha)

    return (yp.reshape(b, t, d), ys.reshape(db, ds, d),
            jnp.stack(kv_p[0]), jnp.stack(kv_p[1]), jnp.stack(kv_p[2]), jnp.stack(gla_p),
            jnp.stack(kv_s[0]), jnp.stack(kv_s[1]), jnp.stack(kv_s[2]), jnp.stack(gla_s))
```

```python
import functools

import numpy as np
import jax
import jax.numpy as jnp
from jax import lax
from jax.experimental import pallas as pl
from jax.experimental.pallas import tpu as pltpu

DIL_CONFIGS = ((128, 1), (512, 4), (2048, 16))
N_GROUPS = 3
HEADS_PER_GROUP = 4
HEAD_DIM_A = 64
W_G = HEADS_PER_GROUP * HEAD_DIM_A
W_A = N_GROUPS * W_G
DIL_BLOCK = 128
N_HEADS_B = 4
HEAD_K_B = 128
HEAD_V_B = 256
W_BK = N_HEADS_B * HEAD_K_B
W_BV = N_HEADS_B * HEAD_V_B
GATE_RANK = 16
GATE_TEMP = 16.0
GLR_PAD = 128
GLA_CHUNK = 64
GLA_SUB = 16
PEER_HEADS = 8
PEER_KEYS = 128
PEER_TOPK = 16
TOPK_SHIFT = PEER_TOPK.bit_length() - 1
PEER_PAIRS = PEER_HEADS * PEER_TOPK
LN_EPS = 1e-5

LANES = 128
SUBLANES = 8
VREG_ELEMS = LANES * SUBLANES
VMEM_LIMIT = 56 * 1024 * 1024

PEER_SECTIONS = 2
PEER_GROUP = SUBLANES
PEER_SLOTS = PEER_PAIRS + PEER_SECTIONS * PEER_GROUP
GROUP_SHIFT = PEER_GROUP.bit_length() - 1
PEER_GPT = PEER_SLOTS // PEER_GROUP
PEER_SPARE_G = PEER_GPT - 1
PEER_GSTRIDE = 32
PEER_GSHIFT = PEER_GSTRIDE.bit_length() - 1
PEER_TB = LANES
TB_SHIFT = PEER_TB.bit_length() - 1
PEER_QCAP = PEER_TB * (PEER_PAIRS // PEER_GROUP)
PEER_UNROLL = 8

_HI = lax.Precision.HIGHEST
_NEG = float("-inf")


def _cparams(sem, vmem=VMEM_LIMIT):
    return pltpu.CompilerParams(dimension_semantics=sem, vmem_limit_bytes=vmem)


def _ln(x):
    mu = jnp.mean(x, axis=-1, keepdims=True)
    xc = x - mu
    var = jnp.mean(xc * xc, axis=-1, keepdims=True)
    return xc * lax.rsqrt(var + LN_EPS)


def _bdot(a, b):
    return jnp.dot(a.astype(jnp.bfloat16), b.astype(jnp.bfloat16), preferred_element_type=jnp.float32)


def _bdot_nt(a, b):
    return lax.dot_general(a.astype(jnp.bfloat16), b.astype(jnp.bfloat16), (((1,), (1,)), ((), ())),
                           preferred_element_type=jnp.float32)


def _alibi_slope(head):
    return float(np.exp2(np.float32(-8.0 * (head + 1) / (N_GROUPS * HEADS_PER_GROUP))))


def _ada_kernel(c_ref, w_ref, b_ref, o_ref):
    c = c_ref[...]
    o_ref[...] = _bdot(c * jax.nn.sigmoid(c), w_ref[...]) + b_ref[...]


def _adaln(c, w_ada, b_ada):
    bc, d = c.shape
    ncol = w_ada.shape[1] // d
    return pl.pallas_call(
        _ada_kernel,
        grid=(ncol,),
        in_specs=[pl.BlockSpec((bc, d), lambda j: (0, 0)),
                  pl.BlockSpec((d, d), lambda j: (0, j)),
                  pl.BlockSpec((1, d), lambda j: (0, j))],
        out_specs=pl.BlockSpec((bc, d), lambda j: (0, j)),
        out_shape=jax.ShapeDtypeStruct((bc, ncol * d), jnp.float32),
        compiler_params=_cparams(("arbitrary",)),
        name="adaln",
    )(c, w_ada, b_ada.reshape(1, -1))


def _inproj_kernel(x_ref, sh_ref, sc_ref, wa_ref, wqb_ref, wkb_ref, wvb_ref, wrb_ref, wglr_ref, wga_ref, wgb_ref,
                   wup_ref, bup_ref, a_ref, qb_ref, kb_ref, vb_ref, rb_ref, gd_ref, ga_ref, gb_ref):
    h = (_ln(x_ref[...]) * (1.0 + sc_ref[...]) + sh_ref[...]).astype(jnp.bfloat16)

    def proj(w_ref):
        return jnp.dot(h, w_ref[...], preferred_element_type=jnp.float32)

    a_ref[...] = proj(wa_ref)
    qb_ref[...] = proj(wqb_ref) * (HEAD_K_B ** -0.5)
    kb_ref[...] = proj(wkb_ref)
    vb_ref[...] = proj(wvb_ref)
    rb_ref[...] = proj(wrb_ref)
    ga_ref[...] = proj(wga_ref)
    gb_ref[...] = proj(wgb_ref)
    glr = proj(wglr_ref)
    gate = _bdot(glr, wup_ref[...]) + bup_ref[...]
    gd_ref[...] = jax.nn.log_sigmoid(gate) * (1.0 / GATE_TEMP)


def _inproj(x2d, shift, scale, mod_map, tm, wts):
    n, d = x2d.shape
    rm = shift.shape[1] if shift.shape[1] == 1 else tm
    mod_spec = pl.BlockSpec((None, rm, d), lambda i: mod_map(i) + (0,))
    row = lambda w: pl.BlockSpec((tm, w), lambda i: (i, 0))
    const = lambda a: pl.BlockSpec(a.shape, lambda i: (0,) * a.ndim, pipeline_mode=pl.Buffered(1))
    names = ("wa", "wqb", "wkb", "wvb", "wrb", "wglr", "wga", "wgb", "wup", "bup")
    widths = (3 * W_A, W_BK, W_BK, W_BV, W_BV, W_BK, d, d)
    return pl.pallas_call(
        _inproj_kernel,
        grid=(n // tm,),
        in_specs=[row(d), mod_spec, mod_spec] + [const(wts[k]) for k in names],
        out_specs=[row(w) for w in widths],
        out_shape=[jax.ShapeDtypeStruct((n, w), jnp.float32) for w in widths],
        compiler_params=_cparams(("arbitrary",)),
        name="inproj",
    )(x2d, shift, scale, *[wts[k] for k in names])


def _dil_prompt_kernel(q_ref, kc_ref, kp_ref, vc_ref, vp_ref, o_ref, l_ref, o_scr, l_scr, *, group, dil):
    has_prev = pl.program_id(2) > 0
    half = pl.program_id(1)
    heads = LANES // HEAD_DIM_A
    qi = lax.broadcasted_iota(jnp.int32, (DIL_BLOCK, DIL_BLOCK), 0)
    ki = lax.broadcasted_iota(jnp.int32, (DIL_BLOCK, DIL_BLOCK), 1)
    valid_p = jnp.logical_and(ki >= qi, has_prev)
    valid_c = ki <= qi
    dist_p = ((qi + DIL_BLOCK - ki) * dil).astype(jnp.float32)
    dist_c = ((qi - ki) * dil).astype(jnp.float32)

    def phase(r, carry):
        rows = pl.ds(r, DIL_BLOCK, stride=dil) if dil > 1 else pl.ds(0, DIL_BLOCK)
        q_all, kc, kp, vc, vp = (ref[rows, :] for ref in (q_ref, kc_ref, kp_ref, vc_ref, vp_ref))
        outs, lses = [], []
        for hh in range(heads):
            slope = jnp.where(half == 0, _alibi_slope(group * HEADS_PER_GROUP + hh),
                              _alibi_slope(group * HEADS_PER_GROUP + heads + hh))
            sl = slice(hh * HEAD_DIM_A, (hh + 1) * HEAD_DIM_A)
            q = q_all[:, sl]
            sp = _bdot_nt(q, kp[:, sl]) * (HEAD_DIM_A ** -0.5) - slope * dist_p
            sc = _bdot_nt(q, kc[:, sl]) * (HEAD_DIM_A ** -0.5) - slope * dist_c
            sp = jnp.where(valid_p, sp, _NEG)
            sc = jnp.where(valid_c, sc, _NEG)
            m = jnp.maximum(jnp.max(sp, axis=-1, keepdims=True), jnp.max(sc, axis=-1, keepdims=True))
            pp = jnp.exp(sp - m)
            pc = jnp.exp(sc - m)
            z = jnp.sum(pp, axis=-1, keepdims=True) + jnp.sum(pc, axis=-1, keepdims=True)
            outs.append((_bdot(pp, vp[:, sl]) + _bdot(pc, vc[:, sl])) / z)
            lses.append(jnp.broadcast_to(m + jnp.log(z), (DIL_BLOCK, HEAD_DIM_A)))
        o_scr[r] = jnp.concatenate(outs, axis=-1)
        l_scr[r] = jnp.concatenate(lses, axis=-1)
        return carry

    lax.fori_loop(0, dil, phase, 0)
    for r in range(dil):
        rows = pl.ds(r, DIL_BLOCK, stride=dil) if dil > 1 else pl.ds(0, DIL_BLOCK)
        o_ref[rows, :] = o_scr[r]
        l_ref[rows, :] = l_scr[r]


def _dil_prompt(qkv, group, dil):
    b, t, wq = qkv.shape
    span = dil * DIL_BLOCK
    assert t % span == 0
    halves = W_G // LANES
    assert halves == 2 and HEADS_PER_GROUP * HEAD_DIM_A == W_G
    qcol, kcol, vcol = group, N_GROUPS + group, 2 * N_GROUPS + group
    blk = (None, span, LANES)
    cur = lambda col: pl.BlockSpec(blk, lambda bi, h, n: (bi, n, halves * col + h))
    prev = lambda col: pl.BlockSpec(blk, lambda bi, h, n: (bi, jnp.maximum(n - 1, 0), halves * col + h))
    ospec = pl.BlockSpec(blk, lambda bi, h, n: (bi, n, h))
    scr = pltpu.VMEM((dil, DIL_BLOCK, LANES), jnp.float32)
    return pl.pallas_call(
        functools.partial(_dil_prompt_kernel, group=group, dil=dil),
        grid=(b, halves, t // span),
        in_specs=[cur(qcol), cur(kcol), prev(kcol), cur(vcol), prev(vcol)],
        out_specs=[ospec, ospec],
        out_shape=[jax.ShapeDtypeStruct((b, t, W_G), jnp.float32)] * 2,
        scratch_shapes=[scr, scr],
        compiler_params=_cparams(("arbitrary", "arbitrary", "arbitrary")),
        name=f"dil_prompt_g{group}",
    )(qkv, qkv, qkv, qkv, qkv)


def _gla_prompt_kernel(q_ref, k_ref, v_ref, g_ref, o_ref, sfin_ref, s_scr):
    c = pl.program_id(1)
    nchunk = pl.num_programs(1)
    C = GLA_CHUNK

    @pl.when(c == 0)
    def _():
        s_scr[...] = jnp.zeros_like(s_scr)

    ri = lax.broadcasted_iota(jnp.int32, (C, C), 0)
    ci = lax.broadcasted_iota(jnp.int32, (C, C), 1)
    tri = (ri >= ci).astype(jnp.float32)
    bcum = jnp.dot(tri, g_ref[...], precision=_HI, preferred_element_type=jnp.float32)
    row16 = lax.broadcasted_iota(jnp.int32, (GLA_SUB, HEAD_K_B), 0)
    lane16 = lax.broadcasted_iota(jnp.int32, (GLA_SUB, LANES), 1)
    rowc = lax.broadcasted_iota(jnp.int32, (C, HEAD_K_B), 0)
    nsub = C // GLA_SUB
    outs = []
    for h in range(N_HEADS_B):
        ks = slice(h * HEAD_K_B, (h + 1) * HEAD_K_B)
        vs = slice(h * HEAD_V_B, (h + 1) * HEAD_V_B)
        bh = bcum[:, ks]
        qh = q_ref[:, ks]
        kh = k_ref[:, ks]
        vh = v_ref[:, vs]
        sh = s_scr[h]
        o_inter = _bdot(qh * jnp.exp(bh), sh)
        arows = []
        for i in range(nsub):
            r0 = i * GLA_SUB
            bi = bh[r0:r0 + GLA_SUB]
            qi_ = qh[r0:r0 + GLA_SUB]
            ki_ = kh[r0:r0 + GLA_SUB]
            a = jnp.zeros((GLA_SUB, LANES), jnp.float32)
            for s in range(GLA_SUB):
                e = jnp.exp(jnp.where(row16 >= s, bi - bi[s:s + 1], _NEG))
                col = jnp.sum(qi_ * (ki_[s:s + 1] * e), axis=-1, keepdims=True)
                a = jnp.where(lane16 == r0 + s, col, a)
            a = a[:, :C]
            if i > 0:
                b0 = bi[0:1]
                qt = qi_ * jnp.exp(bi - b0)
                kt = kh * jnp.exp(jnp.where(rowc < r0, b0 - bh, _NEG))
                a = a + _bdot_nt(qt, kt)
            arows.append(a)
        amat = jnp.concatenate(arows, axis=0)
        outs.append(o_inter + _bdot(amat, vh))
        bl = bh[C - 1:C]
        kt = kh * jnp.exp(bl - bh)
        dcol = jnp.transpose(jnp.broadcast_to(jnp.exp(bl), (SUBLANES, HEAD_K_B)))[:, 0:1]
        upd = lax.dot_general(kt.astype(jnp.bfloat16), vh.astype(jnp.bfloat16), (((0,), (0,)), ((), ())),
                              preferred_element_type=jnp.float32)
        s_scr[h] = dcol * sh + upd
    o_ref[...] = jnp.concatenate(outs, axis=-1)

    @pl.when(c == nchunk - 1)
    def _():
        sfin_ref[...] = s_scr[...]


def _gla_prompt(qb, kb, vb, gd):
    b, t, _ = qb.shape
    assert t % GLA_CHUNK == 0
    spec = lambda w: pl.BlockSpec((None, GLA_CHUNK, w), lambda bi, c: (bi, c, 0))
    sshape = (N_HEADS_B, HEAD_K_B, HEAD_V_B)
    return pl.pallas_call(
        _gla_prompt_kernel,
        grid=(b, t // GLA_CHUNK),
        in_specs=[spec(W_BK), spec(W_BK), spec(W_BV), spec(W_BK)],
        out_specs=[spec(W_BV), pl.BlockSpec((None,) + sshape, lambda bi, c: (bi, 0, 0, 0))],
        out_shape=[jax.ShapeDtypeStruct((b, t, W_BV), jnp.float32),
                   jax.ShapeDtypeStruct((b,) + sshape, jnp.float32)],
        scratch_shapes=[pltpu.VMEM(sshape, jnp.float32)],
        compiler_params=_cparams(("arbitrary", "arbitrary")),
        name="gla_prompt",
    )(qb, kb, vb, gd)


SAMPLE_SEQS = 8


def _to_col(row):
    return jnp.transpose(jnp.broadcast_to(row, (SUBLANES, row.shape[1])))[:, 0:1]


def _to_row(col):
    return jnp.transpose(jnp.broadcast_to(col, (col.shape[0], SUBLANES)))[0:1, :]


def _dil_sample_kernel(a_ref, c0_ref, c1_ref, c2_ref, o_ref, l_ref, n0_ref, n1_ref, n2_ref):
    col = _to_col(a_ref[0])
    scale = HEAD_DIM_A ** -0.5
    kv_w = 2 * W_G
    for g, (cref, nref, (window, dil)) in enumerate(zip((c0_ref, c1_ref, c2_ref), (n0_ref, n1_ref, n2_ref),
                                                        DIL_CONFIGS)):
        q = col[g * W_G:(g + 1) * W_G]
        knew = col[W_A + g * W_G:W_A + (g + 1) * W_G]
        vnew = col[2 * W_A + g * W_G:2 * W_A + (g + 1) * W_G]
        x = cref[0]
        lane = lax.broadcasted_iota(jnp.int32, (1, window), 1)
        on_stride = (lane & (dil - 1)) == 0
        dist = (window - lane).astype(jnp.float32)
        o_cols, l_cols = [], []
        for hh in range(HEADS_PER_GROUP):
            rows = slice(hh * HEAD_DIM_A, (hh + 1) * HEAD_DIM_A)
            qh = q[rows]
            s = jnp.sum(x[rows, :] * qh, axis=0, keepdims=True) * scale
            s = s - _alibi_slope(g * HEADS_PER_GROUP + hh) * dist
            s = jnp.where(on_stride, s, _NEG)
            ss = jnp.sum(knew[rows] * qh, axis=0, keepdims=True) * scale
            m = jnp.maximum(jnp.max(s, axis=1, keepdims=True), ss)
            p = jnp.exp(s - m)
            ps = jnp.exp(ss - m)
            z = jnp.sum(p, axis=1, keepdims=True) + ps
            vrows = slice(W_G + hh * HEAD_DIM_A, W_G + (hh + 1) * HEAD_DIM_A)
            o_cols.append((jnp.sum(x[vrows, :] * p, axis=1, keepdims=True) + ps * vnew[rows]) / z)
            l_cols.append(jnp.broadcast_to(m + jnp.log(z), (HEAD_DIM_A, 1)))
        o_ref[0, :, g * W_G:(g + 1) * W_G] = _to_row(jnp.concatenate(o_cols, axis=0))
        l_ref[0, :, g * W_G:(g + 1) * W_G] = _to_row(jnp.concatenate(l_cols, axis=0))
        shifted = pltpu.roll(x, window - 1, axis=1)
        nref[0] = jnp.where(lane == window - 1, jnp.concatenate([knew, vnew], axis=0), shifted)


def _dil_sample(qkv, caches):
    db = qkv.shape[0]
    kv_w = 2 * W_G
    views, cspecs, oshapes = [], [], []
    for cache, (window, dil) in zip(caches, DIL_CONFIGS):
        assert cache.shape[1] == window, "window caches shorter than the window are not supported"
        assert dil & (dil - 1) == 0
        views.append(jnp.transpose(cache, (0, 2, 3, 4, 1)).reshape(db, kv_w, window))
        cspecs.append(pl.BlockSpec((1, kv_w, window), lambda i: (i, 0, 0)))
        oshapes.append(jax.ShapeDtypeStruct((db, kv_w, window), jnp.float32))
    rspec = lambda w: pl.BlockSpec((1, 1, w), lambda i: (i, 0, 0))
    o, lse, *new = pl.pallas_call(
        _dil_sample_kernel,
        grid=(db,),
        in_specs=[rspec(3 * W_A)] + cspecs,
        out_specs=[rspec(W_A), rspec(W_A)] + cspecs,
        out_shape=[jax.ShapeDtypeStruct((db, 1, W_A), jnp.float32)] * 2 + oshapes,
        compiler_params=_cparams(("arbitrary",)),
        name="dil_sample",
    )(qkv.reshape(db, 1, 3 * W_A), *views)
    new = [jnp.transpose(n.reshape(db, 2, HEADS_PER_GROUP, HEAD_DIM_A, n.shape[-1]), (0, 4, 1, 2, 3)) for n in new]
    return o.reshape(db, W_A), lse.reshape(db, W_A), new


def _gla_sample_kernel(q_ref, k_ref, v_ref, g_ref, s0_ref, o_ref, s_ref):
    for h in range(N_HEADS_B):
        ks = slice(h * HEAD_K_B, (h + 1) * HEAD_K_B)
        vs = slice(h * HEAD_V_B, (h + 1) * HEAD_V_B)
        qT = jnp.transpose(q_ref[:, ks])
        kT = jnp.transpose(k_ref[:, ks])
        aT = jnp.transpose(jnp.exp(g_ref[:, ks]))
        for j in range(SAMPLE_SEQS):
            s_new = aT[:, j:j + 1] * s0_ref[j, h] + kT[:, j:j + 1] * v_ref[j:j + 1, vs]
            s_ref[j, h] = s_new
            o_ref[j:j + 1, vs] = jnp.sum(qT[:, j:j + 1] * s_new, axis=0, keepdims=True)


def _gla_sample(qb, kb, vb, gd, s0):
    db = qb.shape[0]
    row = lambda w: pl.BlockSpec((SAMPLE_SEQS, w), lambda i: (i, 0))
    sspec = pl.BlockSpec((SAMPLE_SEQS, N_HEADS_B, HEAD_K_B, HEAD_V_B), lambda i: (i, 0, 0, 0))
    return pl.pallas_call(
        _gla_sample_kernel,
        grid=(db // SAMPLE_SEQS,),
        in_specs=[row(W_BK), row(W_BK), row(W_BV), row(W_BK), sspec],
        out_specs=[row(W_BV), sspec],
        out_shape=[jax.ShapeDtypeStruct((db, W_BV), jnp.float32), jax.ShapeDtypeStruct(s0.shape, jnp.float32)],
        compiler_params=_cparams(("arbitrary",)),
        name="gla_sample",
    )(qb, kb, vb, gd, s0)


def _merge_kernel(o0_ref, o1_ref, o2_ref, l0_ref, l1_ref, l2_ref, og_ref, rb_ref, ga_ref, gb_ref, x_ref,
                  g1_ref, sh2_ref, sc2_ref, gnw_ref, wbra_ref, wbrb_ref, wout_ref, ln1w_ref, ln1b_ref,
                  wpq_ref, k1_ref, k2_ref, x1_ref, h2_ref, s1_ref, s2_ref, *, alpha):
    l0, l1, l2 = l0_ref[...], l1_ref[...], l2_ref[...]
    m = jnp.maximum(jnp.maximum(l0, l1), l2)
    e0, e1, e2 = jnp.exp(l0 - m), jnp.exp(l1 - m), jnp.exp(l2 - m)
    oa = (e0 * o0_ref[...] + e1 * o1_ref[...] + e2 * o2_ref[...]) / (e0 + e1 + e2)
    og = og_ref[...]
    parts = []
    for h in range(N_HEADS_B):
        oh = og[:, h * HEAD_V_B:(h + 1) * HEAD_V_B]
        parts.append(oh * lax.rsqrt(jnp.mean(oh * oh, axis=-1, keepdims=True) + LN_EPS))
    rb = rb_ref[...]
    ob = jnp.concatenate(parts, axis=-1) * gnw_ref[...] * (rb * jax.nn.sigmoid(rb))
    merged = (jax.nn.sigmoid(ga_ref[...]) * _bdot(oa, wbra_ref[...])
              + jax.nn.sigmoid(gb_ref[...]) * _bdot(ob, wbrb_ref[...]))
    mix = _bdot(merged, wout_ref[...])
    x1 = _ln(alpha * x_ref[...] + g1_ref[...] * mix) * ln1w_ref[...] + ln1b_ref[...]
    x1_ref[...] = x1
    h2 = _ln(x1) * (1.0 + sc2_ref[...]) + sh2_ref[...]
    h2_ref[...] = h2
    qv = _bdot(h2, wpq_ref[...]).astype(jnp.bfloat16)
    half = PEER_KEYS
    for h in range(PEER_HEADS):
        base = h * 2 * half
        s1_ref[h] = lax.dot_general(k1_ref[h], qv[:, base:base + half], (((1,), (1,)), ((), ())),
                                    preferred_element_type=jnp.float32)
        s2_ref[h] = lax.dot_general(k2_ref[h], qv[:, base + half:base + 2 * half], (((1,), (1,)), ((), ())),
                                    preferred_element_type=jnp.float32)


def _merge(o_g, l_g, og, rb, ga, gb, x2d, g1, sh2, sc2, mod_map, tm, wts, alpha):
    n, d = x2d.shape
    rm = g1.shape[1] if g1.shape[1] == 1 else tm
    mod_spec = pl.BlockSpec((None, rm, d), lambda i: mod_map(i) + (0,))
    row = lambda w: pl.BlockSpec((tm, w), lambda i: (i, 0))
    const = lambda a: pl.BlockSpec(a.shape, lambda i: (0,) * a.ndim, pipeline_mode=pl.Buffered(1))
    names = ("gnw", "wbra", "wbrb", "wout", "ln1w", "ln1b", "wpq", "k1", "k2")
    sspec = pl.BlockSpec((PEER_HEADS, PEER_KEYS, tm), lambda i: (0, 0, i))
    return pl.pallas_call(
        functools.partial(_merge_kernel, alpha=alpha),
        grid=(n // tm,),
        in_specs=[row(W_G)] * 6 + [row(W_BV), row(W_BV), row(d), row(d), row(d), mod_spec, mod_spec, mod_spec]
                 + [const(wts[k]) for k in names],
        out_specs=[row(d), row(d), sspec, sspec],
        out_shape=[jax.ShapeDtypeStruct((n, d), jnp.float32)] * 2
                  + [jax.ShapeDtypeStruct((PEER_HEADS, PEER_KEYS, n), jnp.float32)] * 2,
        compiler_params=_cparams(("arbitrary",)),
        name="merge",
    )(*o_g, *l_g, og, rb, ga, gb, x2d, g1, sh2, sc2, *[wts[k] for k in names])


_CAND_IDS = tuple(i * PEER_TOPK + j for i in range(PEER_TOPK) for j in range(PEER_TOPK)
                  if (i + 1) * (j + 1) <= PEER_TOPK)
_NO_ID = PEER_KEYS * PEER_KEYS


def _topk_kernel(s1_ref, s2_ref, e_ref, w_ref, sa, sb, va, vb, ia, ib, cand, sc_scr, ci_scr):
    sub = s1_ref.shape[2]
    shape = (sub, LANES)
    sa[...] = s1_ref[0]
    sb[...] = s2_ref[0]

    def tree(op, xs):
        xs = list(xs)
        while len(xs) > 1:
            xs = [op(xs[k], xs[k + 1]) for k in range(0, len(xs) - 1, 2)] + (xs[-1:] if len(xs) % 2 else [])
        return xs[0]

    def extract(s_scr, ids, r, v_out, i_out):
        n = len(ids)
        m = tree(jnp.maximum, [s_scr[k] for k in range(n)])
        idx = tree(jnp.minimum, [jnp.where(s_scr[k] == m, ids[k], _NO_ID) for k in range(n)])
        for k in range(n):
            s_scr[k] = jnp.where(idx == ids[k], _NEG, s_scr[k])
        v_out[r] = m
        i_out[r] = idx

    def stage1(r, c):
        extract(sa, range(PEER_KEYS), r, va, ia)
        extract(sb, range(PEER_KEYS), r, vb, ib)
        return c

    lax.fori_loop(0, PEER_TOPK, stage1, 0)
    for k, ci in enumerate(_CAND_IDS):
        cand[k] = va[ci // PEER_TOPK] + vb[ci % PEER_TOPK]

    def stage2(r, c):
        extract(cand, _CAND_IDS, r, sc_scr, ci_scr)
        return c

    lax.fori_loop(0, PEER_TOPK, stage2, 0)
    top = sc_scr[0]
    z = jnp.zeros(shape, jnp.float32)
    for r in range(PEER_TOPK):
        z = z + jnp.exp(sc_scr[r] - top)
    for r in range(PEER_TOPK):
        w_ref[0, 0, r] = jnp.exp(sc_scr[r] - top) / z
        ci = ci_scr[r]
        hi = lax.shift_right_logical(ci, TOPK_SHIFT)
        lo = ci & (PEER_TOPK - 1)
        e1 = jnp.zeros(shape, jnp.int32)
        e2 = jnp.zeros(shape, jnp.int32)
        for i in range(PEER_TOPK):
            e1 = jnp.where(hi == i, ia[i], e1)
            e2 = jnp.where(lo == i, ib[i], e2)
        e_ref[0, 0, r] = e1 * PEER_KEYS + e2


def _topk(s1t, s2t):
    nh, nk, n = s1t.shape
    sub = min(SUBLANES, n // LANES)
    nchunk = n // (sub * LANES)
    v1 = s1t.reshape(nh, nk, n // LANES, LANES)
    v2 = s2t.reshape(nh, nk, n // LANES, LANES)
    ispec = pl.BlockSpec((1, nk, sub, LANES), lambda c, h: (h, 0, c, 0))
    ospec = pl.BlockSpec((1, 1, PEER_TOPK, sub, LANES), lambda c, h: (c, h, 0, 0, 0))
    oshape = (nchunk, nh, PEER_TOPK, sub, LANES)
    key = lambda k, dt: pltpu.VMEM((k, sub, LANES), dt)
    e, w = pl.pallas_call(
        _topk_kernel,
        grid=(nchunk, nh),
        in_specs=[ispec, ispec],
        out_specs=[ospec, ospec],
        out_shape=[jax.ShapeDtypeStruct(oshape, jnp.int32), jax.ShapeDtypeStruct(oshape, jnp.float32)],
        scratch_shapes=[key(nk, jnp.float32), key(nk, jnp.float32),
                        key(PEER_TOPK, jnp.float32), key(PEER_TOPK, jnp.float32),
                        key(PEER_TOPK, jnp.int32), key(PEER_TOPK, jnp.int32),
                        key(len(_CAND_IDS), jnp.float32), key(PEER_TOPK, jnp.float32),
                        key(PEER_TOPK, jnp.int32)],
        compiler_params=_cparams(("arbitrary", "arbitrary")),
        name="peer_topk",
    )(v1, v2)
    return e, w


def _load_section(tab_hbm, tab_vmem, sem):
    rows = tab_vmem.shape[0]

    @pl.when(pl.program_id(1) == 0)
    def _():
        start = pl.multiple_of(pl.program_id(0) * rows, SUBLANES)
        cp = pltpu.make_async_copy(tab_hbm.at[pl.ds(start, rows)], tab_vmem, sem)
        cp.start()
        cp.wait()


def _grid_step():
    return pl.program_id(0) * pl.num_programs(1) + pl.program_id(1)


def _peer_u_kernel(nq_ref, glo_ref, ghi_ref, xrow_ref, gt_ref, *refs):
    idx_refs = refs[:PEER_GROUP]
    x_ref, tab_hbm, act_ref, tab_vmem, dbuf, sem = refs[PEER_GROUP:]
    _load_section(tab_hbm, tab_vmem, sem)
    step = _grid_step()

    @pl.when(step == 0)
    def _():
        dbuf[...] = jnp.zeros_like(dbuf)

    nq = nq_ref[step]
    sub = lax.broadcasted_iota(jnp.int32, (SUBLANES, LANES), 0)
    order = _fold_slot_order()

    def fold(a, b, sh):
        keep = (sub & sh) == 0
        u = jnp.where(keep, a, b)
        v = jnp.where(keep, b, a)
        if 2 * sh == SUBLANES:
            w = pltpu.roll(v, sh, axis=0)
        else:
            w = jnp.where(keep, pltpu.roll(v, SUBLANES - sh, axis=0), pltpu.roll(v, sh, axis=0))
        return u + w

    def body(i, c):
        for u in range(PEER_UNROLL):
            q = i * PEER_UNROLL + u
            gt = gt_ref[q]
            x = x_ref[pl.ds(pl.multiple_of(xrow_ref[q], SUBLANES), SUBLANES), :]
            ps = [tab_vmem[pl.ds(pl.multiple_of(idx_refs[order[j]][gt], SUBLANES), SUBLANES), :] * x
                  for j in range(PEER_GROUP)]
            while len(ps) > 1:
                sh = len(ps) // 2
                ps = [fold(ps[2 * k], ps[2 * k + 1], sh) for k in range(sh)]
            dbuf[pl.ds(pl.multiple_of(gt * PEER_GROUP, SUBLANES), SUBLANES), :] = ps[0]
        return c

    lax.fori_loop(0, nq // PEER_UNROLL, body, 0)
    rows_per_g = PEER_GROUP * PEER_TB
    glo, ghi = glo_ref[step], ghi_ref[step]

    def zfill(g, carry):
        act_ref[0, pl.ds(g, 1), :] = jnp.zeros((1, rows_per_g), jnp.float32)
        return carry

    def reduce(g, carry):
        sums = [jnp.sum(jnp.transpose(dbuf[pl.ds(g * rows_per_g + j, PEER_TB, stride=PEER_GROUP), :]),
                        axis=0, keepdims=True) for j in range(PEER_GROUP)]
        act_ref[0, pl.ds(g, 1), :] = jnp.concatenate(sums, axis=-1)
        return carry

    lax.fori_loop(0, glo, zfill, 0)
    lax.fori_loop(glo, ghi, reduce, 0)
    lax.fori_loop(ghi, PEER_GPT, zfill, 0)


def _fold_slot_order():
    pos = [[j] for j in range(PEER_GROUP)]
    sl = [0] * PEER_GROUP
    sh = PEER_GROUP // 2
    groups = pos
    while len(groups) > 1:
        nxt = []
        for i in range(len(groups) // 2):
            for j in groups[2 * i + 1]:
                sl[j] |= sh
            nxt.append(groups[2 * i] + groups[2 * i + 1])
        groups = nxt
        sh //= 2
    return sl


def _peer_v_kernel(nq_ref, glo_ref, ghi_ref, xrow_ref, gt_ref, *refs):
    idx_refs = refs[:PEER_GROUP]
    w_ref, tab_hbm, out_ref, tab_vmem, wb, acc_scr, sem = refs[PEER_GROUP:]
    _load_section(tab_hbm, tab_vmem, sem)
    step = _grid_step()
    nq = nq_ref[step]
    rows_per_g = PEER_GROUP * PEER_TB

    @pl.when(step == 0)
    def _():
        wb[pl.ds(PEER_SPARE_G * rows_per_g, rows_per_g), :] = jnp.zeros((rows_per_g, LANES), jnp.float32)

    acc_scr[...] = jnp.zeros_like(acc_scr)

    def spread(g, carry):
        for j in range(PEER_GROUP):
            r = g * PEER_GROUP + j
            rep = jnp.broadcast_to(w_ref[0, pl.ds(r, 1), :], (LANES, LANES))
            wb[pl.ds(pl.multiple_of(r * PEER_TB, PEER_TB), PEER_TB), :] = jnp.transpose(rep)
        return carry

    lax.fori_loop(glo_ref[step], ghi_ref[step], spread, 0)
    trash = PEER_TB * SUBLANES

    def body(i, carry):
        cur, acc = carry
        for u in range(PEER_UNROLL):
            q = i * PEER_UNROLL + u
            gt = gt_ref[q]
            row0 = gt + lax.shift_right_logical(gt, TB_SHIFT) * ((PEER_GROUP - 1) * PEER_TB)
            terms = [tab_vmem[pl.ds(pl.multiple_of(idx_refs[j][gt], SUBLANES), SUBLANES), :]
                     * wb[pl.ds(row0 + j * PEER_TB, 1), :] for j in range(PEER_GROUP)]
            while len(terms) > 1:
                terms = [terms[2 * k] + terms[2 * k + 1] for k in range(len(terms) // 2)]
            row = xrow_ref[q]
            acc_scr[pl.ds(pl.multiple_of(cur, SUBLANES), SUBLANES), :] = acc
            acc = jnp.where(row == cur, acc + terms[0], terms[0])
            cur = row
        return cur, acc

    cur, acc = lax.fori_loop(0, nq // PEER_UNROLL, body,
                             (jnp.int32(trash), jnp.zeros((SUBLANES, LANES), jnp.float32)))
    acc_scr[pl.ds(pl.multiple_of(cur, SUBLANES), SUBLANES), :] = acc
    out_ref[0] = acc_scr[0:trash, :]


def _gate_kernel(a_ref, gw_ref, w_ref):
    act = a_ref[...]
    w_ref[...] = gw_ref[...] * (0.5 * act * (1.0 + lax.erf(act * (2.0 ** -0.5))))


def _router_kernel(e_ref, w_ref, idx_ref, gw_ref, g0_ref, g01_ref, pos_scr, loc_scr, *, sec_experts):
    shape = e_ref.shape[3:]
    zero = jnp.zeros(shape, jnp.int32)
    r0, r1 = zero, zero
    for p in range(PEER_PAIRS):
        e = e_ref[0, p // PEER_TOPK, p % PEER_TOPK]
        upper = e >= sec_experts
        loc_scr[p] = (e & (sec_experts - 1)) * SUBLANES
        pos_scr[p] = jnp.where(upper, r1 + PEER_SLOTS, r0)
        r0 = r0 + jnp.where(upper, 0, 1)
        r1 = r1 + jnp.where(upper, 1, 0)
    g0 = lax.shift_right_logical(r0 + (PEER_GROUP - 1), GROUP_SHIFT)
    g1 = lax.shift_right_logical(r1 + (PEER_GROUP - 1), GROUP_SHIFT)
    g0_ref[0] = g0
    g01_ref[0] = g0 + g1
    rebase = g0 * PEER_GROUP - PEER_SLOTS
    for p in range(PEER_PAIRS):
        pos = pos_scr[p]
        pos_scr[p] = jnp.where(pos >= PEER_SLOTS, pos + rebase, pos)

    def place(g, carry):
        for j in range(PEER_GROUP):
            slot = g * PEER_GROUP + j
            iv = zero
            wv = jnp.zeros(shape, jnp.float32)
            for p in range(PEER_PAIRS):
                hit = pos_scr[p] == slot
                iv = jnp.where(hit, loc_scr[p], iv)
                wv = jnp.where(hit, w_ref[0, p // PEER_TOPK, p % PEER_TOPK], wv)
            idx_ref[j, 0, g] = iv
            gw_ref[0, slot] = wv
        return carry

    lax.fori_loop(0, PEER_GPT, place, 0)
    for j in range(PEER_GROUP):
        for g in range(PEER_GPT, PEER_GSTRIDE):
            idx_ref[j, 0, g] = zero


def _router(e, w, n_experts):
    nchunk, nh, k, sub, _ = e.shape
    sec_experts = n_experts // PEER_SECTIONS
    assert sec_experts & (sec_experts - 1) == 0 and PEER_SECTIONS == 2
    ispec = pl.BlockSpec((1, nh, k, sub, LANES), lambda c: (c, 0, 0, 0, 0))
    bspec = pl.BlockSpec((1, sub, LANES), lambda c: (c, 0, 0))
    scr = lambda dt: pltpu.VMEM((PEER_PAIRS, sub, LANES), dt)
    return pl.pallas_call(
        functools.partial(_router_kernel, sec_experts=sec_experts),
        grid=(nchunk,),
        in_specs=[ispec, ispec],
        out_specs=[pl.BlockSpec((PEER_GROUP, 1, PEER_GSTRIDE, sub, LANES), lambda c: (0, c, 0, 0, 0)),
                   pl.BlockSpec((1, PEER_SLOTS, sub, LANES), lambda c: (c, 0, 0, 0)), bspec, bspec],
        out_shape=[jax.ShapeDtypeStruct((PEER_GROUP, nchunk, PEER_GSTRIDE, sub, LANES), jnp.int32),
                   jax.ShapeDtypeStruct((nchunk, PEER_SLOTS, sub, LANES), jnp.float32),
                   jax.ShapeDtypeStruct((nchunk, sub, LANES), jnp.int32),
                   jax.ShapeDtypeStruct((nchunk, sub, LANES), jnp.int32)],
        scratch_shapes=[scr(jnp.int32), scr(jnp.int32)],
        compiler_params=_cparams(("arbitrary",)),
        name="peer_router",
    )(e, w)


def _peer_lists(g_lo, g_hi):
    nblk = g_lo.shape[0]
    g = jnp.arange(PEER_GPT, dtype=jnp.int32)[None, None, :]
    valid = jnp.logical_and(g >= g_lo[:, :, None], g < g_hi[:, :, None]).reshape(nblk, PEER_TB * PEER_GPT)
    cand = jnp.argsort(jnp.logical_not(valid), axis=1, stable=True)[:, :PEER_QCAP].astype(jnp.int32)
    count = jnp.sum(valid, axis=1, keepdims=True).astype(jnp.int32)
    live = jnp.arange(PEER_QCAP, dtype=jnp.int32)[None, :] < count
    t = cand // PEER_GPT
    xrow = jnp.where(live, t * SUBLANES, 0)
    xrow = jnp.where(live, xrow, jnp.max(xrow, axis=1, keepdims=True))
    gt = jnp.where(live, (cand % PEER_GPT) * PEER_TB + t, PEER_SPARE_G * PEER_TB)
    nq = (count[:, 0] + PEER_UNROLL - 1) // PEER_UNROLL * PEER_UNROLL
    return nq, jnp.min(g_lo, axis=1), jnp.max(g_hi, axis=1), xrow, gt


def _peer(h2, e, gw, u_rows, v_rows):
    n, d = h2.shape
    assert d == VREG_ELEMS and n % PEER_TB == 0 and PEER_TB == LANES
    n_experts = u_rows.shape[0] // SUBLANES
    sec_rows = u_rows.shape[0] // PEER_SECTIONS
    nblk = n // PEER_TB
    nstep = PEER_SECTIONS * nblk
    idx5, gw4, g0, g01 = _router(e, gw, n_experts)
    idx_by_slot = jnp.transpose(idx5, (0, 1, 3, 2, 4)).reshape(PEER_GROUP, nblk * PEER_GSTRIDE * PEER_TB)
    idx_by_slot = [idx_by_slot[j] for j in range(PEER_GROUP)]
    rows_per_g = PEER_GROUP * PEER_TB
    gw_blk = jnp.transpose(gw4, (0, 2, 1, 3)).reshape(nblk, PEER_GPT, rows_per_g)
    g0 = g0.reshape(nblk, PEER_TB)
    g01 = g01.reshape(nblk, PEER_TB)
    bounds = (jnp.zeros_like(g0), g0, g01)
    lists = [_peer_lists(bounds[s], bounds[s + 1]) for s in range(PEER_SECTIONS)]
    nq, glo, ghi, xrow, gt = [jnp.stack(a).reshape(-1) for a in zip(*lists)]

    npre = 3
    lst = pl.BlockSpec((PEER_QCAP,), lambda s, i, *_: (s * nblk + i,), memory_space=pltpu.SMEM)
    slots = [pl.BlockSpec((PEER_TB * PEER_GSTRIDE,), lambda s, i, *_: (i,), memory_space=pltpu.SMEM)
             ] * PEER_GROUP
    table = pl.BlockSpec(memory_space=pl.ANY)
    tab_scratch = pltpu.VMEM((sec_rows, LANES), jnp.float32)
    slot_rows = PEER_GPT * rows_per_g
    act = pl.pallas_call(
        _peer_u_kernel,
        grid_spec=pltpu.PrefetchScalarGridSpec(
            num_scalar_prefetch=npre,
            grid=(PEER_SECTIONS, nblk),
            in_specs=[lst, lst] + slots + [pl.BlockSpec((PEER_TB * SUBLANES, LANES), lambda s, i, *_: (i, 0)), table],
            out_specs=pl.BlockSpec((1, PEER_GPT, rows_per_g), lambda s, i, *_: (s * nblk + i, 0, 0)),
            scratch_shapes=[tab_scratch, pltpu.VMEM((slot_rows, LANES), jnp.float32), pltpu.SemaphoreType.DMA]),
        out_shape=jax.ShapeDtypeStruct((nstep, PEER_GPT, rows_per_g), jnp.float32),
        compiler_params=_cparams(("arbitrary", "arbitrary")),
        name="peer_u",
    )(nq, glo, ghi, xrow, gt, *idx_by_slot, h2.reshape(n * SUBLANES, LANES), u_rows)
    bg = 8 if nblk % 8 == 0 else nblk
    w = pl.pallas_call(
        _gate_kernel,
        grid=(PEER_SECTIONS, nblk // bg),
        in_specs=[pl.BlockSpec((bg, PEER_GPT, rows_per_g), lambda s, i: (s * (nblk // bg) + i, 0, 0)),
                  pl.BlockSpec((bg, PEER_GPT, rows_per_g), lambda s, i: (i, 0, 0))],
        out_specs=pl.BlockSpec((bg, PEER_GPT, rows_per_g), lambda s, i: (s * (nblk // bg) + i, 0, 0)),
        out_shape=jax.ShapeDtypeStruct((nstep, PEER_GPT, rows_per_g), jnp.float32),
        compiler_params=_cparams(("arbitrary", "arbitrary")),
        name="peer_gate",
    )(act, gw_blk)
    parts = pl.pallas_call(
        _peer_v_kernel,
        grid_spec=pltpu.PrefetchScalarGridSpec(
            num_scalar_prefetch=npre,
            grid=(PEER_SECTIONS, nblk),
            in_specs=[lst, lst] + slots + [
                pl.BlockSpec((1, PEER_SLOTS, LANES), lambda s, i, *_: (s * nblk + i, 0, 0)), table],
            out_specs=pl.BlockSpec((1, PEER_TB * SUBLANES, LANES), lambda s, i, *_: (s, i, 0)),
            scratch_shapes=[tab_scratch, pltpu.VMEM((slot_rows, LANES), jnp.float32),
                            pltpu.VMEM((PEER_TB * SUBLANES + SUBLANES, LANES), jnp.float32),
                            pltpu.SemaphoreType.DMA]),
        out_shape=jax.ShapeDtypeStruct((PEER_SECTIONS, n * SUBLANES, LANES), jnp.float32),
        compiler_params=_cparams(("arbitrary", "arbitrary")),
        name="peer_v",
    )(nq, glo, ghi, xrow, gt, *idx_by_slot, w.reshape(nstep, PEER_SLOTS, LANES), v_rows)
    return parts.reshape(PEER_SECTIONS, n, d)


def _final_kernel(x1_ref, ff_ref, g2_ref, w_ref, b_ref, o_ref, *, alpha):
    ff = ff_ref[0] + ff_ref[1]
    o_ref[...] = _ln(alpha * x1_ref[...] + g2_ref[...] * ff) * w_ref[...] + b_ref[...]


def _final(x1, parts, g2, mod_map, tm, ln2w, ln2b, alpha):
    n, d = x1.shape
    rm = g2.shape[1] if g2.shape[1] == 1 else tm
    vec = pl.BlockSpec((1, d), lambda i: (0, 0))
    return pl.pallas_call(
        functools.partial(_final_kernel, alpha=alpha),
        grid=(n // tm,),
        in_specs=[pl.BlockSpec((tm, d), lambda i: (i, 0)),
                  pl.BlockSpec((PEER_SECTIONS, tm, d), lambda i: (0, i, 0)),
                  pl.BlockSpec((None, rm, d), lambda i: mod_map(i) + (0,)), vec, vec],
        out_specs=pl.BlockSpec((tm, d), lambda i: (i, 0)),
        out_shape=jax.ShapeDtypeStruct((n, d), jnp.float32),
        compiler_params=_cparams(("arbitrary",)),
        name="final_ln",
    )(x1, parts, g2, ln2w, ln2b)


def _layer_weights(l, w_in, w_gla_up, b_gla, gla_norm_w, w_br_a, w_br_b, w_out, ln1_w, ln1_b, w_pq,
                   peer_k1, peer_k2, ln2_w, ln2_b):
    d = w_in.shape[1]
    bf = lambda a: a.astype(jnp.bfloat16)
    sizes = (W_A, W_A, W_A, W_BK, W_BK, W_BV, W_BV, GATE_RANK, d, d)
    offs = np.concatenate([[0], np.cumsum(sizes)])
    col = lambda i, j=None: w_in[l][:, offs[i]:offs[(i if j is None else j) + 1]]
    pad_rank = GLR_PAD - GATE_RANK
    return {
        "wa": bf(col(0, 2)), "wqb": bf(col(3)), "wkb": bf(col(4)), "wvb": bf(col(5)), "wrb": bf(col(6)),
        "wglr": bf(jnp.pad(col(7), ((0, 0), (0, pad_rank)))), "wga": bf(col(8)), "wgb": bf(col(9)),
        "wup": bf(jnp.pad(w_gla_up[l], ((0, pad_rank), (0, 0)))), "bup": b_gla[l].reshape(1, -1),
        "gnw": gla_norm_w[l].reshape(1, -1), "wbra": bf(w_br_a[l]), "wbrb": bf(w_br_b[l]), "wout": bf(w_out[l]),
        "ln1w": ln1_w[l].reshape(1, -1), "ln1b": ln1_b[l].reshape(1, -1), "wpq": bf(w_pq[l]),
        "k1": bf(peer_k1[l]), "k2": bf(peer_k2[l]),
        "ln2w": ln2_w[l].reshape(1, -1), "ln2b": ln2_b[l].reshape(1, -1),
    }


def _ffn_and_norm(x1, h2, s1t, s2t, g2, mod_map, tm, wts, u_rows, v_rows, alpha):
    e, gw = _topk(s1t, s2t)
    parts = _peer(h2, e, gw, u_rows, v_rows)
    return _final(x1, parts, g2, mod_map, tm, wts["ln2w"], wts["ln2b"], alpha)


def kernel(x_prompt, x_sample, c_prompt, c_sample, cache_kv_w128, cache_kv_w512, cache_kv_w2048, state_gla, w_ada, b_ada, w_in, w_gla_up, b_gla, gla_norm_w, w_br_a, w_br_b, w_out, ln1_w, ln1_b, w_pq, peer_k1, peer_k2, peer_u, peer_v, ln2_w, ln2_b):
    depth = w_ada.shape[0]
    b, t, d = x_prompt.shape
    db, ds, _ = x_sample.shape
    assert ds == 1, "the single-token kernels take one new token per sequence"
    alpha = (2 * depth) ** 0.25
    tm_p = 256
    tm_s = db
    yp = x_prompt.reshape(b * t, d)
    ys = x_sample.reshape(db, d)
    caches = (cache_kv_w128, cache_kv_w512, cache_kv_w2048)
    kv_p = [[] for _ in DIL_CONFIGS]
    kv_s = [[] for _ in DIL_CONFIGS]
    gla_p, gla_s = [], []
    nc = b + db
    nc_pad = -(-nc // SUBLANES) * SUBLANES
    c_all = jnp.pad(jnp.concatenate([c_prompt, c_sample], axis=0), ((0, nc_pad - nc), (0, 0)))
    map_p = lambda i: (i // (t // tm_p), 0)
    map_s = lambda i: (0, i)
    for l in range(depth):
        wts = _layer_weights(l, w_in, w_gla_up, b_gla, gla_norm_w, w_br_a, w_br_b, w_out, ln1_w, ln1_b, w_pq,
                             peer_k1, peer_k2, ln2_w, ln2_b)
        u_rows = peer_u[l].reshape(-1, LANES)
        v_rows = peer_v[l].reshape(-1, LANES)
        mod = _adaln(c_all, w_ada[l], b_ada[l])
        mods_p = [m.reshape(b, 1, d) for m in jnp.split(mod[:b], 6, axis=-1)]
        mods_s = [m.reshape(1, db, d) for m in jnp.split(mod[b:b + db], 6, axis=-1)]

        a, qb, kb, vb, rb, gd, ga, gb = _inproj(yp, mods_p[0], mods_p[1], map_p, tm_p, wts)
        a3 = a.reshape(b, t, 3 * W_A)
        o_g, l_g = [], []
        for gi, (window, dil) in enumerate(DIL_CONFIGS):
            o, lse = _dil_prompt(a3, gi, dil)
            o_g.append(o.reshape(b * t, W_G))
            l_g.append(lse.reshape(b * t, W_G))
            keep = min(window, t)
            k_last = a3[:, t - keep:, W_A + gi * W_G:W_A + (gi + 1) * W_G]
            v_last = a3[:, t - keep:, 2 * W_A + gi * W_G:2 * W_A + (gi + 1) * W_G]
            kv_p[gi].append(jnp.stack([k_last, v_last], axis=2).reshape(b, keep, 2, HEADS_PER_GROUP, HEAD_DIM_A))
        og, s_fin = _gla_prompt(qb.reshape(b, t, W_BK), kb.reshape(b, t, W_BK), vb.reshape(b, t, W_BV),
                                gd.reshape(b, t, W_BK))
        gla_p.append(s_fin)
        x1, h2, s1t, s2t = _merge(o_g, l_g, og.reshape(b * t, W_BV), rb, ga, gb, yp, mods_p[2], mods_p[3],
                                  mods_p[4], map_p, tm_p, wts, alpha)
        yp = _ffn_and_norm(x1, h2, s1t, s2t, mods_p[5], map_p, tm_p, wts, u_rows, v_rows, alpha)

        a, qb, kb, vb, rb, gd, ga, gb = _inproj(ys, mods_s[0], mods_s[1], map_s, tm_s, wts)
        layer_caches = [c[l] for c in caches]
        o, lse, new_caches = _dil_sample(a, layer_caches)
        for gi in range(N_GROUPS):
            kv_s[gi].append(new_caches[gi])
        og, s_new = _gla_sample(qb, kb, vb, gd, state_gla[l])
        gla_s.append(s_new)
        o_g = [o[:, gi * W_G:(gi + 1) * W_G] for gi in range(N_GROUPS)]
        l_g = [lse[:, gi * W_G:(gi + 1) * W_G] for gi in range(N_GROUPS)]
        x1, h2, s1t, s2t = _merge(o_g, l_g, og, rb, ga, gb, ys, mods_s[2], mods_s[3], mods_s[4], map_s, tm_s,
                                  wts, alpha)
        ys = _ffn_and_norm(x1, h2, s1t, s2t, mods_s[5], map_s, tm_s, wts, u_rows, v_rows, alpha)

    return (yp.reshape(b, t, d), ys.reshape(db, ds, d),
            jnp.stack(kv_p[0]), jnp.stack(kv_p[1]), jnp.stack(kv_p[2]), jnp.stack(gla_p),
            jnp.stack(kv_s[0]), jnp.stack(kv_s[1]), jnp.stack(kv_s[2]), jnp.stack(gla_s))
```

```python
import functools

import numpy as np
import jax
import jax.numpy as jnp
from jax import lax
from jax.experimental import pallas as pl
from jax.experimental.pallas import tpu as pltpu

DIL_CONFIGS = ((128, 1), (512, 4), (2048, 16))
N_GROUPS = 3
HEADS_PER_GROUP = 4
HEAD_DIM_A = 64
W_G = HEADS_PER_GROUP * HEAD_DIM_A
W_A = N_GROUPS * W_G
DIL_BLOCK = 128
N_HEADS_B = 4
HEAD_K_B = 128
HEAD_V_B = 256
W_BK = N_HEADS_B * HEAD_K_B
W_BV = N_HEADS_B * HEAD_V_B
GATE_RANK = 16
GATE_TEMP = 16.0
GLR_PAD = 128
GLA_CHUNK = 64
GLA_SUB = 16
PEER_HEADS = 8
PEER_KEYS = 128
PEER_TOPK = 16
TOPK_SHIFT = PEER_TOPK.bit_length() - 1
PEER_PAIRS = PEER_HEADS * PEER_TOPK
LN_EPS = 1e-5

LANES = 128
SUBLANES = 8
VREG_ELEMS = LANES * SUBLANES
VMEM_LIMIT = 56 * 1024 * 1024

PEER_SECTIONS = 2
PEER_GROUP = SUBLANES
PEER_SLOTS = PEER_PAIRS + PEER_SECTIONS * PEER_GROUP
GROUP_SHIFT = PEER_GROUP.bit_length() - 1
PEER_GPT = PEER_SLOTS // PEER_GROUP
PEER_SPARE_G = PEER_GPT - 1
PEER_GSTRIDE = 32
PEER_TB = LANES
PEER_QCAP = PEER_TB * (PEER_PAIRS // PEER_GROUP)
PEER_UNROLL = 16
PEER_UNROLL_V = 8

_HI = lax.Precision.HIGHEST
_NEG = float("-inf")


def _cparams(sem, vmem=VMEM_LIMIT):
    return pltpu.CompilerParams(dimension_semantics=sem, vmem_limit_bytes=vmem)


def _ln(x):
    mu = jnp.mean(x, axis=-1, keepdims=True)
    xc = x - mu
    var = jnp.mean(xc * xc, axis=-1, keepdims=True)
    return xc * lax.rsqrt(var + LN_EPS)


def _bdot(a, b):
    return jnp.dot(a.astype(jnp.bfloat16), b.astype(jnp.bfloat16), preferred_element_type=jnp.float32)


def _bdot_nt(a, b):
    return lax.dot_general(a.astype(jnp.bfloat16), b.astype(jnp.bfloat16), (((1,), (1,)), ((), ())),
                           preferred_element_type=jnp.float32)


def _alibi_slope(head):
    return float(np.exp2(np.float32(-8.0 * (head + 1) / (N_GROUPS * HEADS_PER_GROUP))))


def _ada_kernel(c_ref, w_ref, b_ref, o_ref):
    c = c_ref[...]
    o_ref[...] = _bdot(c * jax.nn.sigmoid(c), w_ref[...]) + b_ref[...]


def _adaln(c, w_ada, b_ada):
    bc, d = c.shape
    ncol = w_ada.shape[1] // d
    return pl.pallas_call(
        _ada_kernel,
        grid=(ncol,),
        in_specs=[pl.BlockSpec((bc, d), lambda j: (0, 0)),
                  pl.BlockSpec((d, d), lambda j: (0, j)),
                  pl.BlockSpec((1, d), lambda j: (0, j))],
        out_specs=pl.BlockSpec((bc, d), lambda j: (0, j)),
        out_shape=jax.ShapeDtypeStruct((bc, ncol * d), jnp.float32),
        compiler_params=_cparams(("arbitrary",)),
        name="adaln",
    )(c, w_ada, b_ada.reshape(1, -1))


def _inproj_kernel(x_ref, sh_ref, sc_ref, wa_ref, wqb_ref, wkb_ref, wvb_ref, wrb_ref, wglr_ref, wga_ref, wgb_ref,
                   wup_ref, bup_ref, a_ref, qb_ref, kb_ref, vb_ref, rb_ref, gd_ref, ga_ref, gb_ref):
    h = (_ln(x_ref[...]) * (1.0 + sc_ref[...]) + sh_ref[...]).astype(jnp.bfloat16)

    def proj(w_ref):
        return jnp.dot(h, w_ref[...], preferred_element_type=jnp.float32)

    a_ref[...] = proj(wa_ref)
    qb_ref[...] = proj(wqb_ref) * (HEAD_K_B ** -0.5)
    kb_ref[...] = proj(wkb_ref)
    vb_ref[...] = proj(wvb_ref)
    rb_ref[...] = proj(wrb_ref)
    ga_ref[...] = proj(wga_ref)
    gb_ref[...] = proj(wgb_ref)
    glr = proj(wglr_ref)
    gate = _bdot(glr, wup_ref[...]) + bup_ref[...]
    gd_ref[...] = jax.nn.log_sigmoid(gate) * (1.0 / GATE_TEMP)


def _inproj(x2d, shift, scale, mod_map, tm, wts):
    n, d = x2d.shape
    rm = shift.shape[1] if shift.shape[1] == 1 else tm
    mod_spec = pl.BlockSpec((None, rm, d), lambda i: mod_map(i) + (0,))
    row = lambda w: pl.BlockSpec((tm, w), lambda i: (i, 0))
    const = lambda a: pl.BlockSpec(a.shape, lambda i: (0,) * a.ndim, pipeline_mode=pl.Buffered(1))
    names = ("wa", "wqb", "wkb", "wvb", "wrb", "wglr", "wga", "wgb", "wup", "bup")
    widths = (3 * W_A, W_BK, W_BK, W_BV, W_BV, W_BK, d, d)
    return pl.pallas_call(
        _inproj_kernel,
        grid=(n // tm,),
        in_specs=[row(d), mod_spec, mod_spec] + [const(wts[k]) for k in names],
        out_specs=[row(w) for w in widths],
        out_shape=[jax.ShapeDtypeStruct((n, w), jnp.float32) for w in widths],
        compiler_params=_cparams(("arbitrary",)),
        name="inproj",
    )(x2d, shift, scale, *[wts[k] for k in names])


def _dil_prompt_kernel(q_ref, kc_ref, kp_ref, vc_ref, vp_ref, o_ref, l_ref, o_scr, l_scr, *, group, dil):
    has_prev = pl.program_id(2) > 0
    half = pl.program_id(1)
    heads = LANES // HEAD_DIM_A
    qi = lax.broadcasted_iota(jnp.int32, (DIL_BLOCK, DIL_BLOCK), 0)
    ki = lax.broadcasted_iota(jnp.int32, (DIL_BLOCK, DIL_BLOCK), 1)
    valid_p = jnp.logical_and(ki >= qi, has_prev)
    valid_c = ki <= qi
    dist_p = ((qi + DIL_BLOCK - ki) * dil).astype(jnp.float32)
    dist_c = ((qi - ki) * dil).astype(jnp.float32)

    def phase(r, carry):
        rows = pl.ds(r, DIL_BLOCK, stride=dil) if dil > 1 else pl.ds(0, DIL_BLOCK)
        q_all, kc, kp, vc, vp = (ref[rows, :] for ref in (q_ref, kc_ref, kp_ref, vc_ref, vp_ref))
        outs, lses = [], []
        for hh in range(heads):
            slope = jnp.where(half == 0, _alibi_slope(group * HEADS_PER_GROUP + hh),
                              _alibi_slope(group * HEADS_PER_GROUP + heads + hh))
            sl = slice(hh * HEAD_DIM_A, (hh + 1) * HEAD_DIM_A)
            q = q_all[:, sl]
            sp = _bdot_nt(q, kp[:, sl]) * (HEAD_DIM_A ** -0.5) - slope * dist_p
            sc = _bdot_nt(q, kc[:, sl]) * (HEAD_DIM_A ** -0.5) - slope * dist_c
            sp = jnp.where(valid_p, sp, _NEG)
            sc = jnp.where(valid_c, sc, _NEG)
            m = jnp.maximum(jnp.max(sp, axis=-1, keepdims=True), jnp.max(sc, axis=-1, keepdims=True))
            pp = jnp.exp(sp - m)
            pc = jnp.exp(sc - m)
            z = jnp.sum(pp, axis=-1, keepdims=True) + jnp.sum(pc, axis=-1, keepdims=True)
            outs.append((_bdot(pp, vp[:, sl]) + _bdot(pc, vc[:, sl])) / z)
            lses.append(jnp.broadcast_to(m + jnp.log(z), (DIL_BLOCK, HEAD_DIM_A)))
        o_scr[r] = jnp.concatenate(outs, axis=-1)
        l_scr[r] = jnp.concatenate(lses, axis=-1)
        return carry

    lax.fori_loop(0, dil, phase, 0)
    for r in range(dil):
        rows = pl.ds(r, DIL_BLOCK, stride=dil) if dil > 1 else pl.ds(0, DIL_BLOCK)
        o_ref[rows, :] = o_scr[r]
        l_ref[rows, :] = l_scr[r]


def _dil_prompt(qkv, group, dil):
    b, t, wq = qkv.shape
    span = dil * DIL_BLOCK
    assert t % span == 0
    halves = W_G // LANES
    assert halves == 2 and HEADS_PER_GROUP * HEAD_DIM_A == W_G
    qcol, kcol, vcol = group, N_GROUPS + group, 2 * N_GROUPS + group
    blk = (None, span, LANES)
    cur = lambda col: pl.BlockSpec(blk, lambda bi, h, n: (bi, n, halves * col + h))
    prev = lambda col: pl.BlockSpec(blk, lambda bi, h, n: (bi, jnp.maximum(n - 1, 0), halves * col + h))
    ospec = pl.BlockSpec(blk, lambda bi, h, n: (bi, n, h))
    scr = pltpu.VMEM((dil, DIL_BLOCK, LANES), jnp.float32)
    return pl.pallas_call(
        functools.partial(_dil_prompt_kernel, group=group, dil=dil),
        grid=(b, halves, t // span),
        in_specs=[cur(qcol), cur(kcol), prev(kcol), cur(vcol), prev(vcol)],
        out_specs=[ospec, ospec],
        out_shape=[jax.ShapeDtypeStruct((b, t, W_G), jnp.float32)] * 2,
        scratch_shapes=[scr, scr],
        compiler_params=_cparams(("arbitrary", "arbitrary", "arbitrary")),
        name=f"dil_prompt_g{group}",
    )(qkv, qkv, qkv, qkv, qkv)


def _gla_prompt_kernel(q_ref, k_ref, v_ref, g_ref, o_ref, sfin_ref, s_scr):
    c = pl.program_id(1)
    nchunk = pl.num_programs(1)
    C = GLA_CHUNK

    @pl.when(c == 0)
    def _():
        s_scr[...] = jnp.zeros_like(s_scr)

    ri = lax.broadcasted_iota(jnp.int32, (C, C), 0)
    ci = lax.broadcasted_iota(jnp.int32, (C, C), 1)
    tri = (ri >= ci).astype(jnp.float32)
    bcum = jnp.dot(tri, g_ref[...], precision=_HI, preferred_element_type=jnp.float32)
    row16 = lax.broadcasted_iota(jnp.int32, (GLA_SUB, HEAD_K_B), 0)
    lane16 = lax.broadcasted_iota(jnp.int32, (GLA_SUB, LANES), 1)
    rowc = lax.broadcasted_iota(jnp.int32, (C, HEAD_K_B), 0)
    nsub = C // GLA_SUB
    outs = []
    for h in range(N_HEADS_B):
        ks = slice(h * HEAD_K_B, (h + 1) * HEAD_K_B)
        vs = slice(h * HEAD_V_B, (h + 1) * HEAD_V_B)
        bh = bcum[:, ks]
        qh = q_ref[:, ks]
        kh = k_ref[:, ks]
        vh = v_ref[:, vs]
        sh = s_scr[h]
        o_inter = _bdot(qh * jnp.exp(bh), sh)
        arows = []
        for i in range(nsub):
            r0 = i * GLA_SUB
            bi = bh[r0:r0 + GLA_SUB]
            qi_ = qh[r0:r0 + GLA_SUB]
            ki_ = kh[r0:r0 + GLA_SUB]
            a = jnp.zeros((GLA_SUB, LANES), jnp.float32)
            for s in range(GLA_SUB):
                e = jnp.exp(jnp.where(row16 >= s, bi - bi[s:s + 1], _NEG))
                col = jnp.sum(qi_ * (ki_[s:s + 1] * e), axis=-1, keepdims=True)
                a = jnp.where(lane16 == r0 + s, col, a)
            a = a[:, :C]
            if i > 0:
                b0 = bi[0:1]
                qt = qi_ * jnp.exp(bi - b0)
                kt = kh * jnp.exp(jnp.where(rowc < r0, b0 - bh, _NEG))
                a = a + _bdot_nt(qt, kt)
            arows.append(a)
        amat = jnp.concatenate(arows, axis=0)
        outs.append(o_inter + _bdot(amat, vh))
        bl = bh[C - 1:C]
        kt = kh * jnp.exp(bl - bh)
        dcol = jnp.transpose(jnp.broadcast_to(jnp.exp(bl), (SUBLANES, HEAD_K_B)))[:, 0:1]
        upd = lax.dot_general(kt.astype(jnp.bfloat16), vh.astype(jnp.bfloat16), (((0,), (0,)), ((), ())),
                              preferred_element_type=jnp.float32)
        s_scr[h] = dcol * sh + upd
    o_ref[...] = jnp.concatenate(outs, axis=-1)

    @pl.when(c == nchunk - 1)
    def _():
        sfin_ref[...] = s_scr[...]


def _gla_prompt(qb, kb, vb, gd):
    b, t, _ = qb.shape
    assert t % GLA_CHUNK == 0
    spec = lambda w: pl.BlockSpec((None, GLA_CHUNK, w), lambda bi, c: (bi, c, 0))
    sshape = (N_HEADS_B, HEAD_K_B, HEAD_V_B)
    return pl.pallas_call(
        _gla_prompt_kernel,
        grid=(b, t // GLA_CHUNK),
        in_specs=[spec(W_BK), spec(W_BK), spec(W_BV), spec(W_BK)],
        out_specs=[spec(W_BV), pl.BlockSpec((None,) + sshape, lambda bi, c: (bi, 0, 0, 0))],
        out_shape=[jax.ShapeDtypeStruct((b, t, W_BV), jnp.float32),
                   jax.ShapeDtypeStruct((b,) + sshape, jnp.float32)],
        scratch_shapes=[pltpu.VMEM(sshape, jnp.float32)],
        compiler_params=_cparams(("arbitrary", "arbitrary")),
        name="gla_prompt",
    )(qb, kb, vb, gd)


SAMPLE_SEQS = 8


def _to_col(row):
    return jnp.transpose(jnp.broadcast_to(row, (SUBLANES, row.shape[1])))[:, 0:1]


def _to_row(col):
    return jnp.transpose(jnp.broadcast_to(col, (col.shape[0], SUBLANES)))[0:1, :]


def _dil_sample_kernel(a_ref, c0_ref, c1_ref, c2_ref, o_ref, l_ref, n0_ref, n1_ref, n2_ref):
    col = _to_col(a_ref[0])
    scale = HEAD_DIM_A ** -0.5
    kv_w = 2 * W_G
    for g, (cref, nref, (window, dil)) in enumerate(zip((c0_ref, c1_ref, c2_ref), (n0_ref, n1_ref, n2_ref),
                                                        DIL_CONFIGS)):
        q = col[g * W_G:(g + 1) * W_G]
        knew = col[W_A + g * W_G:W_A + (g + 1) * W_G]
        vnew = col[2 * W_A + g * W_G:2 * W_A + (g + 1) * W_G]
        x = cref[0]
        lane = lax.broadcasted_iota(jnp.int32, (1, window), 1)
        on_stride = (lane & (dil - 1)) == 0
        dist = (window - lane).astype(jnp.float32)
        o_cols, l_cols = [], []
        for hh in range(HEADS_PER_GROUP):
            rows = slice(hh * HEAD_DIM_A, (hh + 1) * HEAD_DIM_A)
            qh = q[rows]
            s = jnp.sum(x[rows, :] * qh, axis=0, keepdims=True) * scale
            s = s - _alibi_slope(g * HEADS_PER_GROUP + hh) * dist
            s = jnp.where(on_stride, s, _NEG)
            ss = jnp.sum(knew[rows] * qh, axis=0, keepdims=True) * scale
            m = jnp.maximum(jnp.max(s, axis=1, keepdims=True), ss)
            p = jnp.exp(s - m)
            ps = jnp.exp(ss - m)
            z = jnp.sum(p, axis=1, keepdims=True) + ps
            vrows = slice(W_G + hh * HEAD_DIM_A, W_G + (hh + 1) * HEAD_DIM_A)
            o_cols.append((jnp.sum(x[vrows, :] * p, axis=1, keepdims=True) + ps * vnew[rows]) / z)
            l_cols.append(jnp.broadcast_to(m + jnp.log(z), (HEAD_DIM_A, 1)))
        o_ref[0, :, g * W_G:(g + 1) * W_G] = _to_row(jnp.concatenate(o_cols, axis=0))
        l_ref[0, :, g * W_G:(g + 1) * W_G] = _to_row(jnp.concatenate(l_cols, axis=0))
        shifted = pltpu.roll(x, window - 1, axis=1)
        nref[0] = jnp.where(lane == window - 1, jnp.concatenate([knew, vnew], axis=0), shifted)


def _dil_sample(qkv, caches):
    db = qkv.shape[0]
    kv_w = 2 * W_G
    views, cspecs, oshapes = [], [], []
    for cache, (window, dil) in zip(caches, DIL_CONFIGS):
        assert cache.shape[1] == window, "window caches shorter than the window are not supported"
        assert dil & (dil - 1) == 0
        views.append(jnp.transpose(cache, (0, 2, 3, 4, 1)).reshape(db, kv_w, window))
        cspecs.append(pl.BlockSpec((1, kv_w, window), lambda i: (i, 0, 0)))
        oshapes.append(jax.ShapeDtypeStruct((db, kv_w, window), jnp.float32))
    rspec = lambda w: pl.BlockSpec((1, 1, w), lambda i: (i, 0, 0))
    o, lse, *new = pl.pallas_call(
        _dil_sample_kernel,
        grid=(db,),
        in_specs=[rspec(3 * W_A)] + cspecs,
        out_specs=[rspec(W_A), rspec(W_A)] + cspecs,
        out_shape=[jax.ShapeDtypeStruct((db, 1, W_A), jnp.float32)] * 2 + oshapes,
        compiler_params=_cparams(("arbitrary",)),
        name="dil_sample",
    )(qkv.reshape(db, 1, 3 * W_A), *views)
    new = [jnp.transpose(n.reshape(db, 2, HEADS_PER_GROUP, HEAD_DIM_A, n.shape[-1]), (0, 4, 1, 2, 3)) for n in new]
    return o.reshape(db, W_A), lse.reshape(db, W_A), new


def _gla_sample_kernel(q_ref, k_ref, v_ref, g_ref, s0_ref, o_ref, s_ref):
    for h in range(N_HEADS_B):
        ks = slice(h * HEAD_K_B, (h + 1) * HEAD_K_B)
        vs = slice(h * HEAD_V_B, (h + 1) * HEAD_V_B)
        qT = jnp.transpose(q_ref[:, ks])
        kT = jnp.transpose(k_ref[:, ks])
        aT = jnp.transpose(jnp.exp(g_ref[:, ks]))
        for j in range(SAMPLE_SEQS):
            s_new = aT[:, j:j + 1] * s0_ref[j, h] + kT[:, j:j + 1] * v_ref[j:j + 1, vs]
            s_ref[j, h] = s_new
            o_ref[j:j + 1, vs] = jnp.sum(qT[:, j:j + 1] * s_new, axis=0, keepdims=True)


def _gla_sample(qb, kb, vb, gd, s0):
    db = qb.shape[0]
    row = lambda w: pl.BlockSpec((SAMPLE_SEQS, w), lambda i: (i, 0))
    sspec = pl.BlockSpec((SAMPLE_SEQS, N_HEADS_B, HEAD_K_B, HEAD_V_B), lambda i: (i, 0, 0, 0))
    return pl.pallas_call(
        _gla_sample_kernel,
        grid=(db // SAMPLE_SEQS,),
        in_specs=[row(W_BK), row(W_BK), row(W_BV), row(W_BK), sspec],
        out_specs=[row(W_BV), sspec],
        out_shape=[jax.ShapeDtypeStruct((db, W_BV), jnp.float32), jax.ShapeDtypeStruct(s0.shape, jnp.float32)],
        compiler_params=_cparams(("arbitrary",)),
        name="gla_sample",
    )(qb, kb, vb, gd, s0)


def _merge_kernel(o0_ref, o1_ref, o2_ref, l0_ref, l1_ref, l2_ref, og_ref, rb_ref, ga_ref, gb_ref, x_ref,
                  g1_ref, sh2_ref, sc2_ref, gnw_ref, wbra_ref, wbrb_ref, wout_ref, ln1w_ref, ln1b_ref,
                  wpq_ref, k1_ref, k2_ref, x1_ref, h2_ref, s1_ref, s2_ref, *, alpha):
    l0, l1, l2 = l0_ref[...], l1_ref[...], l2_ref[...]
    m = jnp.maximum(jnp.maximum(l0, l1), l2)
    e0, e1, e2 = jnp.exp(l0 - m), jnp.exp(l1 - m), jnp.exp(l2 - m)
    oa = (e0 * o0_ref[...] + e1 * o1_ref[...] + e2 * o2_ref[...]) / (e0 + e1 + e2)
    og = og_ref[...]
    parts = []
    for h in range(N_HEADS_B):
        oh = og[:, h * HEAD_V_B:(h + 1) * HEAD_V_B]
        parts.append(oh * lax.rsqrt(jnp.mean(oh * oh, axis=-1, keepdims=True) + LN_EPS))
    rb = rb_ref[...]
    ob = jnp.concatenate(parts, axis=-1) * gnw_ref[...] * (rb * jax.nn.sigmoid(rb))
    merged = (jax.nn.sigmoid(ga_ref[...]) * _bdot(oa, wbra_ref[...])
              + jax.nn.sigmoid(gb_ref[...]) * _bdot(ob, wbrb_ref[...]))
    mix = _bdot(merged, wout_ref[...])
    x1 = _ln(alpha * x_ref[...] + g1_ref[...] * mix) * ln1w_ref[...] + ln1b_ref[...]
    x1_ref[...] = x1
    h2 = _ln(x1) * (1.0 + sc2_ref[...]) + sh2_ref[...]
    h2_ref[...] = h2
    qv = _bdot(h2, wpq_ref[...]).astype(jnp.bfloat16)
    half = PEER_KEYS
    for h in range(PEER_HEADS):
        base = h * 2 * half
        s1_ref[h] = lax.dot_general(k1_ref[h], qv[:, base:base + half], (((1,), (1,)), ((), ())),
                                    preferred_element_type=jnp.float32)
        s2_ref[h] = lax.dot_general(k2_ref[h], qv[:, base + half:base + 2 * half], (((1,), (1,)), ((), ())),
                                    preferred_element_type=jnp.float32)


def _merge(o_g, l_g, og, rb, ga, gb, x2d, g1, sh2, sc2, mod_map, tm, wts, alpha):
    n, d = x2d.shape
    rm = g1.shape[1] if g1.shape[1] == 1 else tm
    mod_spec = pl.BlockSpec((None, rm, d), lambda i: mod_map(i) + (0,))
    row = lambda w: pl.BlockSpec((tm, w), lambda i: (i, 0))
    const = lambda a: pl.BlockSpec(a.shape, lambda i: (0,) * a.ndim, pipeline_mode=pl.Buffered(1))
    names = ("gnw", "wbra", "wbrb", "wout", "ln1w", "ln1b", "wpq", "k1", "k2")
    sspec = pl.BlockSpec((PEER_HEADS, PEER_KEYS, tm), lambda i: (0, 0, i))
    return pl.pallas_call(
        functools.partial(_merge_kernel, alpha=alpha),
        grid=(n // tm,),
        in_specs=[row(W_G)] * 6 + [row(W_BV), row(W_BV), row(d), row(d), row(d), mod_spec, mod_spec, mod_spec]
                 + [const(wts[k]) for k in names],
        out_specs=[row(d), row(d), sspec, sspec],
        out_shape=[jax.ShapeDtypeStruct((n, d), jnp.float32)] * 2
                  + [jax.ShapeDtypeStruct((PEER_HEADS, PEER_KEYS, n), jnp.float32)] * 2,
        compiler_params=_cparams(("arbitrary",)),
        name="merge",
    )(*o_g, *l_g, og, rb, ga, gb, x2d, g1, sh2, sc2, *[wts[k] for k in names])


_CAND_IDS = tuple(i * PEER_TOPK + j for i in range(PEER_TOPK) for j in range(PEER_TOPK)
                  if (i + 1) * (j + 1) <= PEER_TOPK)
_NO_ID = PEER_KEYS * PEER_KEYS
TOPK_CHUNK = 16


def _topk_kernel(s1_ref, s2_ref, e_ref, w_ref, sa, sb, va, vb, ia, ib, cand, sc_scr, ci_scr):
    sub = s1_ref.shape[2]
    shape = (sub, LANES)
    sa[...] = s1_ref[0]
    sb[...] = s2_ref[0]

    def tree(op, xs):
        xs = list(xs)
        while len(xs) > 1:
            xs = [op(xs[k], xs[k + 1]) for k in range(0, len(xs) - 1, 2)] + (xs[-1:] if len(xs) % 2 else [])
        return xs[0]

    def extract(s_scr, ids, r, v_out, i_out):
        n = len(ids)
        chunks = [range(c, min(c + TOPK_CHUNK, n)) for c in range(0, n, TOPK_CHUNK)]
        m = tree(jnp.maximum, [tree(jnp.maximum, [s_scr[k] for k in ch]) for ch in chunks])
        idx = tree(jnp.minimum, [tree(jnp.minimum, [jnp.where(s_scr[k] == m, ids[k], _NO_ID) for k in ch])
                                 for ch in chunks])
        for k in range(n):
            s_scr[k] = jnp.where(idx == ids[k], _NEG, s_scr[k])
        v_out[r] = m
        i_out[r] = idx

    def stage1(r, c):
        extract(sa, range(PEER_KEYS), r, va, ia)
        extract(sb, range(PEER_KEYS), r, vb, ib)
        return c

    lax.fori_loop(0, PEER_TOPK, stage1, 0)
    for k, ci in enumerate(_CAND_IDS):
        cand[k] = va[ci // PEER_TOPK] + vb[ci % PEER_TOPK]

    def stage2(r, c):
        extract(cand, _CAND_IDS, r, sc_scr, ci_scr)
        return c

    lax.fori_loop(0, PEER_TOPK, stage2, 0)
    top = sc_scr[0]
    z = jnp.zeros(shape, jnp.float32)
    for r in range(PEER_TOPK):
        z = z + jnp.exp(sc_scr[r] - top)
    for r in range(PEER_TOPK):
        w_ref[0, 0, r] = jnp.exp(sc_scr[r] - top) / z
        ci = ci_scr[r]
        hi = lax.shift_right_logical(ci, TOPK_SHIFT)
        lo = ci & (PEER_TOPK - 1)
        e1 = jnp.zeros(shape, jnp.int32)
        e2 = jnp.zeros(shape, jnp.int32)
        for i in range(PEER_TOPK):
            e1 = jnp.where(hi == i, ia[i], e1)
            e2 = jnp.where(lo == i, ib[i], e2)
        e_ref[0, 0, r] = e1 * PEER_KEYS + e2


def _topk(s1t, s2t):
    nh, nk, n = s1t.shape
    sub = min(SUBLANES, n // LANES)
    nchunk = n // (sub * LANES)
    v1 = s1t.reshape(nh, nk, n // LANES, LANES)
    v2 = s2t.reshape(nh, nk, n // LANES, LANES)
    ispec = pl.BlockSpec((1, nk, sub, LANES), lambda c, h: (h, 0, c, 0))
    ospec = pl.BlockSpec((1, 1, PEER_TOPK, sub, LANES), lambda c, h: (c, h, 0, 0, 0))
    oshape = (nchunk, nh, PEER_TOPK, sub, LANES)
    key = lambda k, dt: pltpu.VMEM((k, sub, LANES), dt)
    e, w = pl.pallas_call(
        _topk_kernel,
        grid=(nchunk, nh),
        in_specs=[ispec, ispec],
        out_specs=[ospec, ospec],
        out_shape=[jax.ShapeDtypeStruct(oshape, jnp.int32), jax.ShapeDtypeStruct(oshape, jnp.float32)],
        scratch_shapes=[key(nk, jnp.float32), key(nk, jnp.float32),
                        key(PEER_TOPK, jnp.float32), key(PEER_TOPK, jnp.float32),
                        key(PEER_TOPK, jnp.int32), key(PEER_TOPK, jnp.int32),
                        key(len(_CAND_IDS), jnp.float32), key(PEER_TOPK, jnp.float32),
                        key(PEER_TOPK, jnp.int32)],
        compiler_params=_cparams(("arbitrary", "arbitrary")),
        name="peer_topk",
    )(v1, v2)
    return e, w


def _load_section(tab_hbm, tab_vmem, sem):
    rows = tab_vmem.shape[0]

    @pl.when(pl.program_id(1) == 0)
    def _():
        start = pl.multiple_of(pl.program_id(0) * rows, SUBLANES)
        cp = pltpu.make_async_copy(tab_hbm.at[pl.ds(start, rows)], tab_vmem, sem)
        cp.start()
        cp.wait()


def _grid_step():
    return pl.program_id(0) * pl.num_programs(1) + pl.program_id(1)


def _peer_u_kernel(nq_ref, glo_ref, ghi_ref, xrow_ref, gt_ref, *refs):
    idx_refs = refs[:PEER_GROUP]
    x_ref, tab_hbm, act_ref, tab_vmem, dbuf, xs, sem = refs[PEER_GROUP:]
    _load_section(tab_hbm, tab_vmem, sem)
    step = _grid_step()

    @pl.when(step == 0)
    def _():
        dbuf[...] = jnp.zeros_like(dbuf)

    for k in range(SUBLANES):
        xs[pl.ds(k, PEER_TB, stride=SUBLANES), :] = x_ref[:, k * LANES:(k + 1) * LANES]

    nq = nq_ref[step]
    sub = lax.broadcasted_iota(jnp.int32, (SUBLANES, LANES), 0)
    order = _fold_slot_order()

    def fold(a, b, sh):
        keep = (sub & sh) == 0
        u = jnp.where(keep, a, b)
        v = jnp.where(keep, b, a)
        if 2 * sh == SUBLANES:
            w = pltpu.roll(v, sh, axis=0)
        else:
            w = jnp.where(keep, pltpu.roll(v, SUBLANES - sh, axis=0), pltpu.roll(v, sh, axis=0))
        return u + w

    def body(i, c):
        for u in range(PEER_UNROLL):
            q = i * PEER_UNROLL + u
            gt = gt_ref[q]
            x = xs[pl.ds(pl.multiple_of(xrow_ref[q], SUBLANES), SUBLANES), :]
            ps = [tab_vmem[pl.ds(pl.multiple_of(idx_refs[order[j]][gt], SUBLANES), SUBLANES), :] * x
                  for j in range(PEER_GROUP)]
            while len(ps) > 1:
                sh = len(ps) // 2
                ps = [fold(ps[2 * k], ps[2 * k + 1], sh) for k in range(sh)]
            dbuf[pl.ds(pl.multiple_of(gt * PEER_GROUP, SUBLANES), SUBLANES), :] = ps[0]
        return c

    lax.fori_loop(0, nq // PEER_UNROLL, body, 0)
    rows_per_g = PEER_GROUP * PEER_TB
    glo, ghi = glo_ref[step], ghi_ref[step]

    def zfill(g, carry):
        act_ref[0, pl.ds(g, 1), :] = jnp.zeros((1, rows_per_g), jnp.float32)
        return carry

    def reduce(g, carry):
        sums = [jnp.sum(jnp.transpose(dbuf[pl.ds(g * rows_per_g + j, PEER_TB, stride=PEER_GROUP), :]),
                        axis=0, keepdims=True) for j in range(PEER_GROUP)]
        act_ref[0, pl.ds(g, 1), :] = jnp.concatenate(sums, axis=-1)
        return carry

    lax.fori_loop(0, glo, zfill, 0)
    lax.fori_loop(glo, ghi, reduce, 0)
    lax.fori_loop(ghi, PEER_GPT, zfill, 0)


def _fold_slot_order():
    pos = [[j] for j in range(PEER_GROUP)]
    sl = [0] * PEER_GROUP
    sh = PEER_GROUP // 2
    groups = pos
    while len(groups) > 1:
        nxt = []
        for i in range(len(groups) // 2):
            for j in groups[2 * i + 1]:
                sl[j] |= sh
            nxt.append(groups[2 * i] + groups[2 * i + 1])
        groups = nxt
        sh //= 2
    return sl


def _peer_v_kernel(nq_ref, glo_ref, ghi_ref, xrow_ref, gt_ref, wrow_ref, *refs):
    idx_refs = refs[:PEER_GROUP]
    w_ref, tab_hbm, out_ref, tab_vmem, wb, acc_scr, sem = refs[PEER_GROUP:]
    _load_section(tab_hbm, tab_vmem, sem)
    step = _grid_step()
    nq = nq_ref[step]
    rows_per_g = PEER_GROUP * PEER_TB

    @pl.when(step == 0)
    def _():
        wb[pl.ds(PEER_SPARE_G * rows_per_g, rows_per_g), :] = jnp.zeros((rows_per_g, LANES), jnp.float32)

    acc_scr[...] = jnp.zeros_like(acc_scr)

    def spread(g, carry):
        for j in range(PEER_GROUP):
            r = g * PEER_GROUP + j
            rep = jnp.broadcast_to(w_ref[0, pl.ds(r, 1), :], (LANES, LANES))
            wb[pl.ds(pl.multiple_of(r * PEER_TB, PEER_TB), PEER_TB), :] = jnp.transpose(rep)
        return carry

    lax.fori_loop(glo_ref[step], ghi_ref[step], spread, 0)
    trash = PEER_TB * SUBLANES

    def body(i, carry):
        cur, acc = carry
        for u in range(PEER_UNROLL_V):
            q = i * PEER_UNROLL_V + u
            gt = gt_ref[q]
            row0 = wrow_ref[q]
            terms = [tab_vmem[pl.ds(pl.multiple_of(idx_refs[j][gt], SUBLANES), SUBLANES), :]
                     * wb[pl.ds(row0 + j * PEER_TB, 1), :] for j in range(PEER_GROUP)]
            while len(terms) > 1:
                terms = [terms[2 * k] + terms[2 * k + 1] for k in range(len(terms) // 2)]
            row = xrow_ref[q]
            acc_scr[pl.ds(pl.multiple_of(cur, SUBLANES), SUBLANES), :] = acc
            acc = jnp.where(row == cur, acc + terms[0], terms[0])
            cur = row
        return cur, acc

    cur, acc = lax.fori_loop(0, nq // PEER_UNROLL_V, body,
                             (jnp.int32(trash), jnp.zeros((SUBLANES, LANES), jnp.float32)))
    acc_scr[pl.ds(pl.multiple_of(cur, SUBLANES), SUBLANES), :] = acc
    for k in range(SUBLANES):
        out_ref[0, :, k * LANES:(k + 1) * LANES] = acc_scr[pl.ds(k, PEER_TB, stride=SUBLANES), :]


def _gate_kernel(a_ref, gw_ref, w_ref):
    act = a_ref[...]
    w_ref[...] = gw_ref[...] * (0.5 * act * (1.0 + lax.erf(act * (2.0 ** -0.5))))


def _router_kernel(e_ref, w_ref, idx_ref, gw_ref, g0_ref, g01_ref, pos_scr, loc_scr, *, sec_experts):
    shape = e_ref.shape[3:]
    zero = jnp.zeros(shape, jnp.int32)
    r0, r1 = zero, zero
    for p in range(PEER_PAIRS):
        e = e_ref[0, p // PEER_TOPK, p % PEER_TOPK]
        upper = e >= sec_experts
        loc_scr[p] = (e & (sec_experts - 1)) * SUBLANES
        pos_scr[p] = jnp.where(upper, r1 + PEER_SLOTS, r0)
        r0 = r0 + jnp.where(upper, 0, 1)
        r1 = r1 + jnp.where(upper, 1, 0)
    g0 = lax.shift_right_logical(r0 + (PEER_GROUP - 1), GROUP_SHIFT)
    g1 = lax.shift_right_logical(r1 + (PEER_GROUP - 1), GROUP_SHIFT)
    g0_ref[0] = g0
    g01_ref[0] = g0 + g1
    rebase = g0 * PEER_GROUP - PEER_SLOTS
    for p in range(PEER_PAIRS):
        pos = pos_scr[p]
        pos_scr[p] = jnp.where(pos >= PEER_SLOTS, pos + rebase, pos)

    def place(g, carry):
        for j in range(PEER_GROUP):
            slot = g * PEER_GROUP + j
            iv = zero
            wv = jnp.zeros(shape, jnp.float32)
            for p in range(PEER_PAIRS):
                hit = pos_scr[p] == slot
                iv = jnp.where(hit, loc_scr[p], iv)
                wv = jnp.where(hit, w_ref[0, p // PEER_TOPK, p % PEER_TOPK], wv)
            idx_ref[j, 0, g] = iv
            gw_ref[0, slot] = wv
        return carry

    lax.fori_loop(0, PEER_GPT, place, 0)
    for j in range(PEER_GROUP):
        for g in range(PEER_GPT, PEER_GSTRIDE):
            idx_ref[j, 0, g] = zero


def _router(e, w, n_experts):
    nchunk, nh, k, sub, _ = e.shape
    sec_experts = n_experts // PEER_SECTIONS
    assert sec_experts & (sec_experts - 1) == 0 and PEER_SECTIONS == 2
    ispec = pl.BlockSpec((1, nh, k, sub, LANES), lambda c: (c, 0, 0, 0, 0))
    bspec = pl.BlockSpec((1, sub, LANES), lambda c: (c, 0, 0))
    scr = lambda dt: pltpu.VMEM((PEER_PAIRS, sub, LANES), dt)
    return pl.pallas_call(
        functools.partial(_router_kernel, sec_experts=sec_experts),
        grid=(nchunk,),
        in_specs=[ispec, ispec],
        out_specs=[pl.BlockSpec((PEER_GROUP, 1, PEER_GSTRIDE, sub, LANES), lambda c: (0, c, 0, 0, 0)),
                   pl.BlockSpec((1, PEER_SLOTS, sub, LANES), lambda c: (c, 0, 0, 0)), bspec, bspec],
        out_shape=[jax.ShapeDtypeStruct((PEER_GROUP, nchunk, PEER_GSTRIDE, sub, LANES), jnp.int32),
                   jax.ShapeDtypeStruct((nchunk, PEER_SLOTS, sub, LANES), jnp.float32),
                   jax.ShapeDtypeStruct((nchunk, sub, LANES), jnp.int32),
                   jax.ShapeDtypeStruct((nchunk, sub, LANES), jnp.int32)],
        scratch_shapes=[scr(jnp.int32), scr(jnp.int32)],
        compiler_params=_cparams(("arbitrary",)),
        name="peer_router",
    )(e, w)


def _peer_lists(g_lo, g_hi):
    nblk = g_lo.shape[0]
    g = jnp.arange(PEER_GPT, dtype=jnp.int32)[None, None, :]
    valid = jnp.logical_and(g >= g_lo[:, :, None], g < g_hi[:, :, None]).reshape(nblk, PEER_TB * PEER_GPT)
    cand = jnp.argsort(jnp.logical_not(valid), axis=1, stable=True)[:, :PEER_QCAP].astype(jnp.int32)
    count = jnp.sum(valid, axis=1, keepdims=True).astype(jnp.int32)
    live = jnp.arange(PEER_QCAP, dtype=jnp.int32)[None, :] < count
    t = cand // PEER_GPT
    xrow = jnp.where(live, t * SUBLANES, 0)
    xrow = jnp.where(live, xrow, jnp.max(xrow, axis=1, keepdims=True))
    gt = jnp.where(live, (cand % PEER_GPT) * PEER_TB + t, PEER_SPARE_G * PEER_TB)
    nq = (count[:, 0] + PEER_UNROLL - 1) // PEER_UNROLL * PEER_UNROLL
    return nq, jnp.min(g_lo, axis=1), jnp.max(g_hi, axis=1), xrow, gt


def _peer(h2, e, gw, u_rows, v_rows):
    n, d = h2.shape
    assert d == VREG_ELEMS and n % PEER_TB == 0 and PEER_TB == LANES
    n_experts = u_rows.shape[0] // SUBLANES
    sec_rows = u_rows.shape[0] // PEER_SECTIONS
    nblk = n // PEER_TB
    nstep = PEER_SECTIONS * nblk
    idx5, gw4, g0, g01 = _router(e, gw, n_experts)
    idx_by_slot = jnp.transpose(idx5, (0, 1, 3, 2, 4)).reshape(PEER_GROUP, nblk * PEER_GSTRIDE * PEER_TB)
    idx_by_slot = [idx_by_slot[j] for j in range(PEER_GROUP)]
    rows_per_g = PEER_GROUP * PEER_TB
    gw_blk = jnp.transpose(gw4, (0, 2, 1, 3)).reshape(nblk, PEER_GPT, rows_per_g)
    g0 = g0.reshape(nblk, PEER_TB)
    g01 = g01.reshape(nblk, PEER_TB)
    bounds = (jnp.zeros_like(g0), g0, g01)
    lists = [_peer_lists(bounds[s], bounds[s + 1]) for s in range(PEER_SECTIONS)]
    nq, glo, ghi, xrow, gt = [jnp.stack(a).reshape(-1) for a in zip(*lists)]
    wrow = gt + (gt // PEER_TB) * ((PEER_GROUP - 1) * PEER_TB)

    npre = 3
    lst = pl.BlockSpec((PEER_QCAP,), lambda s, i, *_: (s * nblk + i,), memory_space=pltpu.SMEM)
    slots = [pl.BlockSpec((PEER_TB * PEER_GSTRIDE,), lambda s, i, *_: (i,), memory_space=pltpu.SMEM)
             ] * PEER_GROUP
    table = pl.BlockSpec(memory_space=pl.ANY)
    tab_scratch = pltpu.VMEM((sec_rows, LANES), jnp.float32)
    slot_rows = PEER_GPT * rows_per_g
    act = pl.pallas_call(
        _peer_u_kernel,
        grid_spec=pltpu.PrefetchScalarGridSpec(
            num_scalar_prefetch=npre,
            grid=(PEER_SECTIONS, nblk),
            in_specs=[lst, lst] + slots + [pl.BlockSpec((PEER_TB, d), lambda s, i, *_: (i, 0)), table],
            out_specs=pl.BlockSpec((1, PEER_GPT, rows_per_g), lambda s, i, *_: (s * nblk + i, 0, 0)),
            scratch_shapes=[tab_scratch, pltpu.VMEM((slot_rows, LANES), jnp.float32),
                            pltpu.VMEM((PEER_TB * SUBLANES, LANES), jnp.float32), pltpu.SemaphoreType.DMA]),
        out_shape=jax.ShapeDtypeStruct((nstep, PEER_GPT, rows_per_g), jnp.float32),
        compiler_params=_cparams(("arbitrary", "arbitrary")),
        name="peer_u",
    )(nq, glo, ghi, xrow, gt, *idx_by_slot, h2, u_rows)
    bg = 8 if nblk % 8 == 0 else nblk
    w = pl.pallas_call(
        _gate_kernel,
        grid=(PEER_SECTIONS, nblk // bg),
        in_specs=[pl.BlockSpec((bg, PEER_GPT, rows_per_g), lambda s, i: (s * (nblk // bg) + i, 0, 0)),
                  pl.BlockSpec((bg, PEER_GPT, rows_per_g), lambda s, i: (i, 0, 0))],
        out_specs=pl.BlockSpec((bg, PEER_GPT, rows_per_g), lambda s, i: (s * (nblk // bg) + i, 0, 0)),
        out_shape=jax.ShapeDtypeStruct((nstep, PEER_GPT, rows_per_g), jnp.float32),
        compiler_params=_cparams(("arbitrary", "arbitrary")),
        name="peer_gate",
    )(act, gw_blk)
    parts = pl.pallas_call(
        _peer_v_kernel,
        grid_spec=pltpu.PrefetchScalarGridSpec(
            num_scalar_prefetch=npre,
            grid=(PEER_SECTIONS, nblk),
            in_specs=[lst, lst, lst] + slots + [
                pl.BlockSpec((1, PEER_SLOTS, LANES), lambda s, i, *_: (s * nblk + i, 0, 0)), table],
            out_specs=pl.BlockSpec((1, PEER_TB, d), lambda s, i, *_: (s, i, 0)),
            scratch_shapes=[tab_scratch, pltpu.VMEM((slot_rows, LANES), jnp.float32),
                            pltpu.VMEM((PEER_TB * SUBLANES + SUBLANES, LANES), jnp.float32),
                            pltpu.SemaphoreType.DMA]),
        out_shape=jax.ShapeDtypeStruct((PEER_SECTIONS, n, d), jnp.float32),
        compiler_params=_cparams(("arbitrary", "arbitrary")),
        name="peer_v",
    )(nq, glo, ghi, xrow, gt, wrow, *idx_by_slot, w.reshape(nstep, PEER_SLOTS, LANES), v_rows)
    return parts


def _final_kernel(x1_ref, ff_ref, g2_ref, w_ref, b_ref, o_ref, *, alpha):
    ff = ff_ref[0] + ff_ref[1]
    o_ref[...] = _ln(alpha * x1_ref[...] + g2_ref[...] * ff) * w_ref[...] + b_ref[...]


def _final(x1, parts, g2, mod_map, tm, ln2w, ln2b, alpha):
    n, d = x1.shape
    rm = g2.shape[1] if g2.shape[1] == 1 else tm
    vec = pl.BlockSpec((1, d), lambda i: (0, 0))
    return pl.pallas_call(
        functools.partial(_final_kernel, alpha=alpha),
        grid=(n // tm,),
        in_specs=[pl.BlockSpec((tm, d), lambda i: (i, 0)),
                  pl.BlockSpec((PEER_SECTIONS, tm, d), lambda i: (0, i, 0)),
                  pl.BlockSpec((None, rm, d), lambda i: mod_map(i) + (0,)), vec, vec],
        out_specs=pl.BlockSpec((tm, d), lambda i: (i, 0)),
        out_shape=jax.ShapeDtypeStruct((n, d), jnp.float32),
        compiler_params=_cparams(("arbitrary",)),
        name="final_ln",
    )(x1, parts, g2, ln2w, ln2b)


def _layer_weights(l, w_in, w_gla_up, b_gla, gla_norm_w, w_br_a, w_br_b, w_out, ln1_w, ln1_b, w_pq,
                   peer_k1, peer_k2, ln2_w, ln2_b):
    d = w_in.shape[1]
    bf = lambda a: a.astype(jnp.bfloat16)
    sizes = (W_A, W_A, W_A, W_BK, W_BK, W_BV, W_BV, GATE_RANK, d, d)
    offs = np.concatenate([[0], np.cumsum(sizes)])
    col = lambda i, j=None: w_in[l][:, offs[i]:offs[(i if j is None else j) + 1]]
    pad_rank = GLR_PAD - GATE_RANK
    return {
        "wa": bf(col(0, 2)), "wqb": bf(col(3)), "wkb": bf(col(4)), "wvb": bf(col(5)), "wrb": bf(col(6)),
        "wglr": bf(jnp.pad(col(7), ((0, 0), (0, pad_rank)))), "wga": bf(col(8)), "wgb": bf(col(9)),
        "wup": bf(jnp.pad(w_gla_up[l], ((0, pad_rank), (0, 0)))), "bup": b_gla[l].reshape(1, -1),
        "gnw": gla_norm_w[l].reshape(1, -1), "wbra": bf(w_br_a[l]), "wbrb": bf(w_br_b[l]), "wout": bf(w_out[l]),
        "ln1w": ln1_w[l].reshape(1, -1), "ln1b": ln1_b[l].reshape(1, -1), "wpq": bf(w_pq[l]),
        "k1": bf(peer_k1[l]), "k2": bf(peer_k2[l]),
        "ln2w": ln2_w[l].reshape(1, -1), "ln2b": ln2_b[l].reshape(1, -1),
    }


def _ffn_and_norm(x1, h2, s1t, s2t, g2, mod_map, tm, wts, u_rows, v_rows, alpha):
    e, gw = _topk(s1t, s2t)
    parts = _peer(h2, e, gw, u_rows, v_rows)
    return _final(x1, parts, g2, mod_map, tm, wts["ln2w"], wts["ln2b"], alpha)


def kernel(x_prompt, x_sample, c_prompt, c_sample, cache_kv_w128, cache_kv_w512, cache_kv_w2048, state_gla, w_ada, b_ada, w_in, w_gla_up, b_gla, gla_norm_w, w_br_a, w_br_b, w_out, ln1_w, ln1_b, w_pq, peer_k1, peer_k2, peer_u, peer_v, ln2_w, ln2_b):
    depth = w_ada.shape[0]
    b, t, d = x_prompt.shape
    db, ds, _ = x_sample.shape
    assert ds == 1, "the single-token kernels take one new token per sequence"
    alpha = (2 * depth) ** 0.25
    tm_p = 256
    tm_s = db
    yp = x_prompt.reshape(b * t, d)
    ys = x_sample.reshape(db, d)
    caches = (cache_kv_w128, cache_kv_w512, cache_kv_w2048)
    kv_p = [[] for _ in DIL_CONFIGS]
    kv_s = [[] for _ in DIL_CONFIGS]
    gla_p, gla_s = [], []
    nc = b + db
    nc_pad = -(-nc // SUBLANES) * SUBLANES
    c_all = jnp.pad(jnp.concatenate([c_prompt, c_sample], axis=0), ((0, nc_pad - nc), (0, 0)))
    map_p = lambda i: (i // (t // tm_p), 0)
    map_s = lambda i: (0, i)
    for l in range(depth):
        wts = _layer_weights(l, w_in, w_gla_up, b_gla, gla_norm_w, w_br_a, w_br_b, w_out, ln1_w, ln1_b, w_pq,
                             peer_k1, peer_k2, ln2_w, ln2_b)
        u_rows = peer_u[l].reshape(-1, LANES)
        v_rows = peer_v[l].reshape(-1, LANES)
        mod = _adaln(c_all, w_ada[l], b_ada[l])
        mods_p = [m.reshape(b, 1, d) for m in jnp.split(mod[:b], 6, axis=-1)]
        mods_s = [m.reshape(1, db, d) for m in jnp.split(mod[b:b + db], 6, axis=-1)]

        a, qb, kb, vb, rb, gd, ga, gb = _inproj(yp, mods_p[0], mods_p[1], map_p, tm_p, wts)
        a3 = a.reshape(b, t, 3 * W_A)
        o_g, l_g = [], []
        for gi, (window, dil) in enumerate(DIL_CONFIGS):
            o, lse = _dil_prompt(a3, gi, dil)
            o_g.append(o.reshape(b * t, W_G))
            l_g.append(lse.reshape(b * t, W_G))
            keep = min(window, t)
            k_last = a3[:, t - keep:, W_A + gi * W_G:W_A + (gi + 1) * W_G]
            v_last = a3[:, t - keep:, 2 * W_A + gi * W_G:2 * W_A + (gi + 1) * W_G]
            kv_p[gi].append(jnp.stack([k_last, v_last], axis=2).reshape(b, keep, 2, HEADS_PER_GROUP, HEAD_DIM_A))
        og, s_fin = _gla_prompt(qb.reshape(b, t, W_BK), kb.reshape(b, t, W_BK), vb.reshape(b, t, W_BV),
                                gd.reshape(b, t, W_BK))
        gla_p.append(s_fin)
        x1, h2, s1t, s2t = _merge(o_g, l_g, og.reshape(b * t, W_BV), rb, ga, gb, yp, mods_p[2], mods_p[3],
                                  mods_p[4], map_p, tm_p, wts, alpha)
        yp = _ffn_and_norm(x1, h2, s1t, s2t, mods_p[5], map_p, tm_p, wts, u_rows, v_rows, alpha)

        a, qb, kb, vb, rb, gd, ga, gb = _inproj(ys, mods_s[0], mods_s[1], map_s, tm_s, wts)
        layer_caches = [c[l] for c in caches]
        o, lse, new_caches = _dil_sample(a, layer_caches)
        for gi in range(N_GROUPS):
            kv_s[gi].append(new_caches[gi])
        og, s_new = _gla_sample(qb, kb, vb, gd, state_gla[l])
        gla_s.append(s_new)
        o_g = [o[:, gi * W_G:(gi + 1) * W_G] for gi in range(N_GROUPS)]
        l_g = [lse[:, gi * W_G:(gi + 1) * W_G] for gi in range(N_GROUPS)]
        x1, h2, s1t, s2t = _merge(o_g, l_g, og, rb, ga, gb, ys, mods_s[2], mods_s[3], mods_s[4], map_s, tm_s,
                                  wts, alpha)
        ys = _ffn_and_norm(x1, h2, s1t, s2t, mods_s[5], map_s, tm_s, wts, u_rows, v_rows, alpha)

    return (yp.reshape(b, t, d), ys.reshape(db, ds, d),
            jnp.stack(kv_p[0]), jnp.stack(kv_p[1]), jnp.stack(kv_p[2]), jnp.stack(gla_p),
            jnp.stack(kv_s[0]), jnp.stack(kv_s[1]), jnp.stack(kv_s[2]), jnp.stack(gla_s))
```

```python
import functools

import numpy as np
import jax
import jax.numpy as jnp
from jax import lax
from jax.experimental import pallas as pl
from jax.experimental.pallas import tpu as pltpu

DIL_CONFIGS = ((128, 1), (512, 4), (2048, 16))
N_GROUPS = 3
HEADS_PER_GROUP = 4
HEAD_DIM_A = 64
W_G = HEADS_PER_GROUP * HEAD_DIM_A
W_A = N_GROUPS * W_G
DIL_BLOCK = 128
N_HEADS_B = 4
HEAD_K_B = 128
HEAD_V_B = 256
W_BK = N_HEADS_B * HEAD_K_B
W_BV = N_HEADS_B * HEAD_V_B
GATE_RANK = 16
GATE_TEMP = 16.0
GLR_PAD = 128
GLA_CHUNK = 64
GLA_SUB = 16
PEER_HEADS = 8
PEER_KEYS = 128
PEER_TOPK = 16
TOPK_SHIFT = PEER_TOPK.bit_length() - 1
PEER_PAIRS = PEER_HEADS * PEER_TOPK
LN_EPS = 1e-5

LANES = 128
SUBLANES = 8
VREG_ELEMS = LANES * SUBLANES
VMEM_LIMIT = 56 * 1024 * 1024

PEER_SECTIONS = 2
PEER_GROUP = SUBLANES
PEER_SLOTS = PEER_PAIRS + PEER_SECTIONS * PEER_GROUP
GROUP_SHIFT = PEER_GROUP.bit_length() - 1
PEER_GPT = PEER_SLOTS // PEER_GROUP
PEER_SPARE_G = PEER_GPT - 1
PEER_GSTRIDE = 32
PEER_TB = LANES
PEER_QCAP = PEER_TB * (PEER_PAIRS // PEER_GROUP)
PEER_UNROLL = 16
PEER_UNROLL_V = 8

_HI = lax.Precision.HIGHEST
_NEG = float("-inf")


def _cparams(sem, vmem=VMEM_LIMIT):
    return pltpu.CompilerParams(dimension_semantics=sem, vmem_limit_bytes=vmem)


def _ln(x):
    mu = jnp.mean(x, axis=-1, keepdims=True)
    xc = x - mu
    var = jnp.mean(xc * xc, axis=-1, keepdims=True)
    return xc * lax.rsqrt(var + LN_EPS)


def _bdot(a, b):
    return jnp.dot(a.astype(jnp.bfloat16), b.astype(jnp.bfloat16), preferred_element_type=jnp.float32)


def _bdot_nt(a, b):
    return lax.dot_general(a.astype(jnp.bfloat16), b.astype(jnp.bfloat16), (((1,), (1,)), ((), ())),
                           preferred_element_type=jnp.float32)


def _alibi_slope(head):
    return float(np.exp2(np.float32(-8.0 * (head + 1) / (N_GROUPS * HEADS_PER_GROUP))))


def _ada_kernel(c_ref, w_ref, b_ref, o_ref):
    c = c_ref[...]
    o_ref[...] = _bdot(c * jax.nn.sigmoid(c), w_ref[...]) + b_ref[...]


def _adaln(c, w_ada, b_ada):
    bc, d = c.shape
    ncol = w_ada.shape[1] // d
    return pl.pallas_call(
        _ada_kernel,
        grid=(ncol,),
        in_specs=[pl.BlockSpec((bc, d), lambda j: (0, 0)),
                  pl.BlockSpec((d, d), lambda j: (0, j)),
                  pl.BlockSpec((1, d), lambda j: (0, j))],
        out_specs=pl.BlockSpec((bc, d), lambda j: (0, j)),
        out_shape=jax.ShapeDtypeStruct((bc, ncol * d), jnp.float32),
        compiler_params=_cparams(("arbitrary",)),
        name="adaln",
    )(c, w_ada, b_ada.reshape(1, -1))


def _inproj_kernel(x_ref, sh_ref, sc_ref, wa_ref, wqb_ref, wkb_ref, wvb_ref, wrb_ref, wglr_ref, wga_ref, wgb_ref,
                   wup_ref, bup_ref, a_ref, qb_ref, kb_ref, vb_ref, rb_ref, gd_ref, ga_ref, gb_ref):
    h = (_ln(x_ref[...]) * (1.0 + sc_ref[...]) + sh_ref[...]).astype(jnp.bfloat16)

    def proj(w_ref):
        return jnp.dot(h, w_ref[...], preferred_element_type=jnp.float32)

    a_ref[...] = proj(wa_ref)
    qb_ref[...] = proj(wqb_ref) * (HEAD_K_B ** -0.5)
    kb_ref[...] = proj(wkb_ref)
    vb_ref[...] = proj(wvb_ref)
    rb_ref[...] = proj(wrb_ref)
    ga_ref[...] = proj(wga_ref)
    gb_ref[...] = proj(wgb_ref)
    glr = proj(wglr_ref)
    gate = _bdot(glr, wup_ref[...]) + bup_ref[...]
    gd_ref[...] = jax.nn.log_sigmoid(gate) * (1.0 / GATE_TEMP)


def _inproj(x2d, shift, scale, mod_map, tm, wts):
    n, d = x2d.shape
    rm = shift.shape[1] if shift.shape[1] == 1 else tm
    mod_spec = pl.BlockSpec((None, rm, d), lambda i: mod_map(i) + (0,))
    row = lambda w: pl.BlockSpec((tm, w), lambda i: (i, 0))
    const = lambda a: pl.BlockSpec(a.shape, lambda i: (0,) * a.ndim, pipeline_mode=pl.Buffered(1))
    names = ("wa", "wqb", "wkb", "wvb", "wrb", "wglr", "wga", "wgb", "wup", "bup")
    widths = (3 * W_A, W_BK, W_BK, W_BV, W_BV, W_BK, d, d)
    return pl.pallas_call(
        _inproj_kernel,
        grid=(n // tm,),
        in_specs=[row(d), mod_spec, mod_spec] + [const(wts[k]) for k in names],
        out_specs=[row(w) for w in widths],
        out_shape=[jax.ShapeDtypeStruct((n, w), jnp.float32) for w in widths],
        compiler_params=_cparams(("arbitrary",)),
        name="inproj",
    )(x2d, shift, scale, *[wts[k] for k in names])


def _dil_prompt_kernel(q_ref, kc_ref, kp_ref, vc_ref, vp_ref, o_ref, l_ref, o_scr, l_scr, *, group, dil):
    has_prev = pl.program_id(2) > 0
    half = pl.program_id(1)
    heads = LANES // HEAD_DIM_A
    qi = lax.broadcasted_iota(jnp.int32, (DIL_BLOCK, DIL_BLOCK), 0)
    ki = lax.broadcasted_iota(jnp.int32, (DIL_BLOCK, DIL_BLOCK), 1)
    valid_p = jnp.logical_and(ki >= qi, has_prev)
    valid_c = ki <= qi
    dist_p = ((qi + DIL_BLOCK - ki) * dil).astype(jnp.float32)
    dist_c = ((qi - ki) * dil).astype(jnp.float32)

    def phase(r):
        rows = pl.ds(r, DIL_BLOCK, stride=dil) if dil > 1 else pl.ds(0, DIL_BLOCK)
        q_all, kc, kp, vc, vp = (ref[rows, :] for ref in (q_ref, kc_ref, kp_ref, vc_ref, vp_ref))
        outs, lses = [], []
        for hh in range(heads):
            slope = jnp.where(half == 0, _alibi_slope(group * HEADS_PER_GROUP + hh),
                              _alibi_slope(group * HEADS_PER_GROUP + heads + hh))
            sl = slice(hh * HEAD_DIM_A, (hh + 1) * HEAD_DIM_A)
            q = q_all[:, sl]
            sp = _bdot_nt(q, kp[:, sl]) * (HEAD_DIM_A ** -0.5) - slope * dist_p
            sc = _bdot_nt(q, kc[:, sl]) * (HEAD_DIM_A ** -0.5) - slope * dist_c
            sp = jnp.where(valid_p, sp, _NEG)
            sc = jnp.where(valid_c, sc, _NEG)
            m = jnp.maximum(jnp.max(sp, axis=-1, keepdims=True), jnp.max(sc, axis=-1, keepdims=True))
            pp = jnp.exp(sp - m)
            pc = jnp.exp(sc - m)
            z = jnp.sum(pp, axis=-1, keepdims=True) + jnp.sum(pc, axis=-1, keepdims=True)
            outs.append((_bdot(pp, vp[:, sl]) + _bdot(pc, vc[:, sl])) / z)
            lses.append(jnp.broadcast_to(m + jnp.log(z), (DIL_BLOCK, HEAD_DIM_A)))
        o_scr[r] = jnp.concatenate(outs, axis=-1)
        l_scr[r] = jnp.concatenate(lses, axis=-1)

    per_trip = 2 if dil > 1 else 1

    def trip(i, carry):
        for u in range(per_trip):
            phase(i * per_trip + u)
        return carry

    lax.fori_loop(0, dil // per_trip, trip, 0)
    for r in range(dil):
        rows = pl.ds(r, DIL_BLOCK, stride=dil) if dil > 1 else pl.ds(0, DIL_BLOCK)
        o_ref[rows, :] = o_scr[r]
        l_ref[rows, :] = l_scr[r]


def _dil_prompt(qkv, group, dil):
    b, t, wq = qkv.shape
    span = dil * DIL_BLOCK
    assert t % span == 0
    halves = W_G // LANES
    assert halves == 2 and HEADS_PER_GROUP * HEAD_DIM_A == W_G
    qcol, kcol, vcol = group, N_GROUPS + group, 2 * N_GROUPS + group
    blk = (None, span, LANES)
    cur = lambda col: pl.BlockSpec(blk, lambda bi, h, n: (bi, n, halves * col + h))
    prev = lambda col: pl.BlockSpec(blk, lambda bi, h, n: (bi, jnp.maximum(n - 1, 0), halves * col + h))
    ospec = pl.BlockSpec(blk, lambda bi, h, n: (bi, n, h))
    scr = pltpu.VMEM((dil, DIL_BLOCK, LANES), jnp.float32)
    return pl.pallas_call(
        functools.partial(_dil_prompt_kernel, group=group, dil=dil),
        grid=(b, halves, t // span),
        in_specs=[cur(qcol), cur(kcol), prev(kcol), cur(vcol), prev(vcol)],
        out_specs=[ospec, ospec],
        out_shape=[jax.ShapeDtypeStruct((b, t, W_G), jnp.float32)] * 2,
        scratch_shapes=[scr, scr],
        compiler_params=_cparams(("arbitrary", "arbitrary", "arbitrary")),
        name=f"dil_prompt_g{group}",
    )(qkv, qkv, qkv, qkv, qkv)


def _gla_prompt_kernel(q_ref, k_ref, v_ref, g_ref, o_ref, sfin_ref, s_scr):
    c = pl.program_id(1)
    nchunk = pl.num_programs(1)
    C = GLA_CHUNK

    @pl.when(c == 0)
    def _():
        s_scr[...] = jnp.zeros_like(s_scr)

    ri = lax.broadcasted_iota(jnp.int32, (C, C), 0)
    ci = lax.broadcasted_iota(jnp.int32, (C, C), 1)
    tri = (ri >= ci).astype(jnp.float32)
    bcum = jnp.dot(tri, g_ref[...], precision=_HI, preferred_element_type=jnp.float32)
    row16 = lax.broadcasted_iota(jnp.int32, (GLA_SUB, HEAD_K_B), 0)
    lane16 = lax.broadcasted_iota(jnp.int32, (GLA_SUB, LANES), 1)
    rowc = lax.broadcasted_iota(jnp.int32, (C, HEAD_K_B), 0)
    nsub = C // GLA_SUB
    outs = []
    for h in range(N_HEADS_B):
        ks = slice(h * HEAD_K_B, (h + 1) * HEAD_K_B)
        vs = slice(h * HEAD_V_B, (h + 1) * HEAD_V_B)
        bh = bcum[:, ks]
        qh = q_ref[:, ks]
        kh = k_ref[:, ks]
        vh = v_ref[:, vs]
        sh = s_scr[h]
        o_inter = _bdot(qh * jnp.exp(bh), sh)
        arows = []
        for i in range(nsub):
            r0 = i * GLA_SUB
            bi = bh[r0:r0 + GLA_SUB]
            qi_ = qh[r0:r0 + GLA_SUB]
            ki_ = kh[r0:r0 + GLA_SUB]
            a = jnp.zeros((GLA_SUB, LANES), jnp.float32)
            for s in range(GLA_SUB):
                e = jnp.exp(jnp.where(row16 >= s, bi - bi[s:s + 1], _NEG))
                col = jnp.sum(qi_ * (ki_[s:s + 1] * e), axis=-1, keepdims=True)
                a = jnp.where(lane16 == r0 + s, col, a)
            a = a[:, :C]
            if i > 0:
                b0 = bi[0:1]
                qt = qi_ * jnp.exp(bi - b0)
                kt = kh * jnp.exp(jnp.where(rowc < r0, b0 - bh, _NEG))
                a = a + _bdot_nt(qt, kt)
            arows.append(a)
        amat = jnp.concatenate(arows, axis=0)
        outs.append(o_inter + _bdot(amat, vh))
        bl = bh[C - 1:C]
        kt = kh * jnp.exp(bl - bh)
        dcol = jnp.transpose(jnp.broadcast_to(jnp.exp(bl), (SUBLANES, HEAD_K_B)))[:, 0:1]
        upd = lax.dot_general(kt.astype(jnp.bfloat16), vh.astype(jnp.bfloat16), (((0,), (0,)), ((), ())),
                              preferred_element_type=jnp.float32)
        s_scr[h] = dcol * sh + upd
    o_ref[...] = jnp.concatenate(outs, axis=-1)

    @pl.when(c == nchunk - 1)
    def _():
        sfin_ref[...] = s_scr[...]


def _gla_prompt(qb, kb, vb, gd):
    b, t, _ = qb.shape
    assert t % GLA_CHUNK == 0
    spec = lambda w: pl.BlockSpec((None, GLA_CHUNK, w), lambda bi, c: (bi, c, 0))
    sshape = (N_HEADS_B, HEAD_K_B, HEAD_V_B)
    return pl.pallas_call(
        _gla_prompt_kernel,
        grid=(b, t // GLA_CHUNK),
        in_specs=[spec(W_BK), spec(W_BK), spec(W_BV), spec(W_BK)],
        out_specs=[spec(W_BV), pl.BlockSpec((None,) + sshape, lambda bi, c: (bi, 0, 0, 0))],
        out_shape=[jax.ShapeDtypeStruct((b, t, W_BV), jnp.float32),
                   jax.ShapeDtypeStruct((b,) + sshape, jnp.float32)],
        scratch_shapes=[pltpu.VMEM(sshape, jnp.float32)],
        compiler_params=_cparams(("arbitrary", "arbitrary")),
        name="gla_prompt",
    )(qb, kb, vb, gd)


SAMPLE_SEQS = 8


def _to_col(row):
    return jnp.transpose(jnp.broadcast_to(row, (SUBLANES, row.shape[1])))[:, 0:1]


def _to_row(col):
    return jnp.transpose(jnp.broadcast_to(col, (col.shape[0], SUBLANES)))[0:1, :]


def _dil_sample_kernel(a_ref, c0_ref, c1_ref, c2_ref, o_ref, l_ref, n0_ref, n1_ref, n2_ref):
    col = _to_col(a_ref[0])
    scale = HEAD_DIM_A ** -0.5
    kv_w = 2 * W_G
    for g, (cref, nref, (window, dil)) in enumerate(zip((c0_ref, c1_ref, c2_ref), (n0_ref, n1_ref, n2_ref),
                                                        DIL_CONFIGS)):
        q = col[g * W_G:(g + 1) * W_G]
        knew = col[W_A + g * W_G:W_A + (g + 1) * W_G]
        vnew = col[2 * W_A + g * W_G:2 * W_A + (g + 1) * W_G]
        x = cref[0]
        lane = lax.broadcasted_iota(jnp.int32, (1, window), 1)
        on_stride = (lane & (dil - 1)) == 0
        dist = (window - lane).astype(jnp.float32)
        o_cols, l_cols = [], []
        for hh in range(HEADS_PER_GROUP):
            rows = slice(hh * HEAD_DIM_A, (hh + 1) * HEAD_DIM_A)
            qh = q[rows]
            s = jnp.sum(x[rows, :] * qh, axis=0, keepdims=True) * scale
            s = s - _alibi_slope(g * HEADS_PER_GROUP + hh) * dist
            s = jnp.where(on_stride, s, _NEG)
            ss = jnp.sum(knew[rows] * qh, axis=0, keepdims=True) * scale
            m = jnp.maximum(jnp.max(s, axis=1, keepdims=True), ss)
            p = jnp.exp(s - m)
            ps = jnp.exp(ss - m)
            z = jnp.sum(p, axis=1, keepdims=True) + ps
            vrows = slice(W_G + hh * HEAD_DIM_A, W_G + (hh + 1) * HEAD_DIM_A)
            o_cols.append((jnp.sum(x[vrows, :] * p, axis=1, keepdims=True) + ps * vnew[rows]) / z)
            l_cols.append(jnp.broadcast_to(m + jnp.log(z), (HEAD_DIM_A, 1)))
        o_ref[0, :, g * W_G:(g + 1) * W_G] = _to_row(jnp.concatenate(o_cols, axis=0))
        l_ref[0, :, g * W_G:(g + 1) * W_G] = _to_row(jnp.concatenate(l_cols, axis=0))
        shifted = pltpu.roll(x, window - 1, axis=1)
        nref[0] = jnp.where(lane == window - 1, jnp.concatenate([knew, vnew], axis=0), shifted)


def _dil_sample(qkv, caches):
    db = qkv.shape[0]
    kv_w = 2 * W_G
    views, cspecs, oshapes = [], [], []
    for cache, (window, dil) in zip(caches, DIL_CONFIGS):
        assert cache.shape[1] == window, "window caches shorter than the window are not supported"
        assert dil & (dil - 1) == 0
        views.append(jnp.transpose(cache, (0, 2, 3, 4, 1)).reshape(db, kv_w, window))
        cspecs.append(pl.BlockSpec((1, kv_w, window), lambda i: (i, 0, 0)))
        oshapes.append(jax.ShapeDtypeStruct((db, kv_w, window), jnp.float32))
    rspec = lambda w: pl.BlockSpec((1, 1, w), lambda i: (i, 0, 0))
    o, lse, *new = pl.pallas_call(
        _dil_sample_kernel,
        grid=(db,),
        in_specs=[rspec(3 * W_A)] + cspecs,
        out_specs=[rspec(W_A), rspec(W_A)] + cspecs,
        out_shape=[jax.ShapeDtypeStruct((db, 1, W_A), jnp.float32)] * 2 + oshapes,
        compiler_params=_cparams(("arbitrary",)),
        name="dil_sample",
    )(qkv.reshape(db, 1, 3 * W_A), *views)
    new = [jnp.transpose(n.reshape(db, 2, HEADS_PER_GROUP, HEAD_DIM_A, n.shape[-1]), (0, 4, 1, 2, 3)) for n in new]
    return o.reshape(db, W_A), lse.reshape(db, W_A), new


def _gla_sample_kernel(q_ref, k_ref, v_ref, g_ref, s0_ref, o_ref, s_ref):
    for h in range(N_HEADS_B):
        ks = slice(h * HEAD_K_B, (h + 1) * HEAD_K_B)
        vs = slice(h * HEAD_V_B, (h + 1) * HEAD_V_B)
        qT = jnp.transpose(q_ref[:, ks])
        kT = jnp.transpose(k_ref[:, ks])
        aT = jnp.transpose(jnp.exp(g_ref[:, ks]))
        for j in range(SAMPLE_SEQS):
            s_new = aT[:, j:j + 1] * s0_ref[j, h] + kT[:, j:j + 1] * v_ref[j:j + 1, vs]
            s_ref[j, h] = s_new
            o_ref[j:j + 1, vs] = jnp.sum(qT[:, j:j + 1] * s_new, axis=0, keepdims=True)


def _gla_sample(qb, kb, vb, gd, s0):
    db = qb.shape[0]
    row = lambda w: pl.BlockSpec((SAMPLE_SEQS, w), lambda i: (i, 0))
    sspec = pl.BlockSpec((SAMPLE_SEQS, N_HEADS_B, HEAD_K_B, HEAD_V_B), lambda i: (i, 0, 0, 0))
    return pl.pallas_call(
        _gla_sample_kernel,
        grid=(db // SAMPLE_SEQS,),
        in_specs=[row(W_BK), row(W_BK), row(W_BV), row(W_BK), sspec],
        out_specs=[row(W_BV), sspec],
        out_shape=[jax.ShapeDtypeStruct((db, W_BV), jnp.float32), jax.ShapeDtypeStruct(s0.shape, jnp.float32)],
        compiler_params=_cparams(("arbitrary",)),
        name="gla_sample",
    )(qb, kb, vb, gd, s0)


def _merge_kernel(o0_ref, o1_ref, o2_ref, l0_ref, l1_ref, l2_ref, og_ref, rb_ref, ga_ref, gb_ref, x_ref,
                  g1_ref, sh2_ref, sc2_ref, gnw_ref, wbra_ref, wbrb_ref, wout_ref, ln1w_ref, ln1b_ref,
                  wpq_ref, k1_ref, k2_ref, x1_ref, h2_ref, s1_ref, s2_ref, *, alpha):
    l0, l1, l2 = l0_ref[...], l1_ref[...], l2_ref[...]
    m = jnp.maximum(jnp.maximum(l0, l1), l2)
    e0, e1, e2 = jnp.exp(l0 - m), jnp.exp(l1 - m), jnp.exp(l2 - m)
    oa = (e0 * o0_ref[...] + e1 * o1_ref[...] + e2 * o2_ref[...]) / (e0 + e1 + e2)
    og = og_ref[...]
    parts = []
    for h in range(N_HEADS_B):
        oh = og[:, h * HEAD_V_B:(h + 1) * HEAD_V_B]
        parts.append(oh * lax.rsqrt(jnp.mean(oh * oh, axis=-1, keepdims=True) + LN_EPS))
    rb = rb_ref[...]
    ob = jnp.concatenate(parts, axis=-1) * gnw_ref[...] * (rb * jax.nn.sigmoid(rb))
    merged = (jax.nn.sigmoid(ga_ref[...]) * _bdot(oa, wbra_ref[...])
              + jax.nn.sigmoid(gb_ref[...]) * _bdot(ob, wbrb_ref[...]))
    mix = _bdot(merged, wout_ref[...])
    x1 = _ln(alpha * x_ref[...] + g1_ref[...] * mix) * ln1w_ref[...] + ln1b_ref[...]
    x1_ref[...] = x1
    h2 = _ln(x1) * (1.0 + sc2_ref[...]) + sh2_ref[...]
    h2_ref[...] = h2
    qv = _bdot(h2, wpq_ref[...]).astype(jnp.bfloat16)
    half = PEER_KEYS
    for h in range(PEER_HEADS):
        base = h * 2 * half
        s1_ref[h] = lax.dot_general(k1_ref[h], qv[:, base:base + half], (((1,), (1,)), ((), ())),
                                    preferred_element_type=jnp.float32)
        s2_ref[h] = lax.dot_general(k2_ref[h], qv[:, base + half:base + 2 * half], (((1,), (1,)), ((), ())),
                                    preferred_element_type=jnp.float32)


def _merge(o_g, l_g, og, rb, ga, gb, x2d, g1, sh2, sc2, mod_map, tm, wts, alpha):
    n, d = x2d.shape
    rm = g1.shape[1] if g1.shape[1] == 1 else tm
    mod_spec = pl.BlockSpec((None, rm, d), lambda i: mod_map(i) + (0,))
    row = lambda w: pl.BlockSpec((tm, w), lambda i: (i, 0))
    const = lambda a: pl.BlockSpec(a.shape, lambda i: (0,) * a.ndim, pipeline_mode=pl.Buffered(1))
    names = ("gnw", "wbra", "wbrb", "wout", "ln1w", "ln1b", "wpq", "k1", "k2")
    sspec = pl.BlockSpec((PEER_HEADS, PEER_KEYS, tm), lambda i: (0, 0, i))
    return pl.pallas_call(
        functools.partial(_merge_kernel, alpha=alpha),
        grid=(n // tm,),
        in_specs=[row(W_G)] * 6 + [row(W_BV), row(W_BV), row(d), row(d), row(d), mod_spec, mod_spec, mod_spec]
                 + [const(wts[k]) for k in names],
        out_specs=[row(d), row(d), sspec, sspec],
        out_shape=[jax.ShapeDtypeStruct((n, d), jnp.float32)] * 2
                  + [jax.ShapeDtypeStruct((PEER_HEADS, PEER_KEYS, n), jnp.float32)] * 2,
        compiler_params=_cparams(("arbitrary",)),
        name="merge",
    )(*o_g, *l_g, og, rb, ga, gb, x2d, g1, sh2, sc2, *[wts[k] for k in names])


_CAND_IDS = tuple(i * PEER_TOPK + j for i in range(PEER_TOPK) for j in range(PEER_TOPK)
                  if (i + 1) * (j + 1) <= PEER_TOPK)
_NO_ID = PEER_KEYS * PEER_KEYS
TOPK_CHUNK = 16


def _topk_kernel(s1_ref, s2_ref, e_ref, w_ref, sa, sb, va, vb, ia, ib, cand, sc_scr, ci_scr):
    sub = s1_ref.shape[2]
    shape = (sub, LANES)
    sa[...] = s1_ref[0]
    sb[...] = s2_ref[0]

    def tree(op, xs):
        xs = list(xs)
        while len(xs) > 1:
            xs = [op(xs[k], xs[k + 1]) for k in range(0, len(xs) - 1, 2)] + (xs[-1:] if len(xs) % 2 else [])
        return xs[0]

    def extract(s_scr, ids, r, v_out, i_out):
        n = len(ids)
        chunks = [range(c, min(c + TOPK_CHUNK, n)) for c in range(0, n, TOPK_CHUNK)]
        m = tree(jnp.maximum, [tree(jnp.maximum, [s_scr[k] for k in ch]) for ch in chunks])
        idx = tree(jnp.minimum, [tree(jnp.minimum, [jnp.where(s_scr[k] == m, ids[k], _NO_ID) for k in ch])
                                 for ch in chunks])
        for k in range(n):
            s_scr[k] = jnp.where(idx == ids[k], _NEG, s_scr[k])
        v_out[r] = m
        i_out[r] = idx

    def stage1(r, c):
        extract(sa, range(PEER_KEYS), r, va, ia)
        extract(sb, range(PEER_KEYS), r, vb, ib)
        return c

    lax.fori_loop(0, PEER_TOPK, stage1, 0)
    for k, ci in enumerate(_CAND_IDS):
        cand[k] = va[ci // PEER_TOPK] + vb[ci % PEER_TOPK]

    def stage2(r, c):
        extract(cand, _CAND_IDS, r, sc_scr, ci_scr)
        return c

    lax.fori_loop(0, PEER_TOPK, stage2, 0)
    top = sc_scr[0]
    z = jnp.zeros(shape, jnp.float32)
    for r in range(PEER_TOPK):
        z = z + jnp.exp(sc_scr[r] - top)
    for r in range(PEER_TOPK):
        w_ref[0, 0, r] = jnp.exp(sc_scr[r] - top) / z
        ci = ci_scr[r]
        hi = lax.shift_right_logical(ci, TOPK_SHIFT)
        lo = ci & (PEER_TOPK - 1)
        e1 = jnp.zeros(shape, jnp.int32)
        e2 = jnp.zeros(shape, jnp.int32)
        for i in range(PEER_TOPK):
            e1 = jnp.where(hi == i, ia[i], e1)
            e2 = jnp.where(lo == i, ib[i], e2)
        e_ref[0, 0, r] = e1 * PEER_KEYS + e2


def _topk(s1t, s2t):
    nh, nk, n = s1t.shape
    sub = min(SUBLANES, n // LANES)
    nchunk = n // (sub * LANES)
    v1 = s1t.reshape(nh, nk, n // LANES, LANES)
    v2 = s2t.reshape(nh, nk, n // LANES, LANES)
    ispec = pl.BlockSpec((1, nk, sub, LANES), lambda c, h: (h, 0, c, 0))
    ospec = pl.BlockSpec((1, 1, PEER_TOPK, sub, LANES), lambda c, h: (c, h, 0, 0, 0))
    oshape = (nchunk, nh, PEER_TOPK, sub, LANES)
    key = lambda k, dt: pltpu.VMEM((k, sub, LANES), dt)
    e, w = pl.pallas_call(
        _topk_kernel,
        grid=(nchunk, nh),
        in_specs=[ispec, ispec],
        out_specs=[ospec, ospec],
        out_shape=[jax.ShapeDtypeStruct(oshape, jnp.int32), jax.ShapeDtypeStruct(oshape, jnp.float32)],
        scratch_shapes=[key(nk, jnp.float32), key(nk, jnp.float32),
                        key(PEER_TOPK, jnp.float32), key(PEER_TOPK, jnp.float32),
                        key(PEER_TOPK, jnp.int32), key(PEER_TOPK, jnp.int32),
                        key(len(_CAND_IDS), jnp.float32), key(PEER_TOPK, jnp.float32),
                        key(PEER_TOPK, jnp.int32)],
        compiler_params=_cparams(("arbitrary", "arbitrary")),
        name="peer_topk",
    )(v1, v2)
    return e, w


def _load_section(tab_hbm, tab_vmem, sem):
    rows = tab_vmem.shape[0]

    @pl.when(pl.program_id(1) == 0)
    def _():
        start = pl.multiple_of(pl.program_id(0) * rows, SUBLANES)
        cp = pltpu.make_async_copy(tab_hbm.at[pl.ds(start, rows)], tab_vmem, sem)
        cp.start()
        cp.wait()


def _grid_step():
    return pl.program_id(0) * pl.num_programs(1) + pl.program_id(1)


def _peer_u_kernel(nq_ref, glo_ref, ghi_ref, xrow_ref, gt_ref, *refs):
    idx_refs = refs[:PEER_GROUP]
    x_ref, tab_hbm, act_ref, tab_vmem, dbuf, xs, sem = refs[PEER_GROUP:]
    _load_section(tab_hbm, tab_vmem, sem)
    step = _grid_step()

    @pl.when(step == 0)
    def _():
        dbuf[...] = jnp.zeros_like(dbuf)

    for k in range(SUBLANES):
        xs[pl.ds(k, PEER_TB, stride=SUBLANES), :] = x_ref[:, k * LANES:(k + 1) * LANES]

    nq = nq_ref[step]
    sub = lax.broadcasted_iota(jnp.int32, (SUBLANES, LANES), 0)
    order = _fold_slot_order()

    def fold(a, b, sh):
        keep = (sub & sh) == 0
        u = jnp.where(keep, a, b)
        v = jnp.where(keep, b, a)
        if 2 * sh == SUBLANES:
            w = pltpu.roll(v, sh, axis=0)
        else:
            w = jnp.where(keep, pltpu.roll(v, SUBLANES - sh, axis=0), pltpu.roll(v, sh, axis=0))
        return u + w

    def body(i, c):
        for u in range(PEER_UNROLL):
            q = i * PEER_UNROLL + u
            gt = gt_ref[q]
            x = xs[pl.ds(pl.multiple_of(xrow_ref[q], SUBLANES), SUBLANES), :]
            ps = [tab_vmem[pl.ds(pl.multiple_of(idx_refs[order[j]][gt], SUBLANES), SUBLANES), :] * x
                  for j in range(PEER_GROUP)]
            while len(ps) > 1:
                sh = len(ps) // 2
                ps = [fold(ps[2 * k], ps[2 * k + 1], sh) for k in range(sh)]
            dbuf[pl.ds(pl.multiple_of(gt * PEER_GROUP, SUBLANES), SUBLANES), :] = ps[0]
        return c

    lax.fori_loop(0, nq // PEER_UNROLL, body, 0)
    rows_per_g = PEER_GROUP * PEER_TB
    glo, ghi = glo_ref[step], ghi_ref[step]

    def zfill(g, carry):
        act_ref[0, pl.ds(g, 1), :] = jnp.zeros((1, rows_per_g), jnp.float32)
        return carry

    def reduce(g, carry):
        sums = [jnp.sum(jnp.transpose(dbuf[pl.ds(g * rows_per_g + j, PEER_TB, stride=PEER_GROUP), :]),
                        axis=0, keepdims=True) for j in range(PEER_GROUP)]
        act_ref[0, pl.ds(g, 1), :] = jnp.concatenate(sums, axis=-1)
        return carry

    lax.fori_loop(0, glo, zfill, 0)
    lax.fori_loop(glo, ghi, reduce, 0)
    lax.fori_loop(ghi, PEER_GPT, zfill, 0)


def _fold_slot_order():
    pos = [[j] for j in range(PEER_GROUP)]
    sl = [0] * PEER_GROUP
    sh = PEER_GROUP // 2
    groups = pos
    while len(groups) > 1:
        nxt = []
        for i in range(len(groups) // 2):
            for j in groups[2 * i + 1]:
                sl[j] |= sh
            nxt.append(groups[2 * i] + groups[2 * i + 1])
        groups = nxt
        sh //= 2
    return sl


def _peer_v_kernel(nq_ref, glo_ref, ghi_ref, xrow_ref, gt_ref, wrow_ref, *refs):
    idx_refs = refs[:PEER_GROUP]
    w_ref, tab_hbm, out_ref, tab_vmem, wb, acc_scr, sem = refs[PEER_GROUP:]
    _load_section(tab_hbm, tab_vmem, sem)
    step = _grid_step()
    nq = nq_ref[step]
    rows_per_g = PEER_GROUP * PEER_TB

    @pl.when(step == 0)
    def _():
        wb[pl.ds(PEER_SPARE_G * rows_per_g, rows_per_g), :] = jnp.zeros((rows_per_g, LANES), jnp.float32)

    acc_scr[...] = jnp.zeros_like(acc_scr)

    def spread(g, carry):
        for j in range(PEER_GROUP):
            r = g * PEER_GROUP + j
            rep = jnp.broadcast_to(w_ref[0, pl.ds(r, 1), :], (LANES, LANES))
            wb[pl.ds(pl.multiple_of(r * PEER_TB, PEER_TB), PEER_TB), :] = jnp.transpose(rep)
        return carry

    lax.fori_loop(glo_ref[step], ghi_ref[step], spread, 0)
    trash = PEER_TB * SUBLANES

    def body(i, carry):
        cur, acc = carry
        for u in range(PEER_UNROLL_V):
            q = i * PEER_UNROLL_V + u
            gt = gt_ref[q]
            row0 = wrow_ref[q]
            terms = [tab_vmem[pl.ds(pl.multiple_of(idx_refs[j][gt], SUBLANES), SUBLANES), :]
                     * wb[pl.ds(row0 + j * PEER_TB, 1), :] for j in range(PEER_GROUP)]
            while len(terms) > 1:
                terms = [terms[2 * k] + terms[2 * k + 1] for k in range(len(terms) // 2)]
            row = xrow_ref[q]
            acc_scr[pl.ds(pl.multiple_of(cur, SUBLANES), SUBLANES), :] = acc
            acc = jnp.where(row == cur, acc + terms[0], terms[0])
            cur = row
        return cur, acc

    cur, acc = lax.fori_loop(0, nq // PEER_UNROLL_V, body,
                             (jnp.int32(trash), jnp.zeros((SUBLANES, LANES), jnp.float32)))
    acc_scr[pl.ds(pl.multiple_of(cur, SUBLANES), SUBLANES), :] = acc
    for k in range(SUBLANES):
        out_ref[0, :, k * LANES:(k + 1) * LANES] = acc_scr[pl.ds(k, PEER_TB, stride=SUBLANES), :]


def _gate_kernel(a_ref, gw_ref, w_ref):
    act = a_ref[...]
    w_ref[...] = gw_ref[...] * (0.5 * act * (1.0 + lax.erf(act * (2.0 ** -0.5))))


def _router_kernel(e_ref, w_ref, idx_ref, gw_ref, g0_ref, g01_ref, pos_scr, loc_scr, *, sec_experts):
    shape = e_ref.shape[3:]
    zero = jnp.zeros(shape, jnp.int32)
    r0, r1 = zero, zero
    for p in range(PEER_PAIRS):
        e = e_ref[0, p // PEER_TOPK, p % PEER_TOPK]
        upper = e >= sec_experts
        loc_scr[p] = (e & (sec_experts - 1)) * SUBLANES
        pos_scr[p] = jnp.where(upper, r1 + PEER_SLOTS, r0)
        r0 = r0 + jnp.where(upper, 0, 1)
        r1 = r1 + jnp.where(upper, 1, 0)
    g0 = lax.shift_right_logical(r0 + (PEER_GROUP - 1), GROUP_SHIFT)
    g1 = lax.shift_right_logical(r1 + (PEER_GROUP - 1), GROUP_SHIFT)
    g0_ref[0] = g0
    g01_ref[0] = g0 + g1
    rebase = g0 * PEER_GROUP - PEER_SLOTS
    for p in range(PEER_PAIRS):
        pos = pos_scr[p]
        pos_scr[p] = jnp.where(pos >= PEER_SLOTS, pos + rebase, pos)

    def place(g, carry):
        for j in range(PEER_GROUP):
            slot = g * PEER_GROUP + j
            iv = zero
            wv = jnp.zeros(shape, jnp.float32)
            for p in range(PEER_PAIRS):
                hit = pos_scr[p] == slot
                iv = jnp.where(hit, loc_scr[p], iv)
                wv = jnp.where(hit, w_ref[0, p // PEER_TOPK, p % PEER_TOPK], wv)
            idx_ref[j, 0, g] = iv
            gw_ref[0, slot] = wv
        return carry

    lax.fori_loop(0, PEER_GPT, place, 0)
    for j in range(PEER_GROUP):
        for g in range(PEER_GPT, PEER_GSTRIDE):
            idx_ref[j, 0, g] = zero


def _router(e, w, n_experts):
    nchunk, nh, k, sub, _ = e.shape
    sec_experts = n_experts // PEER_SECTIONS
    assert sec_experts & (sec_experts - 1) == 0 and PEER_SECTIONS == 2
    ispec = pl.BlockSpec((1, nh, k, sub, LANES), lambda c: (c, 0, 0, 0, 0))
    bspec = pl.BlockSpec((1, sub, LANES), lambda c: (c, 0, 0))
    scr = lambda dt: pltpu.VMEM((PEER_PAIRS, sub, LANES), dt)
    return pl.pallas_call(
        functools.partial(_router_kernel, sec_experts=sec_experts),
        grid=(nchunk,),
        in_specs=[ispec, ispec],
        out_specs=[pl.BlockSpec((PEER_GROUP, 1, PEER_GSTRIDE, sub, LANES), lambda c: (0, c, 0, 0, 0)),
                   pl.BlockSpec((1, PEER_SLOTS, sub, LANES), lambda c: (c, 0, 0, 0)), bspec, bspec],
        out_shape=[jax.ShapeDtypeStruct((PEER_GROUP, nchunk, PEER_GSTRIDE, sub, LANES), jnp.int32),
                   jax.ShapeDtypeStruct((nchunk, PEER_SLOTS, sub, LANES), jnp.float32),
                   jax.ShapeDtypeStruct((nchunk, sub, LANES), jnp.int32),
                   jax.ShapeDtypeStruct((nchunk, sub, LANES), jnp.int32)],
        scratch_shapes=[scr(jnp.int32), scr(jnp.int32)],
        compiler_params=_cparams(("arbitrary",)),
        name="peer_router",
    )(e, w)


def _peer_lists(g_lo, g_hi):
    nblk = g_lo.shape[0]
    cnt = g_hi - g_lo
    end = jnp.cumsum(cnt, axis=1)
    start = end - cnt
    q = jnp.arange(PEER_QCAP, dtype=jnp.int32)[None, :, None]
    inside = jnp.logical_and(start[:, None, :] <= q, q < end[:, None, :])
    tok = jnp.arange(PEER_TB, dtype=jnp.int32)[None, None, :]
    t = jnp.sum(jnp.where(inside, tok, 0), axis=2)
    g = q[:, :, 0] + jnp.sum(jnp.where(inside, (g_lo - start)[:, None, :], 0), axis=2)
    count = end[:, -1:]
    live = q[:, :, 0] < count
    xrow = jnp.where(live, t * SUBLANES, 0)
    xrow = jnp.where(live, xrow, jnp.max(xrow, axis=1, keepdims=True))
    gt = jnp.where(live, g * PEER_TB + t, PEER_SPARE_G * PEER_TB)
    nq = (count[:, 0] + PEER_UNROLL - 1) // PEER_UNROLL * PEER_UNROLL
    return nq, jnp.min(g_lo, axis=1), jnp.max(g_hi, axis=1), xrow, gt


def _peer(h2, e, gw, u_rows, v_rows):
    n, d = h2.shape
    assert d == VREG_ELEMS and n % PEER_TB == 0 and PEER_TB == LANES
    n_experts = u_rows.shape[0] // SUBLANES
    sec_rows = u_rows.shape[0] // PEER_SECTIONS
    nblk = n // PEER_TB
    nstep = PEER_SECTIONS * nblk
    idx5, gw4, g0, g01 = _router(e, gw, n_experts)
    idx_by_slot = jnp.transpose(idx5, (0, 1, 3, 2, 4)).reshape(PEER_GROUP, nblk * PEER_GSTRIDE * PEER_TB)
    idx_by_slot = [idx_by_slot[j] for j in range(PEER_GROUP)]
    rows_per_g = PEER_GROUP * PEER_TB
    gw_blk = jnp.transpose(gw4, (0, 2, 1, 3)).reshape(nblk, PEER_GPT, rows_per_g)
    g0 = g0.reshape(nblk, PEER_TB)
    g01 = g01.reshape(nblk, PEER_TB)
    bounds = (jnp.zeros_like(g0), g0, g01)
    lists = [_peer_lists(bounds[s], bounds[s + 1]) for s in range(PEER_SECTIONS)]
    nq, glo, ghi, xrow, gt = [jnp.stack(a).reshape(-1) for a in zip(*lists)]
    wrow = gt + (gt // PEER_TB) * ((PEER_GROUP - 1) * PEER_TB)

    npre = 3
    lst = pl.BlockSpec((PEER_QCAP,), lambda s, i, *_: (s * nblk + i,), memory_space=pltpu.SMEM)
    slots = [pl.BlockSpec((PEER_TB * PEER_GSTRIDE,), lambda s, i, *_: (i,), memory_space=pltpu.SMEM)
             ] * PEER_GROUP
    table = pl.BlockSpec(memory_space=pl.ANY)
    tab_scratch = pltpu.VMEM((sec_rows, LANES), jnp.float32)
    slot_rows = PEER_GPT * rows_per_g
    act = pl.pallas_call(
        _peer_u_kernel,
        grid_spec=pltpu.PrefetchScalarGridSpec(
            num_scalar_prefetch=npre,
            grid=(PEER_SECTIONS, nblk),
            in_specs=[lst, lst] + slots + [pl.BlockSpec((PEER_TB, d), lambda s, i, *_: (i, 0)), table],
            out_specs=pl.BlockSpec((1, PEER_GPT, rows_per_g), lambda s, i, *_: (s * nblk + i, 0, 0)),
            scratch_shapes=[tab_scratch, pltpu.VMEM((slot_rows, LANES), jnp.float32),
                            pltpu.VMEM((PEER_TB * SUBLANES, LANES), jnp.float32), pltpu.SemaphoreType.DMA]),
        out_shape=jax.ShapeDtypeStruct((nstep, PEER_GPT, rows_per_g), jnp.float32),
        compiler_params=_cparams(("arbitrary", "arbitrary")),
        name="peer_u",
    )(nq, glo, ghi, xrow, gt, *idx_by_slot, h2, u_rows)
    bg = 8 if nblk % 8 == 0 else nblk
    w = pl.pallas_call(
        _gate_kernel,
        grid=(PEER_SECTIONS, nblk // bg),
        in_specs=[pl.BlockSpec((bg, PEER_GPT, rows_per_g), lambda s, i: (s * (nblk // bg) + i, 0, 0)),
                  pl.BlockSpec((bg, PEER_GPT, rows_per_g), lambda s, i: (i, 0, 0))],
        out_specs=pl.BlockSpec((bg, PEER_GPT, rows_per_g), lambda s, i: (s * (nblk // bg) + i, 0, 0)),
        out_shape=jax.ShapeDtypeStruct((nstep, PEER_GPT, rows_per_g), jnp.float32),
        compiler_params=_cparams(("arbitrary", "arbitrary")),
        name="peer_gate",
    )(act, gw_blk)
    parts = pl.pallas_call(
        _peer_v_kernel,
        grid_spec=pltpu.PrefetchScalarGridSpec(
            num_scalar_prefetch=npre,
            grid=(PEER_SECTIONS, nblk),
            in_specs=[lst, lst, lst] + slots + [
                pl.BlockSpec((1, PEER_SLOTS, LANES), lambda s, i, *_: (s * nblk + i, 0, 0)), table],
            out_specs=pl.BlockSpec((1, PEER_TB, d), lambda s, i, *_: (s, i, 0)),
            scratch_shapes=[tab_scratch, pltpu.VMEM((slot_rows, LANES), jnp.float32),
                            pltpu.VMEM((PEER_TB * SUBLANES + SUBLANES, LANES), jnp.float32),
                            pltpu.SemaphoreType.DMA]),
        out_shape=jax.ShapeDtypeStruct((PEER_SECTIONS, n, d), jnp.float32),
        compiler_params=_cparams(("arbitrary", "arbitrary")),
        name="peer_v",
    )(nq, glo, ghi, xrow, gt, wrow, *idx_by_slot, w.reshape(nstep, PEER_SLOTS, LANES), v_rows)
    return parts


def _final_kernel(x1_ref, ff_ref, g2_ref, w_ref, b_ref, o_ref, *, alpha):
    ff = ff_ref[0] + ff_ref[1]
    o_ref[...] = _ln(alpha * x1_ref[...] + g2_ref[...] * ff) * w_ref[...] + b_ref[...]


def _final(x1, parts, g2, mod_map, tm, ln2w, ln2b, alpha):
    n, d = x1.shape
    rm = g2.shape[1] if g2.shape[1] == 1 else tm
    vec = pl.BlockSpec((1, d), lambda i: (0, 0))
    return pl.pallas_call(
        functools.partial(_final_kernel, alpha=alpha),
        grid=(n // tm,),
        in_specs=[pl.BlockSpec((tm, d), lambda i: (i, 0)),
                  pl.BlockSpec((PEER_SECTIONS, tm, d), lambda i: (0, i, 0)),
                  pl.BlockSpec((None, rm, d), lambda i: mod_map(i) + (0,)), vec, vec],
        out_specs=pl.BlockSpec((tm, d), lambda i: (i, 0)),
        out_shape=jax.ShapeDtypeStruct((n, d), jnp.float32),
        compiler_params=_cparams(("arbitrary",)),
        name="final_ln",
    )(x1, parts, g2, ln2w, ln2b)


def _layer_weights(l, w_in, w_gla_up, b_gla, gla_norm_w, w_br_a, w_br_b, w_out, ln1_w, ln1_b, w_pq,
                   peer_k1, peer_k2, ln2_w, ln2_b):
    d = w_in.shape[1]
    bf = lambda a: a.astype(jnp.bfloat16)
    sizes = (W_A, W_A, W_A, W_BK, W_BK, W_BV, W_BV, GATE_RANK, d, d)
    offs = np.concatenate([[0], np.cumsum(sizes)])
    col = lambda i, j=None: w_in[l][:, offs[i]:offs[(i if j is None else j) + 1]]
    pad_rank = GLR_PAD - GATE_RANK
    return {
        "wa": bf(col(0, 2)), "wqb": bf(col(3)), "wkb": bf(col(4)), "wvb": bf(col(5)), "wrb": bf(col(6)),
        "wglr": bf(jnp.pad(col(7), ((0, 0), (0, pad_rank)))), "wga": bf(col(8)), "wgb": bf(col(9)),
        "wup": bf(jnp.pad(w_gla_up[l], ((0, pad_rank), (0, 0)))), "bup": b_gla[l].reshape(1, -1),
        "gnw": gla_norm_w[l].reshape(1, -1), "wbra": bf(w_br_a[l]), "wbrb": bf(w_br_b[l]), "wout": bf(w_out[l]),
        "ln1w": ln1_w[l].reshape(1, -1), "ln1b": ln1_b[l].reshape(1, -1), "wpq": bf(w_pq[l]),
        "k1": bf(peer_k1[l]), "k2": bf(peer_k2[l]),
        "ln2w": ln2_w[l].reshape(1, -1), "ln2b": ln2_b[l].reshape(1, -1),
    }


def _ffn_and_norm(x1, h2, s1t, s2t, g2, mod_map, tm, wts, u_rows, v_rows, alpha):
    e, gw = _topk(s1t, s2t)
    parts = _peer(h2, e, gw, u_rows, v_rows)
    return _final(x1, parts, g2, mod_map, tm, wts["ln2w"], wts["ln2b"], alpha)


def kernel(x_prompt, x_sample, c_prompt, c_sample, cache_kv_w128, cache_kv_w512, cache_kv_w2048, state_gla, w_ada, b_ada, w_in, w_gla_up, b_gla, gla_norm_w, w_br_a, w_br_b, w_out, ln1_w, ln1_b, w_pq, peer_k1, peer_k2, peer_u, peer_v, ln2_w, ln2_b):
    depth = w_ada.shape[0]
    b, t, d = x_prompt.shape
    db, ds, _ = x_sample.shape
    assert ds == 1, "the single-token kernels take one new token per sequence"
    alpha = (2 * depth) ** 0.25
    tm_p = 256
    tm_s = db
    yp = x_prompt.reshape(b * t, d)
    ys = x_sample.reshape(db, d)
    caches = (cache_kv_w128, cache_kv_w512, cache_kv_w2048)
    kv_p = [[] for _ in DIL_CONFIGS]
    kv_s = [[] for _ in DIL_CONFIGS]
    gla_p, gla_s = [], []
    nc = b + db
    nc_pad = -(-nc // SUBLANES) * SUBLANES
    c_all = jnp.pad(jnp.concatenate([c_prompt, c_sample], axis=0), ((0, nc_pad - nc), (0, 0)))
    map_p = lambda i: (i // (t // tm_p), 0)
    map_s = lambda i: (0, i)
    for l in range(depth):
        wts = _layer_weights(l, w_in, w_gla_up, b_gla, gla_norm_w, w_br_a, w_br_b, w_out, ln1_w, ln1_b, w_pq,
                             peer_k1, peer_k2, ln2_w, ln2_b)
        u_rows = peer_u[l].reshape(-1, LANES)
        v_rows = peer_v[l].reshape(-1, LANES)
        mod = _adaln(c_all, w_ada[l], b_ada[l])
        mods_p = [m.reshape(b, 1, d) for m in jnp.split(mod[:b], 6, axis=-1)]
        mods_s = [m.reshape(1, db, d) for m in jnp.split(mod[b:b + db], 6, axis=-1)]

        a, qb, kb, vb, rb, gd, ga, gb = _inproj(yp, mods_p[0], mods_p[1], map_p, tm_p, wts)
        a3 = a.reshape(b, t, 3 * W_A)
        o_g, l_g = [], []
        for gi, (window, dil) in enumerate(DIL_CONFIGS):
            o, lse = _dil_prompt(a3, gi, dil)
            o_g.append(o.reshape(b * t, W_G))
            l_g.append(lse.reshape(b * t, W_G))
            keep = min(window, t)
            k_last = a3[:, t - keep:, W_A + gi * W_G:W_A + (gi + 1) * W_G]
            v_last = a3[:, t - keep:, 2 * W_A + gi * W_G:2 * W_A + (gi + 1) * W_G]
            kv_p[gi].append(jnp.stack([k_last, v_last], axis=2).reshape(b, keep, 2, HEADS_PER_GROUP, HEAD_DIM_A))
        og, s_fin = _gla_prompt(qb.reshape(b, t, W_BK), kb.reshape(b, t, W_BK), vb.reshape(b, t, W_BV),
                                gd.reshape(b, t, W_BK))
        gla_p.append(s_fin)
        x1, h2, s1t, s2t = _merge(o_g, l_g, og.reshape(b * t, W_BV), rb, ga, gb, yp, mods_p[2], mods_p[3],
                                  mods_p[4], map_p, tm_p, wts, alpha)
        yp = _ffn_and_norm(x1, h2, s1t, s2t, mods_p[5], map_p, tm_p, wts, u_rows, v_rows, alpha)

        a, qb, kb, vb, rb, gd, ga, gb = _inproj(ys, mods_s[0], mods_s[1], map_s, tm_s, wts)
        layer_caches = [c[l] for c in caches]
        o, lse, new_caches = _dil_sample(a, layer_caches)
        for gi in range(N_GROUPS):
            kv_s[gi].append(new_caches[gi])
        og, s_new = _gla_sample(qb, kb, vb, gd, state_gla[l])
        gla_s.append(s_new)
        o_g = [o[:, gi * W_G:(gi + 1) * W_G] for gi in range(N_GROUPS)]
        l_g = [lse[:, gi * W_G:(gi + 1) * W_G] for gi in range(N_GROUPS)]
        x1, h2, s1t, s2t = _merge(o_g, l_g, og, rb, ga, gb, ys, mods_s[2], mods_s[3], mods_s[4], map_s, tm_s,
                                  wts, alpha)
        ys = _ffn_and_norm(x1, h2, s1t, s2t, mods_s[5], map_s, tm_s, wts, u_rows, v_rows, alpha)

    return (yp.reshape(b, t, d), ys.reshape(db, ds, d),
            jnp.stack(kv_p[0]), jnp.stack(kv_p[1]), jnp.stack(kv_p[2]), jnp.stack(gla_p),
            jnp.stack(kv_s[0]), jnp.stack(kv_s[1]), jnp.stack(kv_s[2]), jnp.stack(gla_s))
```

```python
import functools

import numpy as np
import jax
import jax.numpy as jnp
from jax import lax
from jax.experimental import pallas as pl
from jax.experimental.pallas import tpu as pltpu

DIL_CONFIGS = ((128, 1), (512, 4), (2048, 16))
N_GROUPS = 3
HEADS_PER_GROUP = 4
HEAD_DIM_A = 64
W_G = HEADS_PER_GROUP * HEAD_DIM_A
W_A = N_GROUPS * W_G
DIL_BLOCK = 128
N_HEADS_B = 4
HEAD_K_B = 128
HEAD_V_B = 256
W_BK = N_HEADS_B * HEAD_K_B
W_BV = N_HEADS_B * HEAD_V_B
GATE_RANK = 16
GATE_TEMP = 16.0
GLR_PAD = 128
GLA_CHUNK = 64
GLA_SUB = 16
PEER_HEADS = 8
PEER_KEYS = 128
PEER_TOPK = 16
TOPK_SHIFT = PEER_TOPK.bit_length() - 1
PEER_PAIRS = PEER_HEADS * PEER_TOPK
LN_EPS = 1e-5

LANES = 128
SUBLANES = 8
VREG_ELEMS = LANES * SUBLANES
VMEM_LIMIT = 56 * 1024 * 1024

PEER_SECTIONS = 2
PEER_GROUP = SUBLANES
PEER_SLOTS = PEER_PAIRS + PEER_SECTIONS * PEER_GROUP
GROUP_SHIFT = PEER_GROUP.bit_length() - 1
PEER_GPT = PEER_SLOTS // PEER_GROUP
PEER_SPARE_G = PEER_GPT - 1
PEER_GSTRIDE = 32
PEER_TB = LANES
PEER_QCAP = PEER_TB * (PEER_PAIRS // PEER_GROUP)
PEER_UNROLL = 16
PEER_UNROLL_V = 8

_HI = lax.Precision.HIGHEST
_NEG = float("-inf")


def _cparams(sem, vmem=VMEM_LIMIT):
    return pltpu.CompilerParams(dimension_semantics=sem, vmem_limit_bytes=vmem)


def _ln(x):
    mu = jnp.mean(x, axis=-1, keepdims=True)
    xc = x - mu
    var = jnp.mean(xc * xc, axis=-1, keepdims=True)
    return xc * lax.rsqrt(var + LN_EPS)


def _bdot(a, b):
    return jnp.dot(a.astype(jnp.bfloat16), b.astype(jnp.bfloat16), preferred_element_type=jnp.float32)


def _bdot_nt(a, b):
    return lax.dot_general(a.astype(jnp.bfloat16), b.astype(jnp.bfloat16), (((1,), (1,)), ((), ())),
                           preferred_element_type=jnp.float32)


def _alibi_slope(head):
    return float(np.exp2(np.float32(-8.0 * (head + 1) / (N_GROUPS * HEADS_PER_GROUP))))


def _ada_kernel(c_ref, w_ref, b_ref, o_ref):
    c = c_ref[...]
    o_ref[...] = _bdot(c * jax.nn.sigmoid(c), w_ref[...]) + b_ref[...]


def _adaln(c, w_ada, b_ada):
    bc, d = c.shape
    ncol = w_ada.shape[1] // d
    return pl.pallas_call(
        _ada_kernel,
        grid=(ncol,),
        in_specs=[pl.BlockSpec((bc, d), lambda j: (0, 0)),
                  pl.BlockSpec((d, d), lambda j: (0, j)),
                  pl.BlockSpec((1, d), lambda j: (0, j))],
        out_specs=pl.BlockSpec((bc, d), lambda j: (0, j)),
        out_shape=jax.ShapeDtypeStruct((bc, ncol * d), jnp.float32),
        compiler_params=_cparams(("arbitrary",)),
        name="adaln",
    )(c, w_ada, b_ada.reshape(1, -1))


def _inproj_kernel(x_ref, sh_ref, sc_ref, wa_ref, wqb_ref, wkb_ref, wvb_ref, wrb_ref, wglr_ref, wga_ref, wgb_ref,
                   wup_ref, bup_ref, a_ref, qb_ref, kb_ref, vb_ref, rb_ref, gd_ref, ga_ref, gb_ref):
    h = (_ln(x_ref[...]) * (1.0 + sc_ref[...]) + sh_ref[...]).astype(jnp.bfloat16)

    def proj(w_ref):
        return jnp.dot(h, w_ref[...], preferred_element_type=jnp.float32)

    a_ref[...] = proj(wa_ref)
    qb_ref[...] = proj(wqb_ref) * (HEAD_K_B ** -0.5)
    kb_ref[...] = proj(wkb_ref)
    vb_ref[...] = proj(wvb_ref)
    rb_ref[...] = proj(wrb_ref)
    ga_ref[...] = proj(wga_ref)
    gb_ref[...] = proj(wgb_ref)
    glr = proj(wglr_ref)
    gate = _bdot(glr, wup_ref[...]) + bup_ref[...]
    gd_ref[...] = jax.nn.log_sigmoid(gate) * (1.0 / GATE_TEMP)


def _inproj(x2d, shift, scale, mod_map, tm, wts):
    n, d = x2d.shape
    rm = shift.shape[1] if shift.shape[1] == 1 else tm
    mod_spec = pl.BlockSpec((None, rm, d), lambda i: mod_map(i) + (0,))
    row = lambda w: pl.BlockSpec((tm, w), lambda i: (i, 0))
    const = lambda a: pl.BlockSpec(a.shape, lambda i: (0,) * a.ndim, pipeline_mode=pl.Buffered(1))
    names = ("wa", "wqb", "wkb", "wvb", "wrb", "wglr", "wga", "wgb", "wup", "bup")
    widths = (3 * W_A, W_BK, W_BK, W_BV, W_BV, W_BK, d, d)
    return pl.pallas_call(
        _inproj_kernel,
        grid=(n // tm,),
        in_specs=[row(d), mod_spec, mod_spec] + [const(wts[k]) for k in names],
        out_specs=[row(w) for w in widths],
        out_shape=[jax.ShapeDtypeStruct((n, w), jnp.float32) for w in widths],
        compiler_params=_cparams(("arbitrary",)),
        name="inproj",
    )(x2d, shift, scale, *[wts[k] for k in names])


def _dil_prompt_kernel(q_ref, kc_ref, kp_ref, vc_ref, vp_ref, o_ref, l_ref, o_scr, l_scr, *, group, dil):
    has_prev = pl.program_id(2) > 0
    half = pl.program_id(1)
    heads = LANES // HEAD_DIM_A
    qi = lax.broadcasted_iota(jnp.int32, (DIL_BLOCK, DIL_BLOCK), 0)
    ki = lax.broadcasted_iota(jnp.int32, (DIL_BLOCK, DIL_BLOCK), 1)
    valid_p = jnp.logical_and(ki >= qi, has_prev)
    valid_c = ki <= qi
    dist_p = ((qi + DIL_BLOCK - ki) * dil).astype(jnp.float32)
    dist_c = ((qi - ki) * dil).astype(jnp.float32)

    def phase(r):
        rows = pl.ds(r, DIL_BLOCK, stride=dil) if dil > 1 else pl.ds(0, DIL_BLOCK)
        q_all, kc, kp, vc, vp = (ref[rows, :] for ref in (q_ref, kc_ref, kp_ref, vc_ref, vp_ref))
        outs, lses = [], []
        for hh in range(heads):
            slope = jnp.where(half == 0, _alibi_slope(group * HEADS_PER_GROUP + hh),
                              _alibi_slope(group * HEADS_PER_GROUP + heads + hh))
            sl = slice(hh * HEAD_DIM_A, (hh + 1) * HEAD_DIM_A)
            q = q_all[:, sl]
            sp = _bdot_nt(q, kp[:, sl]) * (HEAD_DIM_A ** -0.5) - slope * dist_p
            sc = _bdot_nt(q, kc[:, sl]) * (HEAD_DIM_A ** -0.5) - slope * dist_c
            sp = jnp.where(valid_p, sp, _NEG)
            sc = jnp.where(valid_c, sc, _NEG)
            m = jnp.maximum(jnp.max(sp, axis=-1, keepdims=True), jnp.max(sc, axis=-1, keepdims=True))
            pp = jnp.exp(sp - m)
            pc = jnp.exp(sc - m)
            z = jnp.sum(pp, axis=-1, keepdims=True) + jnp.sum(pc, axis=-1, keepdims=True)
            outs.append((_bdot(pp, vp[:, sl]) + _bdot(pc, vc[:, sl])) / z)
            lses.append(jnp.broadcast_to(m + jnp.log(z), (DIL_BLOCK, HEAD_DIM_A)))
        o_scr[r] = jnp.concatenate(outs, axis=-1)
        l_scr[r] = jnp.concatenate(lses, axis=-1)

    per_trip = 2 if dil > 1 else 1

    def trip(i, carry):
        for u in range(per_trip):
            phase(i * per_trip + u)
        return carry

    lax.fori_loop(0, dil // per_trip, trip, 0)
    for r in range(dil):
        rows = pl.ds(r, DIL_BLOCK, stride=dil) if dil > 1 else pl.ds(0, DIL_BLOCK)
        o_ref[rows, :] = o_scr[r]
        l_ref[rows, :] = l_scr[r]


def _dil_prompt(qkv, group, dil):
    b, t, wq = qkv.shape
    span = dil * DIL_BLOCK
    assert t % span == 0
    halves = W_G // LANES
    assert halves == 2 and HEADS_PER_GROUP * HEAD_DIM_A == W_G
    qcol, kcol, vcol = group, N_GROUPS + group, 2 * N_GROUPS + group
    blk = (None, span, LANES)
    cur = lambda col: pl.BlockSpec(blk, lambda bi, h, n: (bi, n, halves * col + h))
    prev = lambda col: pl.BlockSpec(blk, lambda bi, h, n: (bi, jnp.maximum(n - 1, 0), halves * col + h))
    ospec = pl.BlockSpec(blk, lambda bi, h, n: (bi, n, h))
    scr = pltpu.VMEM((dil, DIL_BLOCK, LANES), jnp.float32)
    return pl.pallas_call(
        functools.partial(_dil_prompt_kernel, group=group, dil=dil),
        grid=(b, halves, t // span),
        in_specs=[cur(qcol), cur(kcol), prev(kcol), cur(vcol), prev(vcol)],
        out_specs=[ospec, ospec],
        out_shape=[jax.ShapeDtypeStruct((b, t, W_G), jnp.float32)] * 2,
        scratch_shapes=[scr, scr],
        compiler_params=_cparams(("arbitrary", "arbitrary", "arbitrary")),
        name=f"dil_prompt_g{group}",
    )(qkv, qkv, qkv, qkv, qkv)


def _gla_prompt_kernel(q_ref, k_ref, v_ref, g_ref, o_ref, sfin_ref, s_scr):
    c = pl.program_id(1)
    nchunk = pl.num_programs(1)
    C = GLA_CHUNK

    @pl.when(c == 0)
    def _():
        s_scr[...] = jnp.zeros_like(s_scr)

    ri = lax.broadcasted_iota(jnp.int32, (C, C), 0)
    ci = lax.broadcasted_iota(jnp.int32, (C, C), 1)
    tri = (ri >= ci).astype(jnp.float32)
    bcum = jnp.dot(tri, g_ref[...], precision=_HI, preferred_element_type=jnp.float32)
    row16 = lax.broadcasted_iota(jnp.int32, (GLA_SUB, HEAD_K_B), 0)
    lane16 = lax.broadcasted_iota(jnp.int32, (GLA_SUB, LANES), 1)
    rowc = lax.broadcasted_iota(jnp.int32, (C, HEAD_K_B), 0)
    nsub = C // GLA_SUB
    outs = []
    for h in range(N_HEADS_B):
        ks = slice(h * HEAD_K_B, (h + 1) * HEAD_K_B)
        vs = slice(h * HEAD_V_B, (h + 1) * HEAD_V_B)
        bh = bcum[:, ks]
        qh = q_ref[:, ks]
        kh = k_ref[:, ks]
        vh = v_ref[:, vs]
        sh = s_scr[h]
        o_inter = _bdot(qh * jnp.exp(bh), sh)
        arows = []
        for i in range(nsub):
            r0 = i * GLA_SUB
            bi = bh[r0:r0 + GLA_SUB]
            qi_ = qh[r0:r0 + GLA_SUB]
            ki_ = kh[r0:r0 + GLA_SUB]
            a = jnp.zeros((GLA_SUB, LANES), jnp.float32)
            for s in range(GLA_SUB):
                e = jnp.exp(jnp.where(row16 >= s, bi - bi[s:s + 1], _NEG))
                col = jnp.sum(qi_ * (ki_[s:s + 1] * e), axis=-1, keepdims=True)
                a = jnp.where(lane16 == r0 + s, col, a)
            a = a[:, :C]
            if i > 0:
                b0 = bi[0:1]
                qt = qi_ * jnp.exp(bi - b0)
                kt = kh * jnp.exp(jnp.where(rowc < r0, b0 - bh, _NEG))
                a = a + _bdot_nt(qt, kt)
            arows.append(a)
        amat = jnp.concatenate(arows, axis=0)
        outs.append(o_inter + _bdot(amat, vh))
        bl = bh[C - 1:C]
        kt = kh * jnp.exp(bl - bh)
        dcol = jnp.transpose(jnp.broadcast_to(jnp.exp(bl), (SUBLANES, HEAD_K_B)))[:, 0:1]
        upd = lax.dot_general(kt.astype(jnp.bfloat16), vh.astype(jnp.bfloat16), (((0,), (0,)), ((), ())),
                              preferred_element_type=jnp.float32)
        s_scr[h] = dcol * sh + upd
    o_ref[...] = jnp.concatenate(outs, axis=-1)

    @pl.when(c == nchunk - 1)
    def _():
        sfin_ref[...] = s_scr[...]


def _gla_prompt(qb, kb, vb, gd):
    b, t, _ = qb.shape
    assert t % GLA_CHUNK == 0
    spec = lambda w: pl.BlockSpec((None, GLA_CHUNK, w), lambda bi, c: (bi, c, 0))
    sshape = (N_HEADS_B, HEAD_K_B, HEAD_V_B)
    return pl.pallas_call(
        _gla_prompt_kernel,
        grid=(b, t // GLA_CHUNK),
        in_specs=[spec(W_BK), spec(W_BK), spec(W_BV), spec(W_BK)],
        out_specs=[spec(W_BV), pl.BlockSpec((None,) + sshape, lambda bi, c: (bi, 0, 0, 0))],
        out_shape=[jax.ShapeDtypeStruct((b, t, W_BV), jnp.float32),
                   jax.ShapeDtypeStruct((b,) + sshape, jnp.float32)],
        scratch_shapes=[pltpu.VMEM(sshape, jnp.float32)],
        compiler_params=_cparams(("arbitrary", "arbitrary")),
        name="gla_prompt",
    )(qb, kb, vb, gd)


SAMPLE_SEQS = 8


def _to_col(row):
    return jnp.transpose(jnp.broadcast_to(row, (SUBLANES, row.shape[1])))[:, 0:1]


def _to_row(col):
    return jnp.transpose(jnp.broadcast_to(col, (col.shape[0], SUBLANES)))[0:1, :]


def _dil_sample_kernel(a_ref, c0_ref, c1_ref, c2_ref, o_ref, l_ref, n0_ref, n1_ref, n2_ref):
    col = _to_col(a_ref[0])
    scale = HEAD_DIM_A ** -0.5
    kv_w = 2 * W_G
    for g, (cref, nref, (window, dil)) in enumerate(zip((c0_ref, c1_ref, c2_ref), (n0_ref, n1_ref, n2_ref),
                                                        DIL_CONFIGS)):
        q = col[g * W_G:(g + 1) * W_G]
        knew = col[W_A + g * W_G:W_A + (g + 1) * W_G]
        vnew = col[2 * W_A + g * W_G:2 * W_A + (g + 1) * W_G]
        x = cref[0]
        lane = lax.broadcasted_iota(jnp.int32, (1, window), 1)
        on_stride = (lane & (dil - 1)) == 0
        dist = (window - lane).astype(jnp.float32)
        o_cols, l_cols = [], []
        for hh in range(HEADS_PER_GROUP):
            rows = slice(hh * HEAD_DIM_A, (hh + 1) * HEAD_DIM_A)
            qh = q[rows]
            s = jnp.sum(x[rows, :] * qh, axis=0, keepdims=True) * scale
            s = s - _alibi_slope(g * HEADS_PER_GROUP + hh) * dist
            s = jnp.where(on_stride, s, _NEG)
            ss = jnp.sum(knew[rows] * qh, axis=0, keepdims=True) * scale
            m = jnp.maximum(jnp.max(s, axis=1, keepdims=True), ss)
            p = jnp.exp(s - m)
            ps = jnp.exp(ss - m)
            z = jnp.sum(p, axis=1, keepdims=True) + ps
            vrows = slice(W_G + hh * HEAD_DIM_A, W_G + (hh + 1) * HEAD_DIM_A)
            o_cols.append((jnp.sum(x[vrows, :] * p, axis=1, keepdims=True) + ps * vnew[rows]) / z)
            l_cols.append(jnp.broadcast_to(m + jnp.log(z), (HEAD_DIM_A, 1)))
        o_ref[0, :, g * W_G:(g + 1) * W_G] = _to_row(jnp.concatenate(o_cols, axis=0))
        l_ref[0, :, g * W_G:(g + 1) * W_G] = _to_row(jnp.concatenate(l_cols, axis=0))
        shifted = pltpu.roll(x, window - 1, axis=1)
        nref[0] = jnp.where(lane == window - 1, jnp.concatenate([knew, vnew], axis=0), shifted)


def _dil_sample(qkv, caches):
    db = qkv.shape[0]
    kv_w = 2 * W_G
    views, cspecs, oshapes = [], [], []
    for cache, (window, dil) in zip(caches, DIL_CONFIGS):
        assert cache.shape[1] == window, "window caches shorter than the window are not supported"
        assert dil & (dil - 1) == 0
        views.append(jnp.transpose(cache, (0, 2, 3, 4, 1)).reshape(db, kv_w, window))
        cspecs.append(pl.BlockSpec((1, kv_w, window), lambda i: (i, 0, 0)))
        oshapes.append(jax.ShapeDtypeStruct((db, kv_w, window), jnp.float32))
    rspec = lambda w: pl.BlockSpec((1, 1, w), lambda i: (i, 0, 0))
    o, lse, *new = pl.pallas_call(
        _dil_sample_kernel,
        grid=(db,),
        in_specs=[rspec(3 * W_A)] + cspecs,
        out_specs=[rspec(W_A), rspec(W_A)] + cspecs,
        out_shape=[jax.ShapeDtypeStruct((db, 1, W_A), jnp.float32)] * 2 + oshapes,
        compiler_params=_cparams(("arbitrary",)),
        name="dil_sample",
    )(qkv.reshape(db, 1, 3 * W_A), *views)
    new = [jnp.transpose(n.reshape(db, 2, HEADS_PER_GROUP, HEAD_DIM_A, n.shape[-1]), (0, 4, 1, 2, 3)) for n in new]
    return o.reshape(db, W_A), lse.reshape(db, W_A), new


def _gla_sample_kernel(q_ref, k_ref, v_ref, g_ref, s0_ref, o_ref, s_ref):
    for h in range(N_HEADS_B):
        ks = slice(h * HEAD_K_B, (h + 1) * HEAD_K_B)
        vs = slice(h * HEAD_V_B, (h + 1) * HEAD_V_B)
        qT = jnp.transpose(q_ref[:, ks])
        kT = jnp.transpose(k_ref[:, ks])
        aT = jnp.transpose(jnp.exp(g_ref[:, ks]))
        for j in range(SAMPLE_SEQS):
            s_new = aT[:, j:j + 1] * s0_ref[j, h] + kT[:, j:j + 1] * v_ref[j:j + 1, vs]
            s_ref[j, h] = s_new
            o_ref[j:j + 1, vs] = jnp.sum(qT[:, j:j + 1] * s_new, axis=0, keepdims=True)


def _gla_sample(qb, kb, vb, gd, s0):
    db = qb.shape[0]
    row = lambda w: pl.BlockSpec((SAMPLE_SEQS, w), lambda i: (i, 0))
    sspec = pl.BlockSpec((SAMPLE_SEQS, N_HEADS_B, HEAD_K_B, HEAD_V_B), lambda i: (i, 0, 0, 0))
    return pl.pallas_call(
        _gla_sample_kernel,
        grid=(db // SAMPLE_SEQS,),
        in_specs=[row(W_BK), row(W_BK), row(W_BV), row(W_BK), sspec],
        out_specs=[row(W_BV), sspec],
        out_shape=[jax.ShapeDtypeStruct((db, W_BV), jnp.float32), jax.ShapeDtypeStruct(s0.shape, jnp.float32)],
        compiler_params=_cparams(("arbitrary",)),
        name="gla_sample",
    )(qb, kb, vb, gd, s0)


def _merge_kernel(o0_ref, o1_ref, o2_ref, l0_ref, l1_ref, l2_ref, og_ref, rb_ref, ga_ref, gb_ref, x_ref,
                  g1_ref, sh2_ref, sc2_ref, gnw_ref, wbra_ref, wbrb_ref, wout_ref, ln1w_ref, ln1b_ref,
                  wpq_ref, k1_ref, k2_ref, x1_ref, h2_ref, s1_ref, s2_ref, *, alpha):
    l0, l1, l2 = l0_ref[...], l1_ref[...], l2_ref[...]
    m = jnp.maximum(jnp.maximum(l0, l1), l2)
    e0, e1, e2 = jnp.exp(l0 - m), jnp.exp(l1 - m), jnp.exp(l2 - m)
    oa = (e0 * o0_ref[...] + e1 * o1_ref[...] + e2 * o2_ref[...]) / (e0 + e1 + e2)
    og = og_ref[...]
    parts = []
    for h in range(N_HEADS_B):
        oh = og[:, h * HEAD_V_B:(h + 1) * HEAD_V_B]
        parts.append(oh * lax.rsqrt(jnp.mean(oh * oh, axis=-1, keepdims=True) + LN_EPS))
    rb = rb_ref[...]
    ob = jnp.concatenate(parts, axis=-1) * gnw_ref[...] * (rb * jax.nn.sigmoid(rb))
    merged = (jax.nn.sigmoid(ga_ref[...]) * _bdot(oa, wbra_ref[...])
              + jax.nn.sigmoid(gb_ref[...]) * _bdot(ob, wbrb_ref[...]))
    mix = _bdot(merged, wout_ref[...])
    x1 = _ln(alpha * x_ref[...] + g1_ref[...] * mix) * ln1w_ref[...] + ln1b_ref[...]
    x1_ref[...] = x1
    h2 = _ln(x1) * (1.0 + sc2_ref[...]) + sh2_ref[...]
    h2_ref[...] = h2
    qv = _bdot(h2, wpq_ref[...]).astype(jnp.bfloat16)
    half = PEER_KEYS
    for h in range(PEER_HEADS):
        base = h * 2 * half
        s1_ref[h] = lax.dot_general(k1_ref[h], qv[:, base:base + half], (((1,), (1,)), ((), ())),
                                    preferred_element_type=jnp.float32)
        s2_ref[h] = lax.dot_general(k2_ref[h], qv[:, base + half:base + 2 * half], (((1,), (1,)), ((), ())),
                                    preferred_element_type=jnp.float32)


def _merge(o_g, l_g, og, rb, ga, gb, x2d, g1, sh2, sc2, mod_map, tm, wts, alpha):
    n, d = x2d.shape
    rm = g1.shape[1] if g1.shape[1] == 1 else tm
    mod_spec = pl.BlockSpec((None, rm, d), lambda i: mod_map(i) + (0,))
    row = lambda w: pl.BlockSpec((tm, w), lambda i: (i, 0))
    const = lambda a: pl.BlockSpec(a.shape, lambda i: (0,) * a.ndim, pipeline_mode=pl.Buffered(1))
    names = ("gnw", "wbra", "wbrb", "wout", "ln1w", "ln1b", "wpq", "k1", "k2")
    sspec = pl.BlockSpec((PEER_HEADS, PEER_KEYS, tm), lambda i: (0, 0, i))
    return pl.pallas_call(
        functools.partial(_merge_kernel, alpha=alpha),
        grid=(n // tm,),
        in_specs=[row(W_G)] * 6 + [row(W_BV), row(W_BV), row(d), row(d), row(d), mod_spec, mod_spec, mod_spec]
                 + [const(wts[k]) for k in names],
        out_specs=[row(d), row(d), sspec, sspec],
        out_shape=[jax.ShapeDtypeStruct((n, d), jnp.float32)] * 2
                  + [jax.ShapeDtypeStruct((PEER_HEADS, PEER_KEYS, n), jnp.float32)] * 2,
        compiler_params=_cparams(("arbitrary",)),
        name="merge",
    )(*o_g, *l_g, og, rb, ga, gb, x2d, g1, sh2, sc2, *[wts[k] for k in names])


_CAND_IDS = tuple(i * PEER_TOPK + j for i in range(PEER_TOPK) for j in range(PEER_TOPK)
                  if (i + 1) * (j + 1) <= PEER_TOPK)
_NO_ID = PEER_KEYS * PEER_KEYS
TOPK_CHUNK = 16


def _topk_kernel(s1_ref, s2_ref, e_ref, w_ref, sa, sb, va, vb, ia, ib, cand, sc_scr, ci_scr):
    sub = s1_ref.shape[2]
    shape = (sub, LANES)
    sa[...] = s1_ref[0]
    sb[...] = s2_ref[0]

    def tree(op, xs):
        xs = list(xs)
        while len(xs) > 1:
            xs = [op(xs[k], xs[k + 1]) for k in range(0, len(xs) - 1, 2)] + (xs[-1:] if len(xs) % 2 else [])
        return xs[0]

    def extract(s_scr, ids, r, v_out, i_out):
        n = len(ids)
        chunks = [range(c, min(c + TOPK_CHUNK, n)) for c in range(0, n, TOPK_CHUNK)]
        m = tree(jnp.maximum, [tree(jnp.maximum, [s_scr[k] for k in ch]) for ch in chunks])
        idx = tree(jnp.minimum, [tree(jnp.minimum, [jnp.where(s_scr[k] == m, ids[k], _NO_ID) for k in ch])
                                 for ch in chunks])
        for k in range(n):
            s_scr[k] = jnp.where(idx == ids[k], _NEG, s_scr[k])
        v_out[r] = m
        i_out[r] = idx

    def stage1(r, c):
        extract(sa, range(PEER_KEYS), r, va, ia)
        extract(sb, range(PEER_KEYS), r, vb, ib)
        return c

    lax.fori_loop(0, PEER_TOPK, stage1, 0)
    for k, ci in enumerate(_CAND_IDS):
        cand[k] = va[ci // PEER_TOPK] + vb[ci % PEER_TOPK]

    def stage2(r, c):
        extract(cand, _CAND_IDS, r, sc_scr, ci_scr)
        return c

    lax.fori_loop(0, PEER_TOPK, stage2, 0)
    top = sc_scr[0]
    z = jnp.zeros(shape, jnp.float32)
    for r in range(PEER_TOPK):
        z = z + jnp.exp(sc_scr[r] - top)
    for r in range(PEER_TOPK):
        w_ref[0, 0, r] = jnp.exp(sc_scr[r] - top) / z
        ci = ci_scr[r]
        hi = lax.shift_right_logical(ci, TOPK_SHIFT)
        lo = ci & (PEER_TOPK - 1)
        e1 = jnp.zeros(shape, jnp.int32)
        e2 = jnp.zeros(shape, jnp.int32)
        for i in range(PEER_TOPK):
            e1 = jnp.where(hi == i, ia[i], e1)
            e2 = jnp.where(lo == i, ib[i], e2)
        e_ref[0, 0, r] = e1 * PEER_KEYS + e2


def _topk(s1t, s2t):
    nh, nk, n = s1t.shape
    sub = min(SUBLANES, n // LANES)
    nchunk = n // (sub * LANES)
    v1 = s1t.reshape(nh, nk, n // LANES, LANES)
    v2 = s2t.reshape(nh, nk, n // LANES, LANES)
    ispec = pl.BlockSpec((1, nk, sub, LANES), lambda c, h: (h, 0, c, 0))
    ospec = pl.BlockSpec((1, 1, PEER_TOPK, sub, LANES), lambda c, h: (c, h, 0, 0, 0))
    oshape = (nchunk, nh, PEER_TOPK, sub, LANES)
    key = lambda k, dt: pltpu.VMEM((k, sub, LANES), dt)
    e, w = pl.pallas_call(
        _topk_kernel,
        grid=(nchunk, nh),
        in_specs=[ispec, ispec],
        out_specs=[ospec, ospec],
        out_shape=[jax.ShapeDtypeStruct(oshape, jnp.int32), jax.ShapeDtypeStruct(oshape, jnp.float32)],
        scratch_shapes=[key(nk, jnp.float32), key(nk, jnp.float32),
                        key(PEER_TOPK, jnp.float32), key(PEER_TOPK, jnp.float32),
                        key(PEER_TOPK, jnp.int32), key(PEER_TOPK, jnp.int32),
                        key(len(_CAND_IDS), jnp.float32), key(PEER_TOPK, jnp.float32),
                        key(PEER_TOPK, jnp.int32)],
        compiler_params=_cparams(("arbitrary", "arbitrary")),
        name="peer_topk",
    )(v1, v2)
    return e, w


def _load_section(tab_hbm, tab_vmem, sem):
    rows = tab_vmem.shape[0]

    @pl.when(pl.program_id(1) == 0)
    def _():
        start = pl.multiple_of(pl.program_id(0) * rows, SUBLANES)
        cp = pltpu.make_async_copy(tab_hbm.at[pl.ds(start, rows)], tab_vmem, sem)
        cp.start()
        cp.wait()


def _grid_step():
    return pl.program_id(0) * pl.num_programs(1) + pl.program_id(1)


def _peer_u_kernel(nq_ref, glo_ref, ghi_ref, xrow_ref, gt_ref, *refs):
    idx_refs = refs[:PEER_GROUP]
    x_ref, tab_hbm, act_ref, tab_vmem, dbuf, xs, sem = refs[PEER_GROUP:]
    _load_section(tab_hbm, tab_vmem, sem)
    step = _grid_step()

    @pl.when(step == 0)
    def _():
        dbuf[...] = jnp.zeros_like(dbuf)

    for k in range(SUBLANES):
        xs[pl.ds(k, PEER_TB, stride=SUBLANES), :] = x_ref[:, k * LANES:(k + 1) * LANES]

    nq = nq_ref[step]
    sub = lax.broadcasted_iota(jnp.int32, (SUBLANES, LANES), 0)
    order = _fold_slot_order()

    def fold(a, b, sh):
        keep = (sub & sh) == 0
        u = jnp.where(keep, a, b)
        v = jnp.where(keep, b, a)
        if 2 * sh == SUBLANES:
            w = pltpu.roll(v, sh, axis=0)
        else:
            w = jnp.where(keep, pltpu.roll(v, SUBLANES - sh, axis=0), pltpu.roll(v, sh, axis=0))
        return u + w

    def body(i, c):
        for u in range(PEER_UNROLL):
            q = i * PEER_UNROLL + u
            gt = gt_ref[q]
            x = xs[pl.ds(pl.multiple_of(xrow_ref[q], SUBLANES), SUBLANES), :]
            ps = [tab_vmem[pl.ds(pl.multiple_of(idx_refs[order[j]][gt], SUBLANES), SUBLANES), :] * x
                  for j in range(PEER_GROUP)]
            while len(ps) > 1:
                sh = len(ps) // 2
                ps = [fold(ps[2 * k], ps[2 * k + 1], sh) for k in range(sh)]
            dbuf[pl.ds(pl.multiple_of(gt * PEER_GROUP, SUBLANES), SUBLANES), :] = ps[0]
        return c

    lax.fori_loop(0, nq // PEER_UNROLL, body, 0)
    rows_per_g = PEER_GROUP * PEER_TB
    glo, ghi = glo_ref[step], ghi_ref[step]

    def zfill(g, carry):
        act_ref[0, pl.ds(pl.multiple_of(g * PEER_GROUP, PEER_GROUP), PEER_GROUP), :] = jnp.zeros(
            (PEER_GROUP, PEER_TB), jnp.float32)
        return carry

    def reduce(g, carry):
        for j in range(PEER_GROUP):
            rows = dbuf[pl.ds(g * rows_per_g + j, PEER_TB, stride=PEER_GROUP), :]
            act_ref[0, pl.ds(g * PEER_GROUP + j, 1), :] = jnp.sum(jnp.transpose(rows), axis=0, keepdims=True)
        return carry

    lax.fori_loop(0, glo, zfill, 0)
    lax.fori_loop(glo, ghi, reduce, 0)
    lax.fori_loop(ghi, PEER_GPT, zfill, 0)


def _fold_slot_order():
    pos = [[j] for j in range(PEER_GROUP)]
    sl = [0] * PEER_GROUP
    sh = PEER_GROUP // 2
    groups = pos
    while len(groups) > 1:
        nxt = []
        for i in range(len(groups) // 2):
            for j in groups[2 * i + 1]:
                sl[j] |= sh
            nxt.append(groups[2 * i] + groups[2 * i + 1])
        groups = nxt
        sh //= 2
    return sl


def _peer_v_kernel(nq_ref, glo_ref, ghi_ref, xrow_ref, gt_ref, wrow_ref, *refs):
    idx_refs = refs[:PEER_GROUP]
    w_ref, tab_hbm, out_ref, tab_vmem, wb, acc_scr, sem = refs[PEER_GROUP:]
    _load_section(tab_hbm, tab_vmem, sem)
    step = _grid_step()
    nq = nq_ref[step]
    rows_per_g = PEER_GROUP * PEER_TB

    @pl.when(step == 0)
    def _():
        wb[pl.ds(PEER_SPARE_G * rows_per_g, rows_per_g), :] = jnp.zeros((rows_per_g, LANES), jnp.float32)

    acc_scr[...] = jnp.zeros_like(acc_scr)

    def spread(g, carry):
        for j in range(PEER_GROUP):
            r = g * PEER_GROUP + j
            rep = jnp.broadcast_to(w_ref[0, pl.ds(r, 1), :], (LANES, LANES))
            wb[pl.ds(pl.multiple_of(r * PEER_TB, PEER_TB), PEER_TB), :] = jnp.transpose(rep)
        return carry

    lax.fori_loop(glo_ref[step], ghi_ref[step], spread, 0)
    trash = PEER_TB * SUBLANES

    def body(i, carry):
        cur, acc = carry
        for u in range(PEER_UNROLL_V):
            q = i * PEER_UNROLL_V + u
            gt = gt_ref[q]
            row0 = wrow_ref[q]
            terms = [tab_vmem[pl.ds(pl.multiple_of(idx_refs[j][gt], SUBLANES), SUBLANES), :]
                     * wb[pl.ds(row0 + j * PEER_TB, 1), :] for j in range(PEER_GROUP)]
            while len(terms) > 1:
                terms = [terms[2 * k] + terms[2 * k + 1] for k in range(len(terms) // 2)]
            row = xrow_ref[q]
            acc_scr[pl.ds(pl.multiple_of(cur, SUBLANES), SUBLANES), :] = acc
            acc = jnp.where(row == cur, acc + terms[0], terms[0])
            cur = row
        return cur, acc

    cur, acc = lax.fori_loop(0, nq // PEER_UNROLL_V, body,
                             (jnp.int32(trash), jnp.zeros((SUBLANES, LANES), jnp.float32)))
    acc_scr[pl.ds(pl.multiple_of(cur, SUBLANES), SUBLANES), :] = acc
    for k in range(SUBLANES):
        out_ref[0, :, k * LANES:(k + 1) * LANES] = acc_scr[pl.ds(k, PEER_TB, stride=SUBLANES), :]


def _gate_kernel(a_ref, gw_ref, w_ref):
    act = a_ref[...]
    w_ref[...] = gw_ref[...] * (0.5 * act * (1.0 + lax.erf(act * (2.0 ** -0.5))))


def _router_kernel(e_ref, w_ref, *refs, sec_experts):
    idx_refs = refs[:PEER_GROUP]
    gw_ref, g0_ref, g01_ref, pos_scr, loc_scr = refs[PEER_GROUP:]
    shape = e_ref.shape[3:]
    sub = shape[0]
    zero = jnp.zeros(shape, jnp.int32)
    r0, r1 = zero, zero
    for p in range(PEER_PAIRS):
        e = e_ref[0, p // PEER_TOPK, p % PEER_TOPK]
        upper = e >= sec_experts
        loc_scr[p] = (e & (sec_experts - 1)) * SUBLANES
        pos_scr[p] = jnp.where(upper, r1 + PEER_SLOTS, r0)
        r0 = r0 + jnp.where(upper, 0, 1)
        r1 = r1 + jnp.where(upper, 1, 0)
    g0 = lax.shift_right_logical(r0 + (PEER_GROUP - 1), GROUP_SHIFT)
    g1 = lax.shift_right_logical(r1 + (PEER_GROUP - 1), GROUP_SHIFT)
    g0_ref[0] = g0
    g01_ref[0] = g0 + g1
    rebase = g0 * PEER_GROUP - PEER_SLOTS
    for p in range(PEER_PAIRS):
        pos = pos_scr[p]
        pos_scr[p] = jnp.where(pos >= PEER_SLOTS, pos + rebase, pos)

    def place(g, carry):
        for j in range(PEER_GROUP):
            slot = g * PEER_GROUP + j
            iv = zero
            wv = jnp.zeros(shape, jnp.float32)
            for p in range(PEER_PAIRS):
                hit = pos_scr[p] == slot
                iv = jnp.where(hit, loc_scr[p], iv)
                wv = jnp.where(hit, w_ref[0, p // PEER_TOPK, p % PEER_TOPK], wv)
            for s in range(sub):
                idx_refs[j][0, s, pl.ds(g, 1), :] = iv[s:s + 1, :]
                gw_ref[0, s, pl.ds(slot, 1), :] = wv[s:s + 1, :]
        return carry

    lax.fori_loop(0, PEER_GPT, place, 0)
    for j in range(PEER_GROUP):
        idx_refs[j][0, :, PEER_GPT:, :] = jnp.zeros((sub, PEER_GSTRIDE - PEER_GPT, LANES), jnp.int32)


def _router(e, w, n_experts):
    nchunk, nh, k, sub, _ = e.shape
    sec_experts = n_experts // PEER_SECTIONS
    assert sec_experts & (sec_experts - 1) == 0 and PEER_SECTIONS == 2
    ispec = pl.BlockSpec((1, nh, k, sub, LANES), lambda c: (c, 0, 0, 0, 0))
    bspec = pl.BlockSpec((1, sub, LANES), lambda c: (c, 0, 0))
    scr = lambda dt: pltpu.VMEM((PEER_PAIRS, sub, LANES), dt)
    blk4 = lambda rows: pl.BlockSpec((1, sub, rows, LANES), lambda c: (c, 0, 0, 0))
    *idx, gw, g0, g01 = pl.pallas_call(
        functools.partial(_router_kernel, sec_experts=sec_experts),
        grid=(nchunk,),
        in_specs=[ispec, ispec],
        out_specs=[blk4(PEER_GSTRIDE)] * PEER_GROUP + [blk4(PEER_SLOTS), bspec, bspec],
        out_shape=[jax.ShapeDtypeStruct((nchunk, sub, PEER_GSTRIDE, LANES), jnp.int32)] * PEER_GROUP
                  + [jax.ShapeDtypeStruct((nchunk, sub, PEER_SLOTS, LANES), jnp.float32),
                     jax.ShapeDtypeStruct((nchunk, sub, LANES), jnp.int32),
                     jax.ShapeDtypeStruct((nchunk, sub, LANES), jnp.int32)],
        scratch_shapes=[scr(jnp.int32), scr(jnp.int32)],
        compiler_params=_cparams(("arbitrary",)),
        name="peer_router",
    )(e, w)
    return idx, gw, g0, g01


def _peer_lists(g_lo, g_hi):
    nblk = g_lo.shape[0]
    cnt = g_hi - g_lo
    end = jnp.cumsum(cnt, axis=1)
    start = end - cnt
    q = jnp.arange(PEER_QCAP, dtype=jnp.int32)[None, :, None]
    inside = jnp.logical_and(start[:, None, :] <= q, q < end[:, None, :])
    tok = jnp.arange(PEER_TB, dtype=jnp.int32)[None, None, :]
    t = jnp.sum(jnp.where(inside, tok, 0), axis=2)
    g = q[:, :, 0] + jnp.sum(jnp.where(inside, (g_lo - start)[:, None, :], 0), axis=2)
    count = end[:, -1:]
    live = q[:, :, 0] < count
    xrow = jnp.where(live, t * SUBLANES, 0)
    xrow = jnp.where(live, xrow, jnp.max(xrow, axis=1, keepdims=True))
    gt = jnp.where(live, g * PEER_TB + t, PEER_SPARE_G * PEER_TB)
    nq = (count[:, 0] + PEER_UNROLL - 1) // PEER_UNROLL * PEER_UNROLL
    return nq, jnp.min(g_lo, axis=1), jnp.max(g_hi, axis=1), xrow, gt


def _peer(h2, e, gw, u_rows, v_rows):
    n, d = h2.shape
    assert d == VREG_ELEMS and n % PEER_TB == 0 and PEER_TB == LANES
    n_experts = u_rows.shape[0] // SUBLANES
    sec_rows = u_rows.shape[0] // PEER_SECTIONS
    nblk = n // PEER_TB
    nstep = PEER_SECTIONS * nblk
    idx4, gw4, g0, g01 = _router(e, gw, n_experts)
    idx_by_slot = [a.reshape(-1) for a in idx4]
    rows_per_g = PEER_GROUP * PEER_TB
    gw_blk = gw4.reshape(nblk, PEER_SLOTS, PEER_TB)
    g0 = g0.reshape(nblk, PEER_TB)
    g01 = g01.reshape(nblk, PEER_TB)
    bounds = (jnp.zeros_like(g0), g0, g01)
    lists = [_peer_lists(bounds[s], bounds[s + 1]) for s in range(PEER_SECTIONS)]
    nq, glo, ghi, xrow, gt = [jnp.stack(a).reshape(-1) for a in zip(*lists)]
    wrow = gt + (gt // PEER_TB) * ((PEER_GROUP - 1) * PEER_TB)

    npre = 3
    lst = pl.BlockSpec((PEER_QCAP,), lambda s, i, *_: (s * nblk + i,), memory_space=pltpu.SMEM)
    slots = [pl.BlockSpec((PEER_TB * PEER_GSTRIDE,), lambda s, i, *_: (i,), memory_space=pltpu.SMEM)
             ] * PEER_GROUP
    table = pl.BlockSpec(memory_space=pl.ANY)
    tab_scratch = pltpu.VMEM((sec_rows, LANES), jnp.float32)
    slot_rows = PEER_GPT * rows_per_g
    act = pl.pallas_call(
        _peer_u_kernel,
        grid_spec=pltpu.PrefetchScalarGridSpec(
            num_scalar_prefetch=npre,
            grid=(PEER_SECTIONS, nblk),
            in_specs=[lst, lst] + slots + [pl.BlockSpec((PEER_TB, d), lambda s, i, *_: (i, 0)), table],
            out_specs=pl.BlockSpec((1, PEER_SLOTS, PEER_TB), lambda s, i, *_: (s * nblk + i, 0, 0)),
            scratch_shapes=[tab_scratch, pltpu.VMEM((slot_rows, LANES), jnp.float32),
                            pltpu.VMEM((PEER_TB * SUBLANES, LANES), jnp.float32), pltpu.SemaphoreType.DMA]),
        out_shape=jax.ShapeDtypeStruct((nstep, PEER_SLOTS, PEER_TB), jnp.float32),
        compiler_params=_cparams(("arbitrary", "arbitrary")),
        name="peer_u",
    )(nq, glo, ghi, xrow, gt, *idx_by_slot, h2, u_rows)
    bg = 8 if nblk % 8 == 0 else nblk
    w = pl.pallas_call(
        _gate_kernel,
        grid=(PEER_SECTIONS, nblk // bg),
        in_specs=[pl.BlockSpec((bg, PEER_SLOTS, PEER_TB), lambda s, i: (s * (nblk // bg) + i, 0, 0)),
                  pl.BlockSpec((bg, PEER_SLOTS, PEER_TB), lambda s, i: (i, 0, 0))],
        out_specs=pl.BlockSpec((bg, PEER_SLOTS, PEER_TB), lambda s, i: (s * (nblk // bg) + i, 0, 0)),
        out_shape=jax.ShapeDtypeStruct((nstep, PEER_SLOTS, PEER_TB), jnp.float32),
        compiler_params=_cparams(("arbitrary", "arbitrary")),
        name="peer_gate",
    )(act, gw_blk)
    parts = pl.pallas_call(
        _peer_v_kernel,
        grid_spec=pltpu.PrefetchScalarGridSpec(
            num_scalar_prefetch=npre,
            grid=(PEER_SECTIONS, nblk),
            in_specs=[lst, lst, lst] + slots + [
                pl.BlockSpec((1, PEER_SLOTS, LANES), lambda s, i, *_: (s * nblk + i, 0, 0)), table],
            out_specs=pl.BlockSpec((1, PEER_TB, d), lambda s, i, *_: (s, i, 0)),
            scratch_shapes=[tab_scratch, pltpu.VMEM((slot_rows, LANES), jnp.float32),
                            pltpu.VMEM((PEER_TB * SUBLANES + SUBLANES, LANES), jnp.float32),
                            pltpu.SemaphoreType.DMA]),
        out_shape=jax.ShapeDtypeStruct((PEER_SECTIONS, n, d), jnp.float32),
        compiler_params=_cparams(("arbitrary", "arbitrary")),
        name="peer_v",
    )(nq, glo, ghi, xrow, gt, wrow, *idx_by_slot, w, v_rows)
    return parts


def _final_kernel(x1_ref, ff_ref, g2_ref, w_ref, b_ref, o_ref, *, alpha):
    ff = ff_ref[0] + ff_ref[1]
    o_ref[...] = _ln(alpha * x1_ref[...] + g2_ref[...] * ff) * w_ref[...] + b_ref[...]


def _final(x1, parts, g2, mod_map, tm, ln2w, ln2b, alpha):
    n, d = x1.shape
    rm = g2.shape[1] if g2.shape[1] == 1 else tm
    vec = pl.BlockSpec((1, d), lambda i: (0, 0))
    return pl.pallas_call(
        functools.partial(_final_kernel, alpha=alpha),
        grid=(n // tm,),
        in_specs=[pl.BlockSpec((tm, d), lambda i: (i, 0)),
                  pl.BlockSpec((PEER_SECTIONS, tm, d), lambda i: (0, i, 0)),
                  pl.BlockSpec((None, rm, d), lambda i: mod_map(i) + (0,)), vec, vec],
        out_specs=pl.BlockSpec((tm, d), lambda i: (i, 0)),
        out_shape=jax.ShapeDtypeStruct((n, d), jnp.float32),
        compiler_params=_cparams(("arbitrary",)),
        name="final_ln",
    )(x1, parts, g2, ln2w, ln2b)


def _layer_weights(l, w_in, w_gla_up, b_gla, gla_norm_w, w_br_a, w_br_b, w_out, ln1_w, ln1_b, w_pq,
                   peer_k1, peer_k2, ln2_w, ln2_b):
    d = w_in.shape[1]
    bf = lambda a: a.astype(jnp.bfloat16)
    sizes = (W_A, W_A, W_A, W_BK, W_BK, W_BV, W_BV, GATE_RANK, d, d)
    offs = np.concatenate([[0], np.cumsum(sizes)])
    col = lambda i, j=None: w_in[l][:, offs[i]:offs[(i if j is None else j) + 1]]
    pad_rank = GLR_PAD - GATE_RANK
    return {
        "wa": bf(col(0, 2)), "wqb": bf(col(3)), "wkb": bf(col(4)), "wvb": bf(col(5)), "wrb": bf(col(6)),
        "wglr": bf(jnp.pad(col(7), ((0, 0), (0, pad_rank)))), "wga": bf(col(8)), "wgb": bf(col(9)),
        "wup": bf(jnp.pad(w_gla_up[l], ((0, pad_rank), (0, 0)))), "bup": b_gla[l].reshape(1, -1),
        "gnw": gla_norm_w[l].reshape(1, -1), "wbra": bf(w_br_a[l]), "wbrb": bf(w_br_b[l]), "wout": bf(w_out[l]),
        "ln1w": ln1_w[l].reshape(1, -1), "ln1b": ln1_b[l].reshape(1, -1), "wpq": bf(w_pq[l]),
        "k1": bf(peer_k1[l]), "k2": bf(peer_k2[l]),
        "ln2w": ln2_w[l].reshape(1, -1), "ln2b": ln2_b[l].reshape(1, -1),
    }


def _ffn_and_norm(x1, h2, s1t, s2t, g2, mod_map, tm, wts, u_rows, v_rows, alpha):
    e, gw = _topk(s1t, s2t)
    parts = _peer(h2, e, gw, u_rows, v_rows)
    return _final(x1, parts, g2, mod_map, tm, wts["ln2w"], wts["ln2b"], alpha)


def kernel(x_prompt, x_sample, c_prompt, c_sample, cache_kv_w128, cache_kv_w512, cache_kv_w2048, state_gla, w_ada, b_ada, w_in, w_gla_up, b_gla, gla_norm_w, w_br_a, w_br_b, w_out, ln1_w, ln1_b, w_pq, peer_k1, peer_k2, peer_u, peer_v, ln2_w, ln2_b):
    depth = w_ada.shape[0]
    b, t, d = x_prompt.shape
    db, ds, _ = x_sample.shape
    assert ds == 1, "the single-token kernels take one new token per sequence"
    alpha = (2 * depth) ** 0.25
    tm_p = 256
    tm_s = db
    yp = x_prompt.reshape(b * t, d)
    ys = x_sample.reshape(db, d)
    caches = (cache_kv_w128, cache_kv_w512, cache_kv_w2048)
    kv_p = [[] for _ in DIL_CONFIGS]
    kv_s = [[] for _ in DIL_CONFIGS]
    gla_p, gla_s = [], []
    nc = b + db
    nc_pad = -(-nc // SUBLANES) * SUBLANES
    c_all = jnp.pad(jnp.concatenate([c_prompt, c_sample], axis=0), ((0, nc_pad - nc), (0, 0)))
    map_p = lambda i: (i // (t // tm_p), 0)
    map_s = lambda i: (0, i)
    for l in range(depth):
        wts = _layer_weights(l, w_in, w_gla_up, b_gla, gla_norm_w, w_br_a, w_br_b, w_out, ln1_w, ln1_b, w_pq,
                             peer_k1, peer_k2, ln2_w, ln2_b)
        u_rows = peer_u[l].reshape(-1, LANES)
        v_rows = peer_v[l].reshape(-1, LANES)
        mod = _adaln(c_all, w_ada[l], b_ada[l])
        mods_p = [m.reshape(b, 1, d) for m in jnp.split(mod[:b], 6, axis=-1)]
        mods_s = [m.reshape(1, db, d) for m in jnp.split(mod[b:b + db], 6, axis=-1)]

        a, qb, kb, vb, rb, gd, ga, gb = _inproj(yp, mods_p[0], mods_p[1], map_p, tm_p, wts)
        a3 = a.reshape(b, t, 3 * W_A)
        o_g, l_g = [], []
        for gi, (window, dil) in enumerate(DIL_CONFIGS):
            o, lse = _dil_prompt(a3, gi, dil)
            o_g.append(o.reshape(b * t, W_G))
            l_g.append(lse.reshape(b * t, W_G))
            keep = min(window, t)
            k_last = a3[:, t - keep:, W_A + gi * W_G:W_A + (gi + 1) * W_G]
            v_last = a3[:, t - keep:, 2 * W_A + gi * W_G:2 * W_A + (gi + 1) * W_G]
            kv_p[gi].append(jnp.stack([k_last, v_last], axis=2).reshape(b, keep, 2, HEADS_PER_GROUP, HEAD_DIM_A))
        og, s_fin = _gla_prompt(qb.reshape(b, t, W_BK), kb.reshape(b, t, W_BK), vb.reshape(b, t, W_BV),
                                gd.reshape(b, t, W_BK))
        gla_p.append(s_fin)
        x1, h2, s1t, s2t = _merge(o_g, l_g, og.reshape(b * t, W_BV), rb, ga, gb, yp, mods_p[2], mods_p[3],
                                  mods_p[4], map_p, tm_p, wts, alpha)
        yp = _ffn_and_norm(x1, h2, s1t, s2t, mods_p[5], map_p, tm_p, wts, u_rows, v_rows, alpha)

        a, qb, kb, vb, rb, gd, ga, gb = _inproj(ys, mods_s[0], mods_s[1], map_s, tm_s, wts)
        layer_caches = [c[l] for c in caches]
        o, lse, new_caches = _dil_sample(a, layer_caches)
        for gi in range(N_GROUPS):
            kv_s[gi].append(new_caches[gi])
        og, s_new = _gla_sample(qb, kb, vb, gd, state_gla[l])
        gla_s.append(s_new)
        o_g = [o[:, gi * W_G:(gi + 1) * W_G] for gi in range(N_GROUPS)]
        l_g = [lse[:, gi * W_G:(gi + 1) * W_G] for gi in range(N_GROUPS)]
        x1, h2, s1t, s2t = _merge(o_g, l_g, og, rb, ga, gb, ys, mods_s[2], mods_s[3], mods_s[4], map_s, tm_s,
                                  wts, alpha)
        ys = _ffn_and_norm(x1, h2, s1t, s2t, mods_s[5], map_s, tm_s, wts, u_rows, v_rows, alpha)

    return (yp.reshape(b, t, d), ys.reshape(db, ds, d),
            jnp.stack(kv_p[0]), jnp.stack(kv_p[1]), jnp.stack(kv_p[2]), jnp.stack(gla_p),
            jnp.stack(kv_s[0]), jnp.stack(kv_s[1]), jnp.stack(kv_s[2]), jnp.stack(gla_s))
```

```python
import functools

import numpy as np
import jax
import jax.numpy as jnp
from jax import lax
from jax.experimental import pallas as pl
from jax.experimental.pallas import tpu as pltpu

DIL_CONFIGS = ((128, 1), (512, 4), (2048, 16))
N_GROUPS = 3
HEADS_PER_GROUP = 4
HEAD_DIM_A = 64
W_G = HEADS_PER_GROUP * HEAD_DIM_A
W_A = N_GROUPS * W_G
DIL_BLOCK = 128
N_HEADS_B = 4
HEAD_K_B = 128
HEAD_V_B = 256
W_BK = N_HEADS_B * HEAD_K_B
W_BV = N_HEADS_B * HEAD_V_B
GATE_RANK = 16
GATE_TEMP = 16.0
GLR_PAD = 128
GLA_CHUNK = 64
GLA_SUB = 16
PEER_HEADS = 8
PEER_KEYS = 128
PEER_TOPK = 16
TOPK_SHIFT = PEER_TOPK.bit_length() - 1
PEER_PAIRS = PEER_HEADS * PEER_TOPK
LN_EPS = 1e-5

LANES = 128
SUBLANES = 8
VREG_ELEMS = LANES * SUBLANES
VMEM_LIMIT = 56 * 1024 * 1024

PEER_SECTIONS = 2
PEER_GROUP = SUBLANES
PEER_SLOTS = PEER_PAIRS + PEER_SECTIONS * PEER_GROUP
GROUP_SHIFT = PEER_GROUP.bit_length() - 1
PEER_GPT = PEER_SLOTS // PEER_GROUP
PEER_SPARE_G = PEER_GPT - 1
PEER_GSTRIDE = 32
PEER_TB = LANES
PEER_QCAP = PEER_TB * (PEER_PAIRS // PEER_GROUP)
PEER_UNROLL = 16
PEER_UNROLL_V = 8

_HI = lax.Precision.HIGHEST
_NEG = float("-inf")


def _cparams(sem, vmem=VMEM_LIMIT):
    return pltpu.CompilerParams(dimension_semantics=sem, vmem_limit_bytes=vmem)


def _ln(x):
    mu = jnp.mean(x, axis=-1, keepdims=True)
    xc = x - mu
    var = jnp.mean(xc * xc, axis=-1, keepdims=True)
    return xc * lax.rsqrt(var + LN_EPS)


def _bdot(a, b):
    return jnp.dot(a.astype(jnp.bfloat16), b.astype(jnp.bfloat16), preferred_element_type=jnp.float32)


def _bdot_nt(a, b):
    return lax.dot_general(a.astype(jnp.bfloat16), b.astype(jnp.bfloat16), (((1,), (1,)), ((), ())),
                           preferred_element_type=jnp.float32)


def _alibi_slope(head):
    return float(np.exp2(np.float32(-8.0 * (head + 1) / (N_GROUPS * HEADS_PER_GROUP))))


def _ada_kernel(c_ref, w_ref, b_ref, o_ref):
    c = c_ref[...]
    o_ref[...] = _bdot(c * jax.nn.sigmoid(c), w_ref[...]) + b_ref[...]


def _adaln(c, w_ada, b_ada):
    bc, d = c.shape
    ncol = w_ada.shape[1] // d
    return pl.pallas_call(
        _ada_kernel,
        grid=(ncol,),
        in_specs=[pl.BlockSpec((bc, d), lambda j: (0, 0)),
                  pl.BlockSpec((d, d), lambda j: (0, j)),
                  pl.BlockSpec((1, d), lambda j: (0, j))],
        out_specs=pl.BlockSpec((bc, d), lambda j: (0, j)),
        out_shape=jax.ShapeDtypeStruct((bc, ncol * d), jnp.float32),
        compiler_params=_cparams(("arbitrary",)),
        name="adaln",
    )(c, w_ada, b_ada.reshape(1, -1))


def _inproj_kernel(x_ref, sh_ref, sc_ref, wa_ref, wqb_ref, wkb_ref, wvb_ref, wrb_ref, wglr_ref, wga_ref, wgb_ref,
                   wup_ref, bup_ref, a_ref, qb_ref, kb_ref, vb_ref, rb_ref, gd_ref, ga_ref, gb_ref):
    h = (_ln(x_ref[...]) * (1.0 + sc_ref[...]) + sh_ref[...]).astype(jnp.bfloat16)

    def proj(w_ref):
        return jnp.dot(h, w_ref[...], preferred_element_type=jnp.float32)

    a_ref[...] = proj(wa_ref)
    qb_ref[...] = proj(wqb_ref) * (HEAD_K_B ** -0.5)
    kb_ref[...] = proj(wkb_ref)
    vb_ref[...] = proj(wvb_ref)
    rb_ref[...] = proj(wrb_ref)
    ga_ref[...] = proj(wga_ref)
    gb_ref[...] = proj(wgb_ref)
    glr = proj(wglr_ref)
    gate = _bdot(glr, wup_ref[...]) + bup_ref[...]
    gd_ref[...] = jax.nn.log_sigmoid(gate) * (1.0 / GATE_TEMP)


def _inproj(x2d, shift, scale, mod_map, tm, wts):
    n, d = x2d.shape
    rm = shift.shape[1] if shift.shape[1] == 1 else tm
    mod_spec = pl.BlockSpec((None, rm, d), lambda i: mod_map(i) + (0,))
    row = lambda w: pl.BlockSpec((tm, w), lambda i: (i, 0))
    const = lambda a: pl.BlockSpec(a.shape, lambda i: (0,) * a.ndim, pipeline_mode=pl.Buffered(1))
    names = ("wa", "wqb", "wkb", "wvb", "wrb", "wglr", "wga", "wgb", "wup", "bup")
    widths = (3 * W_A, W_BK, W_BK, W_BV, W_BV, W_BK, d, d)
    return pl.pallas_call(
        _inproj_kernel,
        grid=(n // tm,),
        in_specs=[row(d), mod_spec, mod_spec] + [const(wts[k]) for k in names],
        out_specs=[row(w) for w in widths],
        out_shape=[jax.ShapeDtypeStruct((n, w), jnp.float32) for w in widths],
        compiler_params=_cparams(("arbitrary",)),
        name="inproj",
    )(x2d, shift, scale, *[wts[k] for k in names])


def _dil_prompt_kernel(q_ref, kc_ref, kp_ref, vc_ref, vp_ref, o_ref, l_ref, o_scr, l_scr, *, group, dil):
    has_prev = pl.program_id(2) > 0
    half = pl.program_id(1)
    heads = LANES // HEAD_DIM_A
    qi = lax.broadcasted_iota(jnp.int32, (DIL_BLOCK, DIL_BLOCK), 0)
    ki = lax.broadcasted_iota(jnp.int32, (DIL_BLOCK, DIL_BLOCK), 1)
    valid_p = jnp.logical_and(ki >= qi, has_prev)
    valid_c = ki <= qi
    dist_p = ((qi + DIL_BLOCK - ki) * dil).astype(jnp.float32)
    dist_c = ((qi - ki) * dil).astype(jnp.float32)

    def phase(r):
        rows = pl.ds(r, DIL_BLOCK, stride=dil) if dil > 1 else pl.ds(0, DIL_BLOCK)
        q_all, kc, kp, vc, vp = (ref[rows, :] for ref in (q_ref, kc_ref, kp_ref, vc_ref, vp_ref))
        outs, lses = [], []
        for hh in range(heads):
            slope = jnp.where(half == 0, _alibi_slope(group * HEADS_PER_GROUP + hh),
                              _alibi_slope(group * HEADS_PER_GROUP + heads + hh))
            sl = slice(hh * HEAD_DIM_A, (hh + 1) * HEAD_DIM_A)
            q = q_all[:, sl]
            sp = _bdot_nt(q, kp[:, sl]) * (HEAD_DIM_A ** -0.5) - slope * dist_p
            sc = _bdot_nt(q, kc[:, sl]) * (HEAD_DIM_A ** -0.5) - slope * dist_c
            sp = jnp.where(valid_p, sp, _NEG)
            sc = jnp.where(valid_c, sc, _NEG)
            m = jnp.maximum(jnp.max(sp, axis=-1, keepdims=True), jnp.max(sc, axis=-1, keepdims=True))
            pp = jnp.exp(sp - m)
            pc = jnp.exp(sc - m)
            z = jnp.sum(pp, axis=-1, keepdims=True) + jnp.sum(pc, axis=-1, keepdims=True)
            outs.append((_bdot(pp, vp[:, sl]) + _bdot(pc, vc[:, sl])) / z)
            lses.append(jnp.broadcast_to(m + jnp.log(z), (DIL_BLOCK, HEAD_DIM_A)))
        o_scr[r] = jnp.concatenate(outs, axis=-1)
        l_scr[r] = jnp.concatenate(lses, axis=-1)

    per_trip = 2 if dil > 1 else 1

    def trip(i, carry):
        for u in range(per_trip):
            phase(i * per_trip + u)
        return carry

    lax.fori_loop(0, dil // per_trip, trip, 0)
    for r in range(dil):
        rows = pl.ds(r, DIL_BLOCK, stride=dil) if dil > 1 else pl.ds(0, DIL_BLOCK)
        o_ref[rows, :] = o_scr[r]
        l_ref[rows, :] = l_scr[r]


def _dil_prompt(qkv, group, dil):
    b, t, wq = qkv.shape
    span = dil * DIL_BLOCK
    assert t % span == 0
    halves = W_G // LANES
    assert halves == 2 and HEADS_PER_GROUP * HEAD_DIM_A == W_G
    qcol, kcol, vcol = group, N_GROUPS + group, 2 * N_GROUPS + group
    blk = (None, span, LANES)
    cur = lambda col: pl.BlockSpec(blk, lambda bi, h, n: (bi, n, halves * col + h))
    prev = lambda col: pl.BlockSpec(blk, lambda bi, h, n: (bi, jnp.maximum(n - 1, 0), halves * col + h))
    ospec = pl.BlockSpec(blk, lambda bi, h, n: (bi, n, h))
    scr = pltpu.VMEM((dil, DIL_BLOCK, LANES), jnp.float32)
    return pl.pallas_call(
        functools.partial(_dil_prompt_kernel, group=group, dil=dil),
        grid=(b, halves, t // span),
        in_specs=[cur(qcol), cur(kcol), prev(kcol), cur(vcol), prev(vcol)],
        out_specs=[ospec, ospec],
        out_shape=[jax.ShapeDtypeStruct((b, t, W_G), jnp.float32)] * 2,
        scratch_shapes=[scr, scr],
        compiler_params=_cparams(("arbitrary", "arbitrary", "arbitrary")),
        name=f"dil_prompt_g{group}",
    )(qkv, qkv, qkv, qkv, qkv)


def _gla_prompt_kernel(q_ref, k_ref, v_ref, g_ref, o_ref, sfin_ref, s_scr):
    c = pl.program_id(1)
    nchunk = pl.num_programs(1)
    C = GLA_CHUNK

    @pl.when(c == 0)
    def _():
        s_scr[...] = jnp.zeros_like(s_scr)

    ri = lax.broadcasted_iota(jnp.int32, (C, C), 0)
    ci = lax.broadcasted_iota(jnp.int32, (C, C), 1)
    tri = (ri >= ci).astype(jnp.float32)
    bcum = jnp.dot(tri, g_ref[...], precision=_HI, preferred_element_type=jnp.float32)
    row16 = lax.broadcasted_iota(jnp.int32, (GLA_SUB, HEAD_K_B), 0)
    lane16 = lax.broadcasted_iota(jnp.int32, (GLA_SUB, LANES), 1)
    rowc = lax.broadcasted_iota(jnp.int32, (C, HEAD_K_B), 0)
    nsub = C // GLA_SUB
    outs = []
    for h in range(N_HEADS_B):
        ks = slice(h * HEAD_K_B, (h + 1) * HEAD_K_B)
        vs = slice(h * HEAD_V_B, (h + 1) * HEAD_V_B)
        bh = bcum[:, ks]
        qh = q_ref[:, ks]
        kh = k_ref[:, ks]
        vh = v_ref[:, vs]
        sh = s_scr[h]
        o_inter = _bdot(qh * jnp.exp(bh), sh)
        arows = []
        for i in range(nsub):
            r0 = i * GLA_SUB
            bi = bh[r0:r0 + GLA_SUB]
            qi_ = qh[r0:r0 + GLA_SUB]
            ki_ = kh[r0:r0 + GLA_SUB]
            a = jnp.zeros((GLA_SUB, LANES), jnp.float32)
            for s in range(GLA_SUB):
                e = jnp.exp(jnp.where(row16 >= s, bi - bi[s:s + 1], _NEG))
                col = jnp.sum(qi_ * (ki_[s:s + 1] * e), axis=-1, keepdims=True)
                a = jnp.where(lane16 == r0 + s, col, a)
            a = a[:, :C]
            if i > 0:
                b0 = bi[0:1]
                qt = qi_ * jnp.exp(bi - b0)
                kt = kh * jnp.exp(jnp.where(rowc < r0, b0 - bh, _NEG))
                a = a + _bdot_nt(qt, kt)
            arows.append(a)
        amat = jnp.concatenate(arows, axis=0)
        outs.append(o_inter + _bdot(amat, vh))
        bl = bh[C - 1:C]
        kt = kh * jnp.exp(bl - bh)
        dcol = jnp.transpose(jnp.broadcast_to(jnp.exp(bl), (SUBLANES, HEAD_K_B)))[:, 0:1]
        upd = lax.dot_general(kt.astype(jnp.bfloat16), vh.astype(jnp.bfloat16), (((0,), (0,)), ((), ())),
                              preferred_element_type=jnp.float32)
        s_scr[h] = dcol * sh + upd
    o_ref[...] = jnp.concatenate(outs, axis=-1)

    @pl.when(c == nchunk - 1)
    def _():
        sfin_ref[...] = s_scr[...]


def _gla_prompt(qb, kb, vb, gd):
    b, t, _ = qb.shape
    assert t % GLA_CHUNK == 0
    spec = lambda w: pl.BlockSpec((None, GLA_CHUNK, w), lambda bi, c: (bi, c, 0))
    sshape = (N_HEADS_B, HEAD_K_B, HEAD_V_B)
    return pl.pallas_call(
        _gla_prompt_kernel,
        grid=(b, t // GLA_CHUNK),
        in_specs=[spec(W_BK), spec(W_BK), spec(W_BV), spec(W_BK)],
        out_specs=[spec(W_BV), pl.BlockSpec((None,) + sshape, lambda bi, c: (bi, 0, 0, 0))],
        out_shape=[jax.ShapeDtypeStruct((b, t, W_BV), jnp.float32),
                   jax.ShapeDtypeStruct((b,) + sshape, jnp.float32)],
        scratch_shapes=[pltpu.VMEM(sshape, jnp.float32)],
        compiler_params=_cparams(("arbitrary", "arbitrary")),
        name="gla_prompt",
    )(qb, kb, vb, gd)


SAMPLE_SEQS = 8


def _to_col(row):
    return jnp.transpose(jnp.broadcast_to(row, (SUBLANES, row.shape[1])))[:, 0:1]


def _to_row(col):
    return jnp.transpose(jnp.broadcast_to(col, (col.shape[0], SUBLANES)))[0:1, :]


def _dil_sample_kernel(a_ref, c0_ref, c1_ref, c2_ref, o_ref, l_ref, n0_ref, n1_ref, n2_ref):
    col = _to_col(a_ref[0])
    scale = HEAD_DIM_A ** -0.5
    kv_w = 2 * W_G
    for g, (cref, nref, (window, dil)) in enumerate(zip((c0_ref, c1_ref, c2_ref), (n0_ref, n1_ref, n2_ref),
                                                        DIL_CONFIGS)):
        q = col[g * W_G:(g + 1) * W_G]
        knew = col[W_A + g * W_G:W_A + (g + 1) * W_G]
        vnew = col[2 * W_A + g * W_G:2 * W_A + (g + 1) * W_G]
        x = cref[0]
        lane = lax.broadcasted_iota(jnp.int32, (1, window), 1)
        on_stride = (lane & (dil - 1)) == 0
        dist = (window - lane).astype(jnp.float32)
        o_cols, l_cols = [], []
        for hh in range(HEADS_PER_GROUP):
            rows = slice(hh * HEAD_DIM_A, (hh + 1) * HEAD_DIM_A)
            qh = q[rows]
            s = jnp.sum(x[rows, :] * qh, axis=0, keepdims=True) * scale
            s = s - _alibi_slope(g * HEADS_PER_GROUP + hh) * dist
            s = jnp.where(on_stride, s, _NEG)
            ss = jnp.sum(knew[rows] * qh, axis=0, keepdims=True) * scale
            m = jnp.maximum(jnp.max(s, axis=1, keepdims=True), ss)
            p = jnp.exp(s - m)
            ps = jnp.exp(ss - m)
            z = jnp.sum(p, axis=1, keepdims=True) + ps
            vrows = slice(W_G + hh * HEAD_DIM_A, W_G + (hh + 1) * HEAD_DIM_A)
            o_cols.append((jnp.sum(x[vrows, :] * p, axis=1, keepdims=True) + ps * vnew[rows]) / z)
            l_cols.append(jnp.broadcast_to(m + jnp.log(z), (HEAD_DIM_A, 1)))
        o_ref[0, :, g * W_G:(g + 1) * W_G] = _to_row(jnp.concatenate(o_cols, axis=0))
        l_ref[0, :, g * W_G:(g + 1) * W_G] = _to_row(jnp.concatenate(l_cols, axis=0))
        shifted = pltpu.roll(x, window - 1, axis=1)
        nref[0] = jnp.where(lane == window - 1, jnp.concatenate([knew, vnew], axis=0), shifted)


def _dil_sample(qkv, caches):
    db = qkv.shape[0]
    kv_w = 2 * W_G
    views, cspecs, oshapes = [], [], []
    for cache, (window, dil) in zip(caches, DIL_CONFIGS):
        assert cache.shape[1] == window, "window caches shorter than the window are not supported"
        assert dil & (dil - 1) == 0
        views.append(jnp.transpose(cache, (0, 2, 3, 4, 1)).reshape(db, kv_w, window))
        cspecs.append(pl.BlockSpec((1, kv_w, window), lambda i: (i, 0, 0)))
        oshapes.append(jax.ShapeDtypeStruct((db, kv_w, window), jnp.float32))
    rspec = lambda w: pl.BlockSpec((1, 1, w), lambda i: (i, 0, 0))
    o, lse, *new = pl.pallas_call(
        _dil_sample_kernel,
        grid=(db,),
        in_specs=[rspec(3 * W_A)] + cspecs,
        out_specs=[rspec(W_A), rspec(W_A)] + cspecs,
        out_shape=[jax.ShapeDtypeStruct((db, 1, W_A), jnp.float32)] * 2 + oshapes,
        compiler_params=_cparams(("arbitrary",)),
        name="dil_sample",
    )(qkv.reshape(db, 1, 3 * W_A), *views)
    new = [jnp.transpose(n.reshape(db, 2, HEADS_PER_GROUP, HEAD_DIM_A, n.shape[-1]), (0, 4, 1, 2, 3)) for n in new]
    return o.reshape(db, W_A), lse.reshape(db, W_A), new


def _gla_sample_kernel(q_ref, k_ref, v_ref, g_ref, s0_ref, o_ref, s_ref):
    for h in range(N_HEADS_B):
        ks = slice(h * HEAD_K_B, (h + 1) * HEAD_K_B)
        vs = slice(h * HEAD_V_B, (h + 1) * HEAD_V_B)
        qT = jnp.transpose(q_ref[:, ks])
        kT = jnp.transpose(k_ref[:, ks])
        aT = jnp.transpose(jnp.exp(g_ref[:, ks]))
        for j in range(SAMPLE_SEQS):
            s_new = aT[:, j:j + 1] * s0_ref[j, h] + kT[:, j:j + 1] * v_ref[j:j + 1, vs]
            s_ref[j, h] = s_new
            o_ref[j:j + 1, vs] = jnp.sum(qT[:, j:j + 1] * s_new, axis=0, keepdims=True)


def _gla_sample(qb, kb, vb, gd, s0):
    db = qb.shape[0]
    row = lambda w: pl.BlockSpec((SAMPLE_SEQS, w), lambda i: (i, 0))
    sspec = pl.BlockSpec((SAMPLE_SEQS, N_HEADS_B, HEAD_K_B, HEAD_V_B), lambda i: (i, 0, 0, 0))
    return pl.pallas_call(
        _gla_sample_kernel,
        grid=(db // SAMPLE_SEQS,),
        in_specs=[row(W_BK), row(W_BK), row(W_BV), row(W_BK), sspec],
        out_specs=[row(W_BV), sspec],
        out_shape=[jax.ShapeDtypeStruct((db, W_BV), jnp.float32), jax.ShapeDtypeStruct(s0.shape, jnp.float32)],
        compiler_params=_cparams(("arbitrary",)),
        name="gla_sample",
    )(qb, kb, vb, gd, s0)


def _merge_kernel(o0_ref, o1_ref, o2_ref, l0_ref, l1_ref, l2_ref, og_ref, rb_ref, ga_ref, gb_ref, x_ref,
                  g1_ref, sh2_ref, sc2_ref, gnw_ref, wbra_ref, wbrb_ref, wout_ref, ln1w_ref, ln1b_ref,
                  wpq_ref, k1_ref, k2_ref, x1_ref, h2_ref, s1_ref, s2_ref, *, alpha):
    l0, l1, l2 = l0_ref[...], l1_ref[...], l2_ref[...]
    m = jnp.maximum(jnp.maximum(l0, l1), l2)
    e0, e1, e2 = jnp.exp(l0 - m), jnp.exp(l1 - m), jnp.exp(l2 - m)
    oa = (e0 * o0_ref[...] + e1 * o1_ref[...] + e2 * o2_ref[...]) / (e0 + e1 + e2)
    og = og_ref[...]
    parts = []
    for h in range(N_HEADS_B):
        oh = og[:, h * HEAD_V_B:(h + 1) * HEAD_V_B]
        parts.append(oh * lax.rsqrt(jnp.mean(oh * oh, axis=-1, keepdims=True) + LN_EPS))
    rb = rb_ref[...]
    ob = jnp.concatenate(parts, axis=-1) * gnw_ref[...] * (rb * jax.nn.sigmoid(rb))
    merged = (jax.nn.sigmoid(ga_ref[...]) * _bdot(oa, wbra_ref[...])
              + jax.nn.sigmoid(gb_ref[...]) * _bdot(ob, wbrb_ref[...]))
    mix = _bdot(merged, wout_ref[...])
    x1 = _ln(alpha * x_ref[...] + g1_ref[...] * mix) * ln1w_ref[...] + ln1b_ref[...]
    x1_ref[...] = x1
    h2 = _ln(x1) * (1.0 + sc2_ref[...]) + sh2_ref[...]
    h2_ref[...] = h2
    qv = _bdot(h2, wpq_ref[...]).astype(jnp.bfloat16)
    half = PEER_KEYS
    for h in range(PEER_HEADS):
        base = h * 2 * half
        s1_ref[h] = lax.dot_general(k1_ref[h], qv[:, base:base + half], (((1,), (1,)), ((), ())),
                                    preferred_element_type=jnp.float32)
        s2_ref[h] = lax.dot_general(k2_ref[h], qv[:, base + half:base + 2 * half], (((1,), (1,)), ((), ())),
                                    preferred_element_type=jnp.float32)


def _merge(o_g, l_g, og, rb, ga, gb, x2d, g1, sh2, sc2, mod_map, tm, wts, alpha):
    n, d = x2d.shape
    rm = g1.shape[1] if g1.shape[1] == 1 else tm
    mod_spec = pl.BlockSpec((None, rm, d), lambda i: mod_map(i) + (0,))
    row = lambda w: pl.BlockSpec((tm, w), lambda i: (i, 0))
    const = lambda a: pl.BlockSpec(a.shape, lambda i: (0,) * a.ndim, pipeline_mode=pl.Buffered(1))
    names = ("gnw", "wbra", "wbrb", "wout", "ln1w", "ln1b", "wpq", "k1", "k2")
    sspec = pl.BlockSpec((PEER_HEADS, PEER_KEYS, tm), lambda i: (0, 0, i))
    return pl.pallas_call(
        functools.partial(_merge_kernel, alpha=alpha),
        grid=(n // tm,),
        in_specs=[row(W_G)] * 6 + [row(W_BV), row(W_BV), row(d), row(d), row(d), mod_spec, mod_spec, mod_spec]
                 + [const(wts[k]) for k in names],
        out_specs=[row(d), row(d), sspec, sspec],
        out_shape=[jax.ShapeDtypeStruct((n, d), jnp.float32)] * 2
                  + [jax.ShapeDtypeStruct((PEER_HEADS, PEER_KEYS, n), jnp.float32)] * 2,
        compiler_params=_cparams(("arbitrary",)),
        name="merge",
    )(*o_g, *l_g, og, rb, ga, gb, x2d, g1, sh2, sc2, *[wts[k] for k in names])


_CAND_IDS = tuple(i * PEER_TOPK + j for i in range(PEER_TOPK) for j in range(PEER_TOPK)
                  if (i + 1) * (j + 1) <= PEER_TOPK)
_NO_ID = PEER_KEYS * PEER_KEYS
TOPK_CHUNK = 16


def _topk_kernel(s1_ref, s2_ref, e_ref, w_ref, sa, sb, va, vb, ia, ib, cand, sc_scr, ci_scr):
    sub = s1_ref.shape[2]
    shape = (sub, LANES)
    sa[...] = s1_ref[0]
    sb[...] = s2_ref[0]

    def tree(op, xs):
        xs = list(xs)
        while len(xs) > 1:
            xs = [op(xs[k], xs[k + 1]) for k in range(0, len(xs) - 1, 2)] + (xs[-1:] if len(xs) % 2 else [])
        return xs[0]

    def extract(s_scr, ids, r, v_out, i_out):
        n = len(ids)
        chunks = [range(c, min(c + TOPK_CHUNK, n)) for c in range(0, n, TOPK_CHUNK)]
        m = tree(jnp.maximum, [tree(jnp.maximum, [s_scr[k] for k in ch]) for ch in chunks])
        idx = tree(jnp.minimum, [tree(jnp.minimum, [jnp.where(s_scr[k] == m, ids[k], _NO_ID) for k in ch])
                                 for ch in chunks])
        for k in range(n):
            s_scr[k] = jnp.where(idx == ids[k], _NEG, s_scr[k])
        v_out[r] = m
        i_out[r] = idx

    def stage1(r, c):
        extract(sa, range(PEER_KEYS), r, va, ia)
        extract(sb, range(PEER_KEYS), r, vb, ib)
        return c

    lax.fori_loop(0, PEER_TOPK, stage1, 0)
    for k, ci in enumerate(_CAND_IDS):
        cand[k] = va[ci // PEER_TOPK] + vb[ci % PEER_TOPK]

    def stage2(r, c):
        extract(cand, _CAND_IDS, r, sc_scr, ci_scr)
        return c

    lax.fori_loop(0, PEER_TOPK, stage2, 0)
    top = sc_scr[0]
    z = jnp.zeros(shape, jnp.float32)
    for r in range(PEER_TOPK):
        z = z + jnp.exp(sc_scr[r] - top)
    for r in range(PEER_TOPK):
        w_ref[0, 0, r] = jnp.exp(sc_scr[r] - top) / z
        ci = ci_scr[r]
        hi = lax.shift_right_logical(ci, TOPK_SHIFT)
        lo = ci & (PEER_TOPK - 1)
        e1 = jnp.zeros(shape, jnp.int32)
        e2 = jnp.zeros(shape, jnp.int32)
        for i in range(PEER_TOPK):
            e1 = jnp.where(hi == i, ia[i], e1)
            e2 = jnp.where(lo == i, ib[i], e2)
        e_ref[0, 0, r] = e1 * PEER_KEYS + e2


def _topk(s1t, s2t):
    nh, nk, n = s1t.shape
    sub = min(SUBLANES, n // LANES)
    nchunk = n // (sub * LANES)
    v1 = s1t.reshape(nh, nk, n // LANES, LANES)
    v2 = s2t.reshape(nh, nk, n // LANES, LANES)
    ispec = pl.BlockSpec((1, nk, sub, LANES), lambda c, h: (h, 0, c, 0))
    ospec = pl.BlockSpec((1, 1, PEER_TOPK, sub, LANES), lambda c, h: (c, h, 0, 0, 0))
    oshape = (nchunk, nh, PEER_TOPK, sub, LANES)
    key = lambda k, dt: pltpu.VMEM((k, sub, LANES), dt)
    e, w = pl.pallas_call(
        _topk_kernel,
        grid=(nchunk, nh),
        in_specs=[ispec, ispec],
        out_specs=[ospec, ospec],
        out_shape=[jax.ShapeDtypeStruct(oshape, jnp.int32), jax.ShapeDtypeStruct(oshape, jnp.float32)],
        scratch_shapes=[key(nk, jnp.float32), key(nk, jnp.float32),
                        key(PEER_TOPK, jnp.float32), key(PEER_TOPK, jnp.float32),
                        key(PEER_TOPK, jnp.int32), key(PEER_TOPK, jnp.int32),
                        key(len(_CAND_IDS), jnp.float32), key(PEER_TOPK, jnp.float32),
                        key(PEER_TOPK, jnp.int32)],
        compiler_params=_cparams(("arbitrary", "arbitrary")),
        name="peer_topk",
    )(v1, v2)
    return e, w


def _load_section(tab_hbm, tab_vmem, sem):
    rows = tab_vmem.shape[0]

    @pl.when(pl.program_id(1) == 0)
    def _():
        start = pl.multiple_of(pl.program_id(0) * rows, SUBLANES)
        cp = pltpu.make_async_copy(tab_hbm.at[pl.ds(start, rows)], tab_vmem, sem)
        cp.start()
        cp.wait()


def _grid_step():
    return pl.program_id(0) * pl.num_programs(1) + pl.program_id(1)


def _peer_u_kernel(nq_ref, glo_ref, ghi_ref, xrow_ref, gt_ref, *refs):
    idx_refs = refs[:PEER_GROUP]
    x_ref, gw_ref, tab_hbm, act_ref, tab_vmem, dbuf, xs, sem = refs[PEER_GROUP:]
    _load_section(tab_hbm, tab_vmem, sem)
    step = _grid_step()

    @pl.when(step == 0)
    def _():
        dbuf[...] = jnp.zeros_like(dbuf)

    for k in range(SUBLANES):
        xs[pl.ds(k, PEER_TB, stride=SUBLANES), :] = x_ref[:, k * LANES:(k + 1) * LANES]

    nq = nq_ref[step]
    sub = lax.broadcasted_iota(jnp.int32, (SUBLANES, LANES), 0)
    order = _fold_slot_order()

    def fold(a, b, sh):
        keep = (sub & sh) == 0
        u = jnp.where(keep, a, b)
        v = jnp.where(keep, b, a)
        if 2 * sh == SUBLANES:
            w = pltpu.roll(v, sh, axis=0)
        else:
            w = jnp.where(keep, pltpu.roll(v, SUBLANES - sh, axis=0), pltpu.roll(v, sh, axis=0))
        return u + w

    def body(i, c):
        for u in range(PEER_UNROLL):
            q = i * PEER_UNROLL + u
            gt = gt_ref[q]
            x = xs[pl.ds(pl.multiple_of(xrow_ref[q], SUBLANES), SUBLANES), :]
            ps = [tab_vmem[pl.ds(pl.multiple_of(idx_refs[order[j]][gt], SUBLANES), SUBLANES), :] * x
                  for j in range(PEER_GROUP)]
            while len(ps) > 1:
                sh = len(ps) // 2
                ps = [fold(ps[2 * k], ps[2 * k + 1], sh) for k in range(sh)]
            dbuf[pl.ds(pl.multiple_of(gt * PEER_GROUP, SUBLANES), SUBLANES), :] = ps[0]
        return c

    lax.fori_loop(0, nq // PEER_UNROLL, body, 0)
    rows_per_g = PEER_GROUP * PEER_TB
    glo, ghi = glo_ref[step], ghi_ref[step]

    def zfill(g, carry):
        act_ref[0, pl.ds(pl.multiple_of(g * PEER_GROUP, PEER_GROUP), PEER_GROUP), :] = jnp.zeros(
            (PEER_GROUP, PEER_TB), jnp.float32)
        return carry

    def reduce(g, carry):
        for j in range(PEER_GROUP):
            rows = dbuf[pl.ds(g * rows_per_g + j, PEER_TB, stride=PEER_GROUP), :]
            act = jnp.sum(jnp.transpose(rows), axis=0, keepdims=True)
            slot = pl.ds(g * PEER_GROUP + j, 1)
            act_ref[0, slot, :] = gw_ref[0, slot, :] * (0.5 * act * (1.0 + lax.erf(act * (2.0 ** -0.5))))
        return carry

    lax.fori_loop(0, glo, zfill, 0)
    lax.fori_loop(glo, ghi, reduce, 0)
    lax.fori_loop(ghi, PEER_GPT, zfill, 0)


def _fold_slot_order():
    pos = [[j] for j in range(PEER_GROUP)]
    sl = [0] * PEER_GROUP
    sh = PEER_GROUP // 2
    groups = pos
    while len(groups) > 1:
        nxt = []
        for i in range(len(groups) // 2):
            for j in groups[2 * i + 1]:
                sl[j] |= sh
            nxt.append(groups[2 * i] + groups[2 * i + 1])
        groups = nxt
        sh //= 2
    return sl


def _peer_v_kernel(nq_ref, glo_ref, ghi_ref, xrow_ref, gt_ref, wrow_ref, *refs):
    idx_refs = refs[:PEER_GROUP]
    w_ref, tab_hbm, out_ref, tab_vmem, wb, acc_scr, sem = refs[PEER_GROUP:]
    _load_section(tab_hbm, tab_vmem, sem)
    step = _grid_step()
    nq = nq_ref[step]
    rows_per_g = PEER_GROUP * PEER_TB

    @pl.when(step == 0)
    def _():
        wb[pl.ds(PEER_SPARE_G * rows_per_g, rows_per_g), :] = jnp.zeros((rows_per_g, LANES), jnp.float32)

    acc_scr[...] = jnp.zeros_like(acc_scr)

    def spread(g, carry):
        for j in range(PEER_GROUP):
            r = g * PEER_GROUP + j
            rep = jnp.broadcast_to(w_ref[0, pl.ds(r, 1), :], (LANES, LANES))
            wb[pl.ds(pl.multiple_of(r * PEER_TB, PEER_TB), PEER_TB), :] = jnp.transpose(rep)
        return carry

    lax.fori_loop(glo_ref[step], ghi_ref[step], spread, 0)
    trash = PEER_TB * SUBLANES

    def body(i, carry):
        cur, acc = carry
        for u in range(PEER_UNROLL_V):
            q = i * PEER_UNROLL_V + u
            gt = gt_ref[q]
            row0 = wrow_ref[q]
            terms = [tab_vmem[pl.ds(pl.multiple_of(idx_refs[j][gt], SUBLANES), SUBLANES), :]
                     * wb[pl.ds(row0 + j * PEER_TB, 1), :] for j in range(PEER_GROUP)]
            while len(terms) > 1:
                terms = [terms[2 * k] + terms[2 * k + 1] for k in range(len(terms) // 2)]
            row = xrow_ref[q]
            acc_scr[pl.ds(pl.multiple_of(cur, SUBLANES), SUBLANES), :] = acc
            acc = jnp.where(row == cur, acc + terms[0], terms[0])
            cur = row
        return cur, acc

    cur, acc = lax.fori_loop(0, nq // PEER_UNROLL_V, body,
                             (jnp.int32(trash), jnp.zeros((SUBLANES, LANES), jnp.float32)))
    acc_scr[pl.ds(pl.multiple_of(cur, SUBLANES), SUBLANES), :] = acc
    for k in range(SUBLANES):
        out_ref[0, :, k * LANES:(k + 1) * LANES] = acc_scr[pl.ds(k, PEER_TB, stride=SUBLANES), :]


def _router_kernel(e_ref, w_ref, *refs, sec_experts):
    idx_refs = refs[:PEER_GROUP]
    gw_ref, g0_ref, g01_ref, pos_scr, loc_scr = refs[PEER_GROUP:]
    shape = e_ref.shape[3:]
    sub = shape[0]
    zero = jnp.zeros(shape, jnp.int32)
    r0, r1 = zero, zero
    for p in range(PEER_PAIRS):
        e = e_ref[0, p // PEER_TOPK, p % PEER_TOPK]
        upper = e >= sec_experts
        loc_scr[p] = (e & (sec_experts - 1)) * SUBLANES
        pos_scr[p] = jnp.where(upper, r1 + PEER_SLOTS, r0)
        r0 = r0 + jnp.where(upper, 0, 1)
        r1 = r1 + jnp.where(upper, 1, 0)
    g0 = lax.shift_right_logical(r0 + (PEER_GROUP - 1), GROUP_SHIFT)
    g1 = lax.shift_right_logical(r1 + (PEER_GROUP - 1), GROUP_SHIFT)
    g0_ref[0] = g0
    g01_ref[0] = g0 + g1
    rebase = g0 * PEER_GROUP - PEER_SLOTS
    for p in range(PEER_PAIRS):
        pos = pos_scr[p]
        pos_scr[p] = jnp.where(pos >= PEER_SLOTS, pos + rebase, pos)

    def place(g, carry):
        for j in range(PEER_GROUP):
            slot = g * PEER_GROUP + j
            iv = zero
            wv = jnp.zeros(shape, jnp.float32)
            for p in range(PEER_PAIRS):
                hit = pos_scr[p] == slot
                iv = jnp.where(hit, loc_scr[p], iv)
                wv = jnp.where(hit, w_ref[0, p // PEER_TOPK, p % PEER_TOPK], wv)
            for s in range(sub):
                idx_refs[j][0, s, pl.ds(g, 1), :] = iv[s:s + 1, :]
                gw_ref[0, s, pl.ds(slot, 1), :] = wv[s:s + 1, :]
        return carry

    lax.fori_loop(0, PEER_GPT, place, 0)
    for j in range(PEER_GROUP):
        idx_refs[j][0, :, PEER_GPT:, :] = jnp.zeros((sub, PEER_GSTRIDE - PEER_GPT, LANES), jnp.int32)


def _router(e, w, n_experts):
    nchunk, nh, k, sub, _ = e.shape
    sec_experts = n_experts // PEER_SECTIONS
    assert sec_experts & (sec_experts - 1) == 0 and PEER_SECTIONS == 2
    ispec = pl.BlockSpec((1, nh, k, sub, LANES), lambda c: (c, 0, 0, 0, 0))
    bspec = pl.BlockSpec((1, sub, LANES), lambda c: (c, 0, 0))
    scr = lambda dt: pltpu.VMEM((PEER_PAIRS, sub, LANES), dt)
    blk4 = lambda rows: pl.BlockSpec((1, sub, rows, LANES), lambda c: (c, 0, 0, 0))
    *idx, gw, g0, g01 = pl.pallas_call(
        functools.partial(_router_kernel, sec_experts=sec_experts),
        grid=(nchunk,),
        in_specs=[ispec, ispec],
        out_specs=[blk4(PEER_GSTRIDE)] * PEER_GROUP + [blk4(PEER_SLOTS), bspec, bspec],
        out_shape=[jax.ShapeDtypeStruct((nchunk, sub, PEER_GSTRIDE, LANES), jnp.int32)] * PEER_GROUP
                  + [jax.ShapeDtypeStruct((nchunk, sub, PEER_SLOTS, LANES), jnp.float32),
                     jax.ShapeDtypeStruct((nchunk, sub, LANES), jnp.int32),
                     jax.ShapeDtypeStruct((nchunk, sub, LANES), jnp.int32)],
        scratch_shapes=[scr(jnp.int32), scr(jnp.int32)],
        compiler_params=_cparams(("arbitrary",)),
        name="peer_router",
    )(e, w)
    return idx, gw, g0, g01


def _peer_lists(g_lo, g_hi):
    nblk = g_lo.shape[0]
    cnt = g_hi - g_lo
    end = jnp.cumsum(cnt, axis=1)
    start = end - cnt
    q = jnp.arange(PEER_QCAP, dtype=jnp.int32)[None, :, None]
    inside = jnp.logical_and(start[:, None, :] <= q, q < end[:, None, :])
    tok = jnp.arange(PEER_TB, dtype=jnp.int32)[None, None, :]
    t = jnp.sum(jnp.where(inside, tok, 0), axis=2)
    g = q[:, :, 0] + jnp.sum(jnp.where(inside, (g_lo - start)[:, None, :], 0), axis=2)
    count = end[:, -1:]
    live = q[:, :, 0] < count
    xrow = jnp.where(live, t * SUBLANES, 0)
    xrow = jnp.where(live, xrow, jnp.max(xrow, axis=1, keepdims=True))
    gt = jnp.where(live, g * PEER_TB + t, PEER_SPARE_G * PEER_TB)
    nq = (count[:, 0] + PEER_UNROLL - 1) // PEER_UNROLL * PEER_UNROLL
    return nq, jnp.min(g_lo, axis=1), jnp.max(g_hi, axis=1), xrow, gt


def _peer(h2, e, gw, u_rows, v_rows):
    n, d = h2.shape
    assert d == VREG_ELEMS and n % PEER_TB == 0 and PEER_TB == LANES
    n_experts = u_rows.shape[0] // SUBLANES
    sec_rows = u_rows.shape[0] // PEER_SECTIONS
    nblk = n // PEER_TB
    nstep = PEER_SECTIONS * nblk
    idx4, gw4, g0, g01 = _router(e, gw, n_experts)
    idx_by_slot = [a.reshape(-1) for a in idx4]
    rows_per_g = PEER_GROUP * PEER_TB
    gw_blk = gw4.reshape(nblk, PEER_SLOTS, PEER_TB)
    g0 = g0.reshape(nblk, PEER_TB)
    g01 = g01.reshape(nblk, PEER_TB)
    bounds = (jnp.zeros_like(g0), g0, g01)
    lists = [_peer_lists(bounds[s], bounds[s + 1]) for s in range(PEER_SECTIONS)]
    nq, glo, ghi, xrow, gt = [jnp.stack(a).reshape(-1) for a in zip(*lists)]
    wrow = gt + (gt // PEER_TB) * ((PEER_GROUP - 1) * PEER_TB)

    npre = 3
    lst = pl.BlockSpec((PEER_QCAP,), lambda s, i, *_: (s * nblk + i,), memory_space=pltpu.SMEM)
    slots = [pl.BlockSpec((PEER_TB * PEER_GSTRIDE,), lambda s, i, *_: (i,), memory_space=pltpu.SMEM)
             ] * PEER_GROUP
    table = pl.BlockSpec(memory_space=pl.ANY)
    tab_scratch = pltpu.VMEM((sec_rows, LANES), jnp.float32)
    slot_rows = PEER_GPT * rows_per_g
    w = pl.pallas_call(
        _peer_u_kernel,
        grid_spec=pltpu.PrefetchScalarGridSpec(
            num_scalar_prefetch=npre,
            grid=(PEER_SECTIONS, nblk),
            in_specs=[lst, lst] + slots + [pl.BlockSpec((PEER_TB, d), lambda s, i, *_: (i, 0)),
                                           pl.BlockSpec((1, PEER_SLOTS, PEER_TB), lambda s, i, *_: (i, 0, 0)), table],
            out_specs=pl.BlockSpec((1, PEER_SLOTS, PEER_TB), lambda s, i, *_: (s * nblk + i, 0, 0)),
            scratch_shapes=[tab_scratch, pltpu.VMEM((slot_rows, LANES), jnp.float32),
                            pltpu.VMEM((PEER_TB * SUBLANES, LANES), jnp.float32), pltpu.SemaphoreType.DMA]),
        out_shape=jax.ShapeDtypeStruct((nstep, PEER_SLOTS, PEER_TB), jnp.float32),
        compiler_params=_cparams(("arbitrary", "arbitrary")),
        name="peer_u",
    )(nq, glo, ghi, xrow, gt, *idx_by_slot, h2, gw_blk, u_rows)
    parts = pl.pallas_call(
        _peer_v_kernel,
        grid_spec=pltpu.PrefetchScalarGridSpec(
            num_scalar_prefetch=npre,
            grid=(PEER_SECTIONS, nblk),
            in_specs=[lst, lst, lst] + slots + [
                pl.BlockSpec((1, PEER_SLOTS, LANES), lambda s, i, *_: (s * nblk + i, 0, 0)), table],
            out_specs=pl.BlockSpec((1, PEER_TB, d), lambda s, i, *_: (s, i, 0)),
            scratch_shapes=[tab_scratch, pltpu.VMEM((slot_rows, LANES), jnp.float32),
                            pltpu.VMEM((PEER_TB * SUBLANES + SUBLANES, LANES), jnp.float32),
                            pltpu.SemaphoreType.DMA]),
        out_shape=jax.ShapeDtypeStruct((PEER_SECTIONS, n, d), jnp.float32),
        compiler_params=_cparams(("arbitrary", "arbitrary")),
        name="peer_v",
    )(nq, glo, ghi, xrow, gt, wrow, *idx_by_slot, w, v_rows)
    return parts


def _final_kernel(x1_ref, ff_ref, g2_ref, w_ref, b_ref, o_ref, *, alpha):
    ff = ff_ref[0] + ff_ref[1]
    o_ref[...] = _ln(alpha * x1_ref[...] + g2_ref[...] * ff) * w_ref[...] + b_ref[...]


def _final(x1, parts, g2, mod_map, tm, ln2w, ln2b, alpha):
    n, d = x1.shape
    rm = g2.shape[1] if g2.shape[1] == 1 else tm
    vec = pl.BlockSpec((1, d), lambda i: (0, 0))
    return pl.pallas_call(
        functools.partial(_final_kernel, alpha=alpha),
        grid=(n // tm,),
        in_specs=[pl.BlockSpec((tm, d), lambda i: (i, 0)),
                  pl.BlockSpec((PEER_SECTIONS, tm, d), lambda i: (0, i, 0)),
                  pl.BlockSpec((None, rm, d), lambda i: mod_map(i) + (0,)), vec, vec],
        out_specs=pl.BlockSpec((tm, d), lambda i: (i, 0)),
        out_shape=jax.ShapeDtypeStruct((n, d), jnp.float32),
        compiler_params=_cparams(("arbitrary",)),
        name="final_ln",
    )(x1, parts, g2, ln2w, ln2b)


def _layer_weights(l, w_in, w_gla_up, b_gla, gla_norm_w, w_br_a, w_br_b, w_out, ln1_w, ln1_b, w_pq,
                   peer_k1, peer_k2, ln2_w, ln2_b):
    d = w_in.shape[1]
    bf = lambda a: a.astype(jnp.bfloat16)
    sizes = (W_A, W_A, W_A, W_BK, W_BK, W_BV, W_BV, GATE_RANK, d, d)
    offs = np.concatenate([[0], np.cumsum(sizes)])
    col = lambda i, j=None: w_in[l][:, offs[i]:offs[(i if j is None else j) + 1]]
    pad_rank = GLR_PAD - GATE_RANK
    return {
        "wa": bf(col(0, 2)), "wqb": bf(col(3)), "wkb": bf(col(4)), "wvb": bf(col(5)), "wrb": bf(col(6)),
        "wglr": bf(jnp.pad(col(7), ((0, 0), (0, pad_rank)))), "wga": bf(col(8)), "wgb": bf(col(9)),
        "wup": bf(jnp.pad(w_gla_up[l], ((0, pad_rank), (0, 0)))), "bup": b_gla[l].reshape(1, -1),
        "gnw": gla_norm_w[l].reshape(1, -1), "wbra": bf(w_br_a[l]), "wbrb": bf(w_br_b[l]), "wout": bf(w_out[l]),
        "ln1w": ln1_w[l].reshape(1, -1), "ln1b": ln1_b[l].reshape(1, -1), "wpq": bf(w_pq[l]),
        "k1": bf(peer_k1[l]), "k2": bf(peer_k2[l]),
        "ln2w": ln2_w[l].reshape(1, -1), "ln2b": ln2_b[l].reshape(1, -1),
    }


def _ffn_and_norm(x1, h2, s1t, s2t, g2, mod_map, tm, wts, u_rows, v_rows, alpha):
    e, gw = _topk(s1t, s2t)
    parts = _peer(h2, e, gw, u_rows, v_rows)
    return _final(x1, parts, g2, mod_map, tm, wts["ln2w"], wts["ln2b"], alpha)


def kernel(x_prompt, x_sample, c_prompt, c_sample, cache_kv_w128, cache_kv_w512, cache_kv_w2048, state_gla, w_ada, b_ada, w_in, w_gla_up, b_gla, gla_norm_w, w_br_a, w_br_b, w_out, ln1_w, ln1_b, w_pq, peer_k1, peer_k2, peer_u, peer_v, ln2_w, ln2_b):
    depth = w_ada.shape[0]
    b, t, d = x_prompt.shape
    db, ds, _ = x_sample.shape
    assert ds == 1, "the single-token kernels take one new token per sequence"
    alpha = (2 * depth) ** 0.25
    tm_p = 256
    tm_s = db
    yp = x_prompt.reshape(b * t, d)
    ys = x_sample.reshape(db, d)
    caches = (cache_kv_w128, cache_kv_w512, cache_kv_w2048)
    kv_p = [[] for _ in DIL_CONFIGS]
    kv_s = [[] for _ in DIL_CONFIGS]
    gla_p, gla_s = [], []
    nc = b + db
    nc_pad = -(-nc // SUBLANES) * SUBLANES
    c_all = jnp.pad(jnp.concatenate([c_prompt, c_sample], axis=0), ((0, nc_pad - nc), (0, 0)))
    map_p = lambda i: (i // (t // tm_p), 0)
    map_s = lambda i: (0, i)
    for l in range(depth):
        wts = _layer_weights(l, w_in, w_gla_up, b_gla, gla_norm_w, w_br_a, w_br_b, w_out, ln1_w, ln1_b, w_pq,
                             peer_k1, peer_k2, ln2_w, ln2_b)
        u_rows = peer_u[l].reshape(-1, LANES)
        v_rows = peer_v[l].reshape(-1, LANES)
        mod = _adaln(c_all, w_ada[l], b_ada[l])
        mods_p = [m.reshape(b, 1, d) for m in jnp.split(mod[:b], 6, axis=-1)]
        mods_s = [m.reshape(1, db, d) for m in jnp.split(mod[b:b + db], 6, axis=-1)]

        a, qb, kb, vb, rb, gd, ga, gb = _inproj(yp, mods_p[0], mods_p[1], map_p, tm_p, wts)
        a3 = a.reshape(b, t, 3 * W_A)
        o_g, l_g = [], []
        for gi, (window, dil) in enumerate(DIL_CONFIGS):
            o, lse = _dil_prompt(a3, gi, dil)
            o_g.append(o.reshape(b * t, W_G))
            l_g.append(lse.reshape(b * t, W_G))
            keep = min(window, t)
            k_last = a3[:, t - keep:, W_A + gi * W_G:W_A + (gi + 1) * W_G]
            v_last = a3[:, t - keep:, 2 * W_A + gi * W_G:2 * W_A + (gi + 1) * W_G]
            kv_p[gi].append(jnp.stack([k_last, v_last], axis=2).reshape(b, keep, 2, HEADS_PER_GROUP, HEAD_DIM_A))
        og, s_fin = _gla_prompt(qb.reshape(b, t, W_BK), kb.reshape(b, t, W_BK), vb.reshape(b, t, W_BV),
                                gd.reshape(b, t, W_BK))
        gla_p.append(s_fin)
        x1, h2, s1t, s2t = _merge(o_g, l_g, og.reshape(b * t, W_BV), rb, ga, gb, yp, mods_p[2], mods_p[3],
                                  mods_p[4], map_p, tm_p, wts, alpha)
        yp = _ffn_and_norm(x1, h2, s1t, s2t, mods_p[5], map_p, tm_p, wts, u_rows, v_rows, alpha)

        a, qb, kb, vb, rb, gd, ga, gb = _inproj(ys, mods_s[0], mods_s[1], map_s, tm_s, wts)
        layer_caches = [c[l] for c in caches]
        o, lse, new_caches = _dil_sample(a, layer_caches)
        for gi in range(N_GROUPS):
            kv_s[gi].append(new_caches[gi])
        og, s_new = _gla_sample(qb, kb, vb, gd, state_gla[l])
        gla_s.append(s_new)
        o_g = [o[:, gi * W_G:(gi + 1) * W_G] for gi in range(N_GROUPS)]
        l_g = [lse[:, gi * W_G:(gi + 1) * W_G] for gi in range(N_GROUPS)]
        x1, h2, s1t, s2t = _merge(o_g, l_g, og, rb, ga, gb, ys, mods_s[2], mods_s[3], mods_s[4], map_s, tm_s,
                                  wts, alpha)
        ys = _ffn_and_norm(x1, h2, s1t, s2t, mods_s[5], map_s, tm_s, wts, u_rows, v_rows, alpha)

    return (yp.reshape(b, t, d), ys.reshape(db, ds, d),
            jnp.stack(kv_p[0]), jnp.stack(kv_p[1]), jnp.stack(kv_p[2]), jnp.stack(gla_p),
            jnp.stack(kv_s[0]), jnp.stack(kv_s[1]), jnp.stack(kv_s[2]), jnp.stack(gla_s))
```
